```python
import jax, jax.numpy as jnp
from jax import lax
import numpy as np

D_MODEL = 1024
BATCH = 4
SEQ = 8192
DEPTH = 2

CHUNK = 64
HEAD_DIM = 64
D_MIX = D_MODEL
DA = 3 * D_MIX // 8
DC = D_MIX // 4
DB = D_MIX - DA - DC
HA = DA // HEAD_DIM
HB = DB // HEAD_DIM
CONV_K = 4
LORA_W = 64
LORA_A = 64
LORA_G = 128
LORA_V = 32
RWKV_SIZES = (DB, DB, DB, LORA_W, LORA_A, LORA_G)
RWKV_COLS = sum(RWKV_SIZES)
S5_GC = 16
S5_G = DC // S5_GC
S5_P = 64
IN_SIZES = (DA, DA, DA, HA, HA, RWKV_COLS, DC)
IN_COLS = sum(IN_SIZES)
N_GROUPS = 4
EXPERTS_PER_GROUP = 8
N_EXPERTS = N_GROUPS * EXPERTS_PER_GROUP
TOP_K = 2
D_EXPERT = D_MODEL // 2
ROW_BLOCK = 128
NORM_EPS = 1e-6
HEAD_NORM_EPS = 1e-5
RWKV_GN_EPS = 64e-5
L2_EPS = 1e-12

IN_SPLITS = [int(s) for s in np.cumsum(IN_SIZES)[:-1]]
RWKV_SPLITS = [int(s) for s in np.cumsum(RWKV_SIZES)[:-1]]

kernel_name = 'hybrid_mlstm_rwkv7_s5_hmoe_adaln'


def rmsnorm(x, g):
    x32 = x.astype(jnp.float32)
    y = x32 * lax.rsqrt(jnp.mean(x32 * x32, axis=-1, keepdims=True) + NORM_EPS)
    return (y * g.astype(jnp.float32)).astype(x.dtype)


def head_layernorm(h, eps):
    mu = jnp.mean(h, axis=-1, keepdims=True)
    var = jnp.mean(jnp.square(h - mu), axis=-1, keepdims=True)
    return (h - mu) * lax.rsqrt(var + eps)


def causal_conv(x, w, b):
    k_len = w.shape[0]
    s_len = x.shape[1]
    xp = jnp.pad(x, ((0, 0), (k_len - 1, 0), (0, 0)))
    return sum(xp[:, j:j + s_len] * w[j] for j in range(k_len)) + b


def mlstm_mixer(qk_pre, v_pre, o_pre, i_pre, f_pre, conv_w, conv_b, w_q, w_k, b_i, b_f, norm_w):
    f32 = jnp.float32
    bsz, s_len, _ = v_pre.shape
    nc = s_len // CHUNK
    cx = jax.nn.silu(causal_conv(qk_pre, conv_w, conv_b)).reshape(bsz, s_len, HA, HEAD_DIM)
    q = jnp.einsum('bshd,hde->bshe', cx, w_q)
    k = jnp.einsum('bshd,hde->bshe', cx, w_k) * (HEAD_DIM ** -0.5)
    v = v_pre.reshape(bsz, s_len, HA, HEAD_DIM)

    def to_chunks(t):
        return t.astype(f32).reshape(bsz, nc, CHUNK, HA, HEAD_DIM).transpose(0, 3, 1, 2, 4)

    def gate_chunks(t):
        return t.astype(f32).reshape(bsz, nc, CHUNK, HA).transpose(0, 3, 1, 2)

    qc, kc, vc = to_chunks(q), to_chunks(k), to_chunks(v)
    ig = gate_chunks(i_pre + b_i)
    flog = jax.nn.log_sigmoid(gate_chunks(f_pre + b_f))
    bcum = jnp.cumsum(flog, axis=-1)
    gtot = bcum[..., -1]

    lw = gtot[..., None] - bcum + ig
    m_loc = jnp.max(lw, axis=-1)
    wloc = jnp.exp(lw - m_loc[..., None])
    c_loc = jnp.einsum('bhcsk,bhcsv->bhckv', kc * wloc[..., None], vc)
    n_loc = jnp.einsum('bhcs,bhcsk->bhck', wloc, kc)

    def step(carry, inp):
        c_prev, n_prev, m_prev = carry
        g_c, ml_c, cl_c, nl_c = inp
        m_new = jnp.maximum(g_c + m_prev, ml_c)
        a_old = jnp.exp(g_c + m_prev - m_new)
        a_loc = jnp.exp(ml_c - m_new)
        c_new = a_old[..., None, None] * c_prev + a_loc[..., None, None] * cl_c
        n_new = a_old[..., None] * n_prev + a_loc[..., None] * nl_c
        return (c_new, n_new, m_new), (c_prev, n_prev, m_prev)

    init = (jnp.zeros((bsz, HA, HEAD_DIM, HEAD_DIM), f32),
            jnp.zeros((bsz, HA, HEAD_DIM), f32),
            jnp.zeros((bsz, HA), f32))
    xs = tuple(jnp.moveaxis(t, 2, 0) for t in (gtot, m_loc, c_loc, n_loc))
    _, (c_st, n_st, m_st) = lax.scan(step, init, xs)
    c_st, n_st, m_st = (jnp.moveaxis(t, 0, 2) for t in (c_st, n_st, m_st))

    causal = jnp.tril(jnp.ones((CHUNK, CHUNK), dtype=bool))
    dmat = jnp.where(causal, bcum[..., :, None] - bcum[..., None, :] + ig[..., None, :], -jnp.inf)
    inter = bcum + m_st[..., None]
    m_t = jnp.maximum(inter, jnp.max(dmat, axis=-1))
    s_qk = jnp.einsum('bhctd,bhcsd->bhcts', qc, kc) * jnp.exp(dmat - m_t[..., None])
    w_inter = jnp.exp(inter - m_t)
    num = (w_inter[..., None] * jnp.einsum('bhctk,bhckv->bhctv', qc, c_st)
           + jnp.einsum('bhcts,bhcsv->bhctv', s_qk, vc))
    den = w_inter * jnp.einsum('bhctk,bhck->bhct', qc, n_st) + jnp.sum(s_qk, axis=-1)
    h = num / jnp.maximum(jnp.abs(den), jnp.exp(-m_t))[..., None]
    h = h.transpose(0, 2, 3, 1, 4).reshape(bsz, s_len, HA, HEAD_DIM)
    h = head_layernorm(h, HEAD_NORM_EPS).reshape(bsz, s_len, DA) * norm_w.astype(f32)
    return h * jax.nn.sigmoid(o_pre.astype(f32))


def rwkv7_mixer(p, mu, w0, w_up, a0, a_up, g_up, k_k, k_a, r_k, ln_w, ln_b,
                v_first=None, v0=None, v_dn=None, v_up=None):
    f32 = jnp.float32
    bsz, s_len, _ = p.shape
    prev = jnp.pad(p, ((0, 0), (1, 0), (0, 0)))[:, :s_len]
    p = p + mu * (prev - p)
    r, k, v, wd, ad, gd = jnp.split(p, RWKV_SPLITS, axis=-1)
    wlog = -jax.nn.softplus(-(w0 + jnp.tanh(wd) @ w_up).astype(f32)) - 0.5
    decay = jnp.exp(-jnp.exp(wlog))
    a = jax.nn.sigmoid((a0 + ad @ a_up).astype(f32))
    g = jax.nn.sigmoid(gd) @ g_up
    if v_first is None:
        v_first = v
    else:
        v = v + (v_first - v) * jax.nn.sigmoid(v0 + (v @ v_dn) @ v_up)

    def heads(t):
        return t.astype(f32).reshape(bsz, s_len, HB, HEAD_DIM)

    r_h, k_h, v_h, a_h, w_h = heads(r), heads(k), heads(v), heads(a), heads(decay)
    kk = k_h * k_k.astype(f32).reshape(HB, HEAD_DIM)
    kk = kk / jnp.maximum(jnp.sqrt(jnp.sum(kk * kk, axis=-1, keepdims=True)), L2_EPS)
    k_h = k_h * (1.0 + (a_h - 1.0) * k_a.astype(f32).reshape(HB, HEAD_DIM))

    def tm(t):
        return jnp.moveaxis(t, 1, 0)

    xs = (tm(r_h), tm(w_h), tm(k_h), tm(v_h), tm(-kk), tm(kk * a_h))

    def step(state, inp):
        r_t, w_t, k_t, v_t, a_t, b_t = inp
        sa = jnp.einsum('bhvk,bhk->bhv', state, a_t)
        state = (state * w_t[:, :, None, :] + sa[..., None] * b_t[:, :, None, :]
                 + v_t[..., None] * k_t[:, :, None, :])
        return state, jnp.einsum('bhvk,bhk->bhv', state, r_t)

    _, y = lax.scan(step, jnp.zeros((bsz, HB, HEAD_DIM, HEAD_DIM), f32), xs)
    y = jnp.moveaxis(y, 0, 1)
    y = head_layernorm(y, RWKV_GN_EPS).reshape(bsz, s_len, DB) * ln_w.astype(f32) + ln_b.astype(f32)
    bonus = jnp.sum(r_h * k_h * r_k.astype(f32), axis=-1, keepdims=True) * v_h
    y = (y + bonus.reshape(bsz, s_len, DB)) * g.astype(f32)
    return y, v_first


def _complex_affine_combine(e1, e2):
    a1r, a1i, b1r, b1i = e1
    a2r, a2i, b2r, b2i = e2
    return (a2r * a1r - a2i * a1i,
            a2r * a1i + a2i * a1r,
            a2r * b1r - a2i * b1i + b2r,
            a2r * b1i + a2i * b1r + b2i)


def s5_mixer(u, a_re, a_im, log_dt, b_re, b_im, c_re, c_im, d, glu_w, glu_b):
    f32 = jnp.float32
    bsz, s_len, _ = u.shape
    uf = u.astype(f32).reshape(bsz, s_len, S5_G, S5_GC)
    a_re = a_re.astype(f32)
    a_im = a_im.astype(f32)
    dt = jnp.exp(log_dt.astype(f32))[:, None]
    mag = jnp.exp(a_re * dt)
    ang = a_im * dt
    ab_re = mag * jnp.cos(ang)
    ab_im = mag * jnp.sin(ang)
    inv = 1.0 / (a_re * a_re + a_im * a_im)
    co_re = ((ab_re - 1.0) * a_re + ab_im * a_im) * inv
    co_im = (ab_im * a_re - (ab_re - 1.0) * a_im) * inv
    b_re = b_re.astype(f32)
    b_im = b_im.astype(f32)
    bb_re = co_re[..., None] * b_re - co_im[..., None] * b_im
    bb_im = co_re[..., None] * b_im + co_im[..., None] * b_re
    bu_re = jnp.einsum('gpc,bsgc->bsgp', bb_re, uf)
    bu_im = jnp.einsum('gpc,bsgc->bsgp', bb_im, uf)
    shape_a = (1, s_len, S5_G, S5_P)
    elems = (jnp.broadcast_to(ab_re, shape_a), jnp.broadcast_to(ab_im, shape_a), bu_re, bu_im)
    _, _, x_re, x_im = lax.associative_scan(_complex_affine_combine, elems, axis=1)
    y = (jnp.einsum('gcp,bsgp->bsgc', c_re.astype(f32), x_re)
         - jnp.einsum('gcp,bsgp->bsgc', c_im.astype(f32), x_im)
         + d.astype(f32).reshape(S5_G, S5_GC) * uf)
    y = jax.nn.gelu(y.reshape(bsz, s_len, DC))
    return y * jax.nn.sigmoid(y @ glu_w.astype(f32) + glu_b.astype(f32))


def hier_moe(h, w_rg, b_rg, w_re, b_re, w1, w3, w2):
    f32 = jnp.float32
    bsz, s_len, d = h.shape
    n_tok = bsz * s_len
    ht = h.reshape(n_tok, d)
    p_group = jax.nn.softmax((ht @ w_rg + b_rg).astype(f32), axis=-1)
    gp, gi = lax.top_k(p_group, 1)
    e_logits = (ht @ w_re + b_re).astype(f32).reshape(n_tok, N_GROUPS, EXPERTS_PER_GROUP)
    sel = jnp.take_along_axis(e_logits, gi[:, :, None], axis=1)[:, 0]
    ev, ei = lax.top_k(sel, TOP_K)
    wts = gp * jax.nn.softmax(ev, axis=-1)

    n_assign = n_tok * TOP_K
    eid = (gi * EXPERTS_PER_GROUP + ei).reshape(n_assign)
    tok = jnp.repeat(jnp.arange(n_tok, dtype=jnp.int32), TOP_K)
    wflat = wts.reshape(n_assign)
    order = jnp.argsort(eid)
    e_s, tok_s, w_s = eid[order], tok[order], wflat[order]
    counts = jnp.zeros((N_EXPERTS,), jnp.int32).at[eid].add(1)
    starts = jnp.cumsum(counts) - counts
    pcounts = (counts + ROW_BLOCK - 1) // ROW_BLOCK * ROW_BLOCK
    pends = jnp.cumsum(pcounts)
    pstarts = pends - pcounts
    dest = pstarts[e_s] + jnp.arange(n_assign, dtype=jnp.int32) - starts[e_s]
    n_rows = n_assign + N_EXPERTS * ROW_BLOCK
    n_blocks = n_rows // ROW_BLOCK
    row_tok = jnp.full((n_rows,), n_tok, jnp.int32).at[dest].set(tok_s)
    row_w = jnp.zeros((n_rows,), f32).at[dest].set(w_s)
    blk_e = jnp.minimum(jnp.searchsorted(pends, jnp.arange(n_blocks, dtype=jnp.int32) * ROW_BLOCK,
                                         side='right'), N_EXPERTS - 1).astype(jnp.int32)
    x_pad = jnp.concatenate([ht, jnp.zeros((1, d), ht.dtype)], axis=0)
    xr = x_pad[row_tok].reshape(n_blocks, ROW_BLOCK, d)

    def expert_block(args):
        xb, e = args
        return (jax.nn.silu(xb @ w1[e]) * (xb @ w3[e])) @ w2[e]

    yr = lax.map(expert_block, (xr, blk_e)).reshape(n_rows, d)
    out = jnp.zeros((n_tok + 1, d), yr.dtype).at[row_tok].add(yr * row_w[:, None].astype(yr.dtype))
    return out[:n_tok].reshape(bsz, s_len, d)


def setup_inputs(seed: int = 0) -> dict:
    key = jax.random.key(seed)
    ks = iter(jax.random.split(key, 64))
    f32 = jnp.float32

    def nrm(shape, scale):
        return scale * jax.random.normal(next(ks), shape, f32)

    def unif(shape, lo, hi):
        return jax.random.uniform(next(ks), shape, f32, lo, hi)

    L = DEPTH
    LV = DEPTH - 1
    return {
        'x': nrm((BATCH, SEQ, D_MODEL), 1.0),
        'c': nrm((BATCH, D_MODEL), 1.0),
        'ada_w': nrm((L, D_MODEL, 6 * D_MODEL), 0.1 * D_MODEL ** -0.5),
        'ada_b': nrm((L, 6 * D_MODEL), 0.02),
        'norm_mix': 1.0 + nrm((L, D_MODEL), 0.02),
        'norm_ffn': 1.0 + nrm((L, D_MODEL), 0.02),
        'norm_final': 1.0 + nrm((D_MODEL,), 0.02),
        'w_in': nrm((L, D_MODEL, IN_COLS), D_MODEL ** -0.5),
        'w_out': nrm((L, D_MIX, D_MODEL), D_MIX ** -0.5),
        'mlstm_conv_w': nrm((L, CONV_K, DA), CONV_K ** -0.5),
        'mlstm_conv_b': nrm((L, DA), 0.02),
        'mlstm_w_q': nrm((L, HA, HEAD_DIM, HEAD_DIM), HEAD_DIM ** -0.5),
        'mlstm_w_k': nrm((L, HA, HEAD_DIM, HEAD_DIM), HEAD_DIM ** -0.5),
        'mlstm_b_i': nrm((L, HA), 0.1),
        'mlstm_b_f': jnp.linspace(3.0, 6.0, HA, dtype=f32)[None, :] + nrm((L, HA), 0.1),
        'mlstm_norm_w': 1.0 + nrm((L, DA), 0.02),
        'rwkv_mu': unif((L, RWKV_COLS), 0.0, 1.0),
        'rwkv_w0': jnp.linspace(-6.0, 1.0, DB, dtype=f32)[None, :] + nrm((L, DB), 0.1),
        'rwkv_w_up': nrm((L, LORA_W, DB), 0.1),
        'rwkv_a0': nrm((L, DB), 0.1),
        'rwkv_a_up': nrm((L, LORA_A, DB), 0.5 * LORA_A ** -0.5),
        'rwkv_g_up': nrm((L, LORA_G, DB), LORA_G ** -0.5),
        'rwkv_k_k': 0.85 + nrm((L, DB), 0.02),
        'rwkv_k_a': 1.0 + nrm((L, DB), 0.02),
        'rwkv_r_k': nrm((L, HB, HEAD_DIM), 0.1),
        'rwkv_ln_w': 1.0 + nrm((L, DB), 0.02),
        'rwkv_ln_b': nrm((L, DB), 0.02),
        'rwkv_v0': 1.0 + nrm((LV, DB), 0.1),
        'rwkv_v_dn': nrm((LV, DB, LORA_V), DB ** -0.5),
        'rwkv_v_up': nrm((LV, LORA_V, DB), 0.1 * LORA_V ** -0.5),
        's5_a_re': -0.5 + nrm((L, S5_G, S5_P), 0.01),
        's5_a_im': jnp.pi * jnp.arange(S5_P, dtype=f32) + nrm((L, S5_G, S5_P), 0.01),
        's5_log_dt': unif((L, S5_G), float(np.log(1e-3)), float(np.log(1e-1))),
        's5_b_re': nrm((L, S5_G, S5_P, S5_GC), (2 * S5_GC) ** -0.5),
        's5_b_im': nrm((L, S5_G, S5_P, S5_GC), (2 * S5_GC) ** -0.5),
        's5_c_re': nrm((L, S5_G, S5_GC, S5_P), S5_P ** -0.5),
        's5_c_im': nrm((L, S5_G, S5_GC, S5_P), S5_P ** -0.5),
        's5_d': nrm((L, DC), 1.0),
        's5_glu_w': nrm((L, DC, DC), DC ** -0.5),
        's5_glu_b': nrm((L, DC), 0.02),
        'moe_w_rg': nrm((L, D_MODEL, N_GROUPS), D_MODEL ** -0.5),
        'moe_b_rg': nrm((L, N_GROUPS), 0.01),
        'moe_w_re': nrm((L, D_MODEL, N_EXPERTS), D_MODEL ** -0.5),
        'moe_b_re': nrm((L, N_EXPERTS), 0.01),
        'moe_w1': nrm((L, N_EXPERTS, D_MODEL, D_EXPERT), D_MODEL ** -0.5),
        'moe_w3': nrm((L, N_EXPERTS, D_MODEL, D_EXPERT), D_MODEL ** -0.5),
        'moe_w2': nrm((L, N_EXPERTS, D_EXPERT, D_MODEL), D_EXPERT ** -0.5),
    }


def reference(x, c, ada_w, ada_b, norm_mix, norm_ffn, norm_final, w_in, w_out,
              mlstm_conv_w, mlstm_conv_b, mlstm_w_q, mlstm_w_k, mlstm_b_i, mlstm_b_f, mlstm_norm_w,
              rwkv_mu, rwkv_w0, rwkv_w_up, rwkv_a0, rwkv_a_up, rwkv_g_up, rwkv_k_k, rwkv_k_a,
              rwkv_r_k, rwkv_ln_w, rwkv_ln_b, rwkv_v0, rwkv_v_dn, rwkv_v_up,
              s5_a_re, s5_a_im, s5_log_dt, s5_b_re, s5_b_im, s5_c_re, s5_c_im, s5_d,
              s5_glu_w, s5_glu_b,
              moe_w_rg, moe_b_rg, moe_w_re, moe_b_re, moe_w1, moe_w3, moe_w2):
    out_dtype = x.dtype
    cond = jax.nn.silu(c)
    v_first = None
    for l in range(DEPTH):
        mod = (cond @ ada_w[l] + ada_b[l])[:, None, :]
        sh1, sc1, gt1, sh2, sc2, gt2 = jnp.split(mod, 6, axis=-1)

        h = rmsnorm(x, norm_mix[l]) * (1.0 + sc1) + sh1
        proj = h @ w_in[l]
        a_qk, a_v, a_o, a_i, a_f, b_cols, c_u = jnp.split(proj, IN_SPLITS, axis=-1)
        y_a = mlstm_mixer(a_qk, a_v, a_o, a_i, a_f, mlstm_conv_w[l], mlstm_conv_b[l],
                          mlstm_w_q[l], mlstm_w_k[l], mlstm_b_i[l], mlstm_b_f[l], mlstm_norm_w[l])
        if l == 0:
            y_b, v_first = rwkv7_mixer(b_cols, rwkv_mu[l], rwkv_w0[l], rwkv_w_up[l], rwkv_a0[l],
                                       rwkv_a_up[l], rwkv_g_up[l], rwkv_k_k[l], rwkv_k_a[l],
                                       rwkv_r_k[l], rwkv_ln_w[l], rwkv_ln_b[l])
        else:
            y_b, _ = rwkv7_mixer(b_cols, rwkv_mu[l], rwkv_w0[l], rwkv_w_up[l], rwkv_a0[l],
                                 rwkv_a_up[l], rwkv_g_up[l], rwkv_k_k[l], rwkv_k_a[l],
                                 rwkv_r_k[l], rwkv_ln_w[l], rwkv_ln_b[l],
                                 v_first, rwkv_v0[l - 1], rwkv_v_dn[l - 1], rwkv_v_up[l - 1])
        y_c = s5_mixer(c_u, s5_a_re[l], s5_a_im[l], s5_log_dt[l], s5_b_re[l], s5_b_im[l],
                       s5_c_re[l], s5_c_im[l], s5_d[l], s5_glu_w[l], s5_glu_b[l])
        mixed = jnp.concatenate([y_a.astype(h.dtype), y_b.astype(h.dtype), y_c.astype(h.dtype)],
                                axis=-1) @ w_out[l]
        x = x + (1.0 + gt1) * mixed

        h = rmsnorm(x, norm_ffn[l]) * (1.0 + sc2) + sh2
        x = x + (1.0 + gt2) * hier_moe(h, moe_w_rg[l], moe_b_rg[l], moe_w_re[l], moe_b_re[l],
                                       moe_w1[l], moe_w3[l], moe_w2[l])
    return rmsnorm(x, norm_final).astype(out_dtype)
```

```python
import functools

import jax
import jax.numpy as jnp
from jax import lax
from jax.experimental import pallas as pl
from jax.experimental.pallas import tpu as pltpu

F32 = jnp.float32
BF16 = jnp.bfloat16

HEAD_DIM = 64
CHUNK = 64
CONV_K = 4
S5_GC = 16
S5_P = 64
N_GROUPS = 4
EXPERTS_PER_GROUP = 8
N_EXPERTS = N_GROUPS * EXPERTS_PER_GROUP
NORM_EPS = 1e-6
HEAD_NORM_EPS = 1e-5
RWKV_GN_EPS = 64e-5
L2_EPS = 1e-12
LANES = 128
SUBLANES = 8
MOE_ROWS = 256
VMEM_LIMIT = 56 * 1024 * 1024


def _cparams(*sem):
    return pltpu.CompilerParams(dimension_semantics=sem, vmem_limit_bytes=VMEM_LIMIT)


def _row_tile(n, want):
    t = min(n, want)
    assert n % t == 0
    return t


def _dot(a, b):
    return jnp.dot(a.astype(BF16), b.astype(BF16), preferred_element_type=F32)


def _dot_nt(a, b):
    return lax.dot_general(a.astype(BF16), b.astype(BF16), (((1,), (1,)), ((), ())),
                           preferred_element_type=F32)


def _dot_tn(a, b):
    return lax.dot_general(a.astype(BF16), b.astype(BF16), (((0,), (0,)), ((), ())),
                           preferred_element_type=F32)


def _split(a):
    hi = a.astype(BF16)
    lo = (a - hi.astype(F32)).astype(BF16)
    return hi, lo


def _dot_xa(a, b_exact):
    hi, lo = _split(a)
    return (jnp.dot(hi, b_exact, preferred_element_type=F32)
            + jnp.dot(lo, b_exact, preferred_element_type=F32))


def _dot_xb(a_exact, b):
    hi, lo = _split(b)
    return (jnp.dot(a_exact, hi, preferred_element_type=F32)
            + jnp.dot(a_exact, lo, preferred_element_type=F32))


def _dot3(a, b):
    ah, al = _split(a)
    bh, bl = _split(b)
    return (jnp.dot(ah, bh, preferred_element_type=F32)
            + jnp.dot(ah, bl, preferred_element_type=F32)
            + jnp.dot(al, bh, preferred_element_type=F32))


def _dot3_nt(a, b):
    ah, al = _split(a)
    bh, bl = _split(b)
    dn = (((1,), (1,)), ((), ()))
    return (lax.dot_general(ah, bh, dn, preferred_element_type=F32)
            + lax.dot_general(ah, bl, dn, preferred_element_type=F32)
            + lax.dot_general(al, bh, dn, preferred_element_type=F32))


def _sigmoid(x):
    return 1.0 / (1.0 + jnp.exp(-x))


def _silu(x):
    return x * _sigmoid(x)


def _log_sigmoid(x):
    return jnp.minimum(x, 0.0) - jnp.log1p(jnp.exp(-jnp.abs(x)))


def _rmsnorm(x, g):
    ms = jnp.mean(x * x, axis=-1, keepdims=True)
    return x * lax.rsqrt(ms + NORM_EPS) * g


def _tri_incl(n):
    r = lax.broadcasted_iota(jnp.int32, (n, n), 0)
    c = lax.broadcasted_iota(jnp.int32, (n, n), 1)
    return (c <= r).astype(BF16)


def _head_ones(width):
    r = lax.broadcasted_iota(jnp.int32, (width, width), 0) // HEAD_DIM
    c = lax.broadcasted_iota(jnp.int32, (width, width), 1) // HEAD_DIM
    return (r == c).astype(BF16)


def _mod_kernel(c_ref, w_ref, b_ref, o_ref):
    o_ref[0] = _dot(_silu(c_ref[...]), w_ref[0]) + b_ref[0]


def _modulation(c, ada_w, ada_b):
    depth, d, d6 = ada_w.shape
    bsz = c.shape[0]
    tn = _row_tile(d6, 1024)
    return pl.pallas_call(
        _mod_kernel,
        grid=(depth, d6 // tn),
        in_specs=[pl.BlockSpec((bsz, d), lambda l, j: (0, 0)),
                  pl.BlockSpec((1, d, tn), lambda l, j: (l, 0, j)),
                  pl.BlockSpec((1, 1, tn), lambda l, j: (l, 0, j))],
        out_specs=pl.BlockSpec((1, bsz, tn), lambda l, j: (l, 0, j)),
        out_shape=jax.ShapeDtypeStruct((depth, bsz, d6), F32),
        compiler_params=_cparams("parallel", "parallel"),
        name="adaln_mod",
    )(c, ada_w, ada_b.reshape(depth, 1, d6))


def _in_kernel(x_ref, sh_ref, sc_ref, g_ref, w_ref, *out_refs, widths):
    h = _rmsnorm(x_ref[...], g_ref[...]) * (1.0 + sc_ref[0, 0]) + sh_ref[0, 0]
    hb = h.astype(BF16)
    off = 0
    for o_ref, wd in zip(out_refs, widths):
        o_ref[...] = jnp.dot(hb, w_ref[:, off:off + wd], preferred_element_type=F32)
        off += wd


def _in_proj(x2d, mod_l, g, w_pad, widths, seq):
    n_tok, d = x2d.shape
    tm = _row_tile(seq, 512)
    per_b = seq // tm
    tot = sum(widths)
    return pl.pallas_call(
        functools.partial(_in_kernel, widths=widths),
        grid=(n_tok // tm,),
        in_specs=[pl.BlockSpec((tm, d), lambda i: (i, 0)),
                  pl.BlockSpec((1, 1, 1, d), lambda i: (i // per_b, 0, 0, 0)),
                  pl.BlockSpec((1, 1, 1, d), lambda i: (i // per_b, 1, 0, 0)),
                  pl.BlockSpec((1, d), lambda i: (0, 0)),
                  pl.BlockSpec((d, tot), lambda i: (0, 0))],
        out_specs=[pl.BlockSpec((tm, wd), lambda i: (i, 0)) for wd in widths],
        out_shape=[jax.ShapeDtypeStruct((n_tok, wd), F32) for wd in widths],
        compiler_params=_cparams("parallel"),
        name="in_proj",
    )(x2d, mod_l, mod_l, g, w_pad)


def _mlstm_kernel(qkvo_ref, gate_ref, cw_ref, cb_ref, wq_ref, wk_ref, gb_ref, nw_ref, out_ref,
                  xf_sc, q_sc, k_sc, g_sc, c_sc, n_sc, m_sc, *, tb, heads):
    dh, L = HEAD_DIM, CHUNK
    da = heads * dh
    i = pl.program_id(1)

    @pl.when(i == 0)
    def _():
        xf_sc[0:SUBLANES, :] = jnp.zeros((SUBLANES, da), F32)
        c_sc[...] = jnp.zeros_like(c_sc)
        n_sc[...] = jnp.zeros_like(n_sc)
        m_sc[...] = jnp.zeros_like(m_sc)

    xqk = qkvo_ref[:, 0:da]
    xf_sc[pl.ds(SUBLANES, tb), :] = xqk
    acc = xqk * cw_ref[CONV_K - 1:CONV_K, :] + cb_ref[...]
    for j in range(1, CONV_K):
        acc = acc + xf_sc[pl.ds(SUBLANES - j, tb), :] * cw_ref[CONV_K - 1 - j:CONV_K - j, :]
    xf_sc[0:SUBLANES, :] = xf_sc[pl.ds(tb, SUBLANES), :]
    cx = _silu(acc).astype(BF16)
    q_sc[...] = jnp.dot(cx, wq_ref[...], preferred_element_type=F32)
    k_sc[...] = jnp.dot(cx, wk_ref[...], preferred_element_type=F32) * (dh ** -0.5)

    g = gate_ref[...] + gb_ref[...]
    lane = lax.broadcasted_iota(jnp.int32, g.shape, 1)
    is_f = (lane >= heads) & (lane < 2 * heads)
    g_sc[...] = jnp.where(is_f, _log_sigmoid(g), g)

    tri = _tri_incl(L)
    rr = lax.broadcasted_iota(jnp.int32, (L, L), 0)
    cc = lax.broadcasted_iota(jnp.int32, (L, L), 1)
    causal = cc <= rr
    lane_c = lax.broadcasted_iota(jnp.int32, (L, LANES), 1)

    def chunk(c, carry):
        r0 = pl.multiple_of(c * L, L)
        gc = g_sc[pl.ds(r0, L), :]
        bc = _dot_xb(tri, gc)
        zt = jnp.where(lane_c < heads, gc, bc).T
        outs = []
        for h in range(heads):
            hs = slice(h * dh, (h + 1) * dh)
            qc = q_sc[pl.ds(r0, L), hs]
            kc = k_sc[pl.ds(r0, L), hs]
            vc = qkvo_ref[pl.ds(r0, L), da + h * dh:da + (h + 1) * dh]
            oc = qkvo_ref[pl.ds(r0, L), 2 * da + h * dh:2 * da + (h + 1) * dh]
            ig_col = gc[:, h:h + 1]
            bc_col = bc[:, heads + h:heads + h + 1]
            ig_row = zt[h:h + 1, :]
            bc_row = zt[heads + h:heads + h + 1, :]
            gtot = bc[L - 1:L, heads + h:heads + h + 1]
            c_prev = c_sc[h]
            n_prev = n_sc[h]
            m_prev = m_sc[h][:, 0:1]

            lw = gtot - bc_col + ig_col
            m_loc = jnp.max(lw, axis=0, keepdims=True)
            kw = kc * jnp.exp(lw - m_loc)
            c_loc = _dot_tn(kw, vc)
            n_loc = jnp.sum(kw, axis=0, keepdims=True)
            m_new = jnp.maximum(gtot + m_prev, m_loc)
            a_old = jnp.exp(gtot + m_prev - m_new)
            a_loc = jnp.exp(m_loc - m_new)
            c_sc[h] = a_old * c_prev + a_loc * c_loc
            n_sc[h] = a_old * n_prev + a_loc * n_loc
            m_sc[h] = jnp.broadcast_to(m_new, (1, dh))

            dmat = jnp.where(causal, bc_col - bc_row + ig_row, -jnp.inf)
            inter = bc_col + m_prev
            m_t = jnp.maximum(inter, jnp.max(dmat, axis=1, keepdims=True))
            s_qk = _dot_nt(qc, kc) * jnp.exp(dmat - m_t)
            w_inter = jnp.exp(inter - m_t)
            num = w_inter * _dot(qc, c_prev) + _dot(s_qk, vc)
            den = (w_inter * jnp.sum(qc * n_prev, axis=1, keepdims=True)
                   + jnp.sum(s_qk, axis=1, keepdims=True))
            hh = num / jnp.maximum(jnp.abs(den), jnp.exp(-m_t))
            mu = jnp.mean(hh, axis=1, keepdims=True)
            dlt = hh - mu
            var = jnp.mean(dlt * dlt, axis=1, keepdims=True)
            outs.append(dlt * lax.rsqrt(var + HEAD_NORM_EPS) * _sigmoid(oc))
        out_ref[pl.ds(r0, L), :] = jnp.concatenate(outs, axis=1) * nw_ref[...]
        return carry

    lax.fori_loop(0, tb // L, chunk, 0)


def _block_diag(w):
    heads, dh, _ = w.shape
    eye = jnp.eye(heads, dtype=w.dtype)
    return (eye[:, None, :, None] * w[:, :, None, :]).reshape(heads * dh, heads * dh)


def _mlstm(qkvo, gates, conv_w, conv_b, w_q, w_k, b_i, b_f, norm_w, bsz, seq):
    heads = w_q.shape[0]
    da = heads * HEAD_DIM
    tb = _row_tile(seq, 512)
    nblk = seq // tb
    gbias = jnp.zeros((1, LANES), F32).at[0, :heads].set(b_i).at[0, heads:2 * heads].set(b_f)
    kern = functools.partial(_mlstm_kernel, tb=tb, heads=heads)
    const = lambda b, i: (0, 0)
    return pl.pallas_call(
        kern,
        grid=(bsz, nblk),
        in_specs=[pl.BlockSpec((tb, 3 * da), lambda b, i: (b * nblk + i, 0)),
                  pl.BlockSpec((tb, LANES), lambda b, i: (b * nblk + i, 0)),
                  pl.BlockSpec((CONV_K, da), const),
                  pl.BlockSpec((1, da), const),
                  pl.BlockSpec((da, da), const),
                  pl.BlockSpec((da, da), const),
                  pl.BlockSpec((1, LANES), const),
                  pl.BlockSpec((1, da), const)],
        out_specs=pl.BlockSpec((tb, da), lambda b, i: (b * nblk + i, 0)),
        out_shape=jax.ShapeDtypeStruct((bsz * seq, da), F32),
        scratch_shapes=[pltpu.VMEM((tb + SUBLANES, da), F32),
                        pltpu.VMEM((tb, da), F32),
                        pltpu.VMEM((tb, da), F32),
                        pltpu.VMEM((tb, LANES), F32),
                        pltpu.VMEM((heads, HEAD_DIM, HEAD_DIM), F32),
                        pltpu.VMEM((heads, 1, HEAD_DIM), F32),
                        pltpu.VMEM((heads, 1, HEAD_DIM), F32)],
        compiler_params=_cparams("parallel", "arbitrary"),
        name="mlstm",
    )(qkvo, gates, conv_w, conv_b.reshape(1, da), _block_diag(w_q).astype(BF16),
      _block_diag(w_k).astype(BF16), gbias, norm_w.reshape(1, da))


def _rwkv_kernel(*refs, tb, heads, lw_dim, la_dim, lg_dim, has_vres):
    dh, L = HEAD_DIM, CHUNK
    db = heads * dh
    it = iter(refs)
    p_ref = next(it)
    vf_ref = next(it) if has_vres else None
    (mu_ref, w0_ref, wup_ref, a0_ref, aup_ref, gup_ref, kk_ref, ka_ref, rk_ref,
     lnw_ref, lnb_ref) = (next(it) for _ in range(11))
    if has_vres:
        v0_ref, vdn_ref, vup_ref = (next(it) for _ in range(3))
    y_ref = next(it)
    vout_ref = None if has_vres else next(it)
    (xf_sc, r_sc, k_sc, v_sc, a_sc, b_sc, lw_sc, y_sc, st_sc) = (next(it) for _ in range(9))

    i = pl.program_id(1)

    @pl.when(i == 0)
    def _():
        xf_sc[0:SUBLANES, :] = jnp.zeros((SUBLANES, xf_sc.shape[1]), F32)
        st_sc[...] = jnp.zeros_like(st_sc)

    p = p_ref[...]
    xf_sc[pl.ds(SUBLANES, tb), :] = p
    prev = xf_sc[pl.ds(SUBLANES - 1, tb), :]
    xf_sc[0:SUBLANES, :] = xf_sc[pl.ds(tb, SUBLANES), :]
    p = p + mu_ref[...] * (prev - p)

    o = 0
    r = p[:, o:o + db]; o += db
    k = p[:, o:o + db]; o += db
    v = p[:, o:o + db]; o += db
    wd = p[:, o:o + lw_dim]; o += lw_dim
    ad = p[:, o:o + la_dim]; o += la_dim
    gd = p[:, o:o + lg_dim]

    ones_h = _head_ones(db)
    wlog = _log_sigmoid(w0_ref[...] + _dot(jnp.tanh(wd), wup_ref[...])) - 0.5
    lw_sc[...] = -jnp.exp(wlog)
    a = _sigmoid(a0_ref[...] + _dot(ad, aup_ref[...]))
    gate = _dot(_sigmoid(gd), gup_ref[...])
    if has_vres:
        v = v + (vf_ref[...] - v) * _sigmoid(v0_ref[...] + _dot(_dot(v, vdn_ref[...]), vup_ref[...]))
    else:
        vout_ref[...] = v
    kk = k * kk_ref[...]
    kk = kk / jnp.maximum(jnp.sqrt(_dot_xa(kk * kk, ones_h)), L2_EPS)
    k2 = k * (1.0 + (a - 1.0) * ka_ref[...])
    bonus = _dot_xa(r * k2 * rk_ref[...], ones_h) * v
    r_sc[...] = r
    k_sc[...] = k2
    v_sc[...] = v
    a_sc[...] = -kk
    b_sc[...] = kk * a

    tri = _tri_incl(L)
    rr = lax.broadcasted_iota(jnp.int32, (L, L), 0)
    cc = lax.broadcasted_iota(jnp.int32, (L, L), 1)
    strict = cc < rr
    incl = cc <= rr

    def chunk(c, carry):
        r0 = pl.multiple_of(c * L, L)
        rows = pl.ds(r0, L)
        lwc = lw_sc[rows, :]
        cum = _dot_xb(tri, lwc)
        cum_l = cum[L - 1:L, :]
        e_in = jnp.exp(cum)
        e_ex = jnp.exp(cum - lwc)
        e_inv = jnp.exp(-cum)
        e_end = jnp.exp(cum_l - cum)
        at = a_sc[rows, :] * e_ex
        rt = r_sc[rows, :] * e_in
        bv = b_sc[rows, :]
        kv = k_sc[rows, :]
        bt = bv * e_inv
        kt = kv * e_inv
        bg = bv * e_end
        kg = kv * e_end
        g_l = jnp.exp(cum_l)
        vch = v_sc[rows, :]
        for h in range(heads):
            hs = slice(h * dh, (h + 1) * dh)
            st = st_sc[h]
            ar = jnp.concatenate([at[:, hs], rt[:, hs]], axis=0)
            bk = jnp.concatenate([bt[:, hs], kt[:, hs]], axis=0)
            g4 = _dot_nt(ar, bk)
            n_ab = jnp.where(strict, g4[:L, :L], 0.0)
            n_ak = jnp.where(strict, g4[:L, L:], 0.0)
            m_rb = jnp.where(incl, g4[L:, :L], 0.0)
            m_rk = jnp.where(incl, g4[L:, L:], 0.0)
            ars = _dot_nt(ar, st)
            vh = vch[:, hs]
            x = ars[:L] + _dot(n_ak, vh)
            pw = n_ab
            for step in range(6):
                x = x + _dot(pw, x)
                if step < 5:
                    pw = _dot(pw, pw)
            uv = jnp.concatenate([x, vh], axis=0)
            y_sc[rows, hs] = ars[L:] + _dot(jnp.concatenate([m_rb, m_rk], axis=1), uv)
            st_sc[h] = st * g_l[:, hs] + _dot_tn(uv, jnp.concatenate([bg[:, hs], kg[:, hs]], axis=0))
        return carry

    lax.fori_loop(0, tb // L, chunk, 0)

    y = y_sc[...]
    mean = _dot_xa(y, ones_h) * (1.0 / dh)
    dlt = y - mean
    var = _dot_xa(dlt * dlt, ones_h) * (1.0 / dh)
    yn = dlt * lax.rsqrt(var + RWKV_GN_EPS) * lnw_ref[...] + lnb_ref[...]
    y_ref[...] = (yn + bonus) * gate


def _rwkv(pcols, v_first, prm, bsz, seq):
    db = prm["w0"].shape[0]
    heads = db // HEAD_DIM
    cols = pcols.shape[1]
    lw_dim, la_dim, lg_dim = prm["w_up"].shape[0], prm["a_up"].shape[0], prm["g_up"].shape[0]
    has_vres = v_first is not None
    tb = _row_tile(seq, 512)
    nblk = seq // tb
    row = lambda b, i: (b * nblk + i, 0)
    const = lambda b, i: (0, 0)
    vec = lambda a: a.reshape(1, -1).astype(F32)

    args = [pcols]
    specs = [pl.BlockSpec((tb, cols), row)]
    if has_vres:
        args.append(v_first)
        specs.append(pl.BlockSpec((tb, db), row))
    small = [vec(prm["mu"]), vec(prm["w0"]), prm["w_up"].astype(BF16), vec(prm["a0"]),
             prm["a_up"].astype(BF16), prm["g_up"].astype(BF16), vec(prm["k_k"]), vec(prm["k_a"]),
             vec(prm["r_k"]), vec(prm["ln_w"]), vec(prm["ln_b"])]
    if has_vres:
        lv = prm["v_dn"].shape[1]
        lvp = -(-lv // LANES) * LANES
        v_dn = jnp.zeros((db, lvp), F32).at[:, :lv].set(prm["v_dn"]).astype(BF16)
        v_up = jnp.zeros((lvp, db), F32).at[:lv, :].set(prm["v_up"]).astype(BF16)
        small += [vec(prm["v0"]), v_dn, v_up]
    args += small
    specs += [pl.BlockSpec(a.shape, const) for a in small]

    out_shape = [jax.ShapeDtypeStruct((bsz * seq, db), F32)]
    out_specs = [pl.BlockSpec((tb, db), row)]
    if not has_vres:
        out_shape.append(jax.ShapeDtypeStruct((bsz * seq, db), F32))
        out_specs.append(pl.BlockSpec((tb, db), row))

    kern = functools.partial(_rwkv_kernel, tb=tb, heads=heads, lw_dim=lw_dim, la_dim=la_dim,
                             lg_dim=lg_dim, has_vres=has_vres)
    res = pl.pallas_call(
        kern,
        grid=(bsz, nblk),
        in_specs=specs,
        out_specs=out_specs,
        out_shape=out_shape,
        scratch_shapes=[pltpu.VMEM((tb + SUBLANES, cols), F32)]
        + [pltpu.VMEM((tb, db), F32) for _ in range(7)]
        + [pltpu.VMEM((heads, HEAD_DIM, HEAD_DIM), F32)],
        compiler_params=_cparams("parallel", "arbitrary"),
        name="rwkv7",
    )(*args)
    return (res[0], v_first) if has_vres else (res[0], res[1])


def _cmul(ar, ai, br, bi):
    return ar * br - ai * bi, ar * bi + ai * br


def _shift_rows(x, d):
    row = lax.broadcasted_iota(jnp.int32, x.shape, 0)
    return jnp.where(row >= d, pltpu.roll(x, d, 0), 0.0)


def _s5_kernel(u_ref, are_r, aim_r, dt_r, are_c, aim_c, dt_c, bre_ref, bim_ref, cre_ref, cim_ref,
               y_ref, m_sc, *, bsz, nchunk):
    L, gc, P = CHUNK, S5_GC, S5_P
    n = L * gc
    a_re, a_im, dt = are_r[0], aim_r[0], jnp.exp(dt_r[0])
    mag, ang = jnp.exp(a_re * dt), a_im * dt
    ab_re, ab_im = mag * jnp.cos(ang), mag * jnp.sin(ang)
    inv = 1.0 / (a_re * a_re + a_im * a_im)
    co_re = ((ab_re - 1.0) * a_re + ab_im * a_im) * inv
    co_im = (ab_im * a_re - (ab_re - 1.0) * a_im) * inv
    bb_re, bb_im = _cmul(co_re, co_im, bre_ref[0], bim_ref[0])
    c_re, c_im = cre_ref[0], cim_ref[0]

    def powers(tau):
        m = jnp.exp(tau * (a_re * dt))
        return m * jnp.cos(tau * ang), m * jnp.sin(tau * ang)

    a_re_c, a_im_c, dt_c_ = are_c[0], aim_c[0], jnp.exp(dt_c[0])
    tau_row = lax.broadcasted_iota(jnp.int32, (P, L), 1).astype(F32)
    pm = jnp.exp(tau_row * (a_re_c * dt_c_))
    pt_re = pm * jnp.cos(tau_row * (a_im_c * dt_c_))
    pt_im = pm * jnp.sin(tau_row * (a_im_c * dt_c_))
    pair = lax.broadcasted_iota(jnp.int32, (gc * gc, gc), 0)
    col = lax.broadcasted_iota(jnp.int32, (gc * gc, gc), 1)
    rep_c = (pair // gc == col).astype(BF16)
    rep_b = (pair % gc == col).astype(BF16)
    cb_re, cb_im = _cmul(_dot_xb(rep_c, c_re), _dot_xb(rep_c, c_im),
                         _dot_xb(rep_b, bb_re), _dot_xb(rep_b, bb_im))
    kap = _dot3(cb_re, pt_re) - _dot3(cb_im, pt_im)

    kap_pad = jnp.concatenate([kap, jnp.zeros_like(kap)], axis=1)
    srow = lax.broadcasted_iota(jnp.int32, (L, LANES), 0)
    tcol = lax.broadcasted_iota(jnp.int32, (L, LANES), 1)
    for cp in range(gc):
        for c2 in range(0, gc, 2):
            k0 = jnp.broadcast_to(kap_pad[c2 * gc + cp:c2 * gc + cp + 1, :], (L, LANES))
            k1 = jnp.broadcast_to(kap_pad[(c2 + 1) * gc + cp:(c2 + 1) * gc + cp + 1, :], (L, LANES))
            t0 = pltpu.roll(k0, 0, 1, stride=1, stride_axis=0)
            t1 = pltpu.roll(k1, L, 1, stride=1, stride_axis=0)
            blk = jnp.where(tcol < L, jnp.where(tcol >= srow, t0, 0.0),
                            jnp.where(tcol - L >= srow, t1, 0.0))
            m_sc[cp * L:(cp + 1) * L, c2 * L:(c2 + 2) * L] = blk.astype(BF16)

    s_col = lax.broadcasted_iota(jnp.int32, (L, P), 0).astype(F32)
    pw_re, pw_im = powers((L - 1.0) - s_col)
    pg_re, pg_im = powers(s_col + 1.0)
    w_re, w_im, g_re, g_im = [], [], [], []
    for c in range(gc):
        br = jnp.broadcast_to(bb_re[c:c + 1, :], (L, P))
        bi = jnp.broadcast_to(bb_im[c:c + 1, :], (L, P))
        wr, wi = _cmul(br, bi, pw_re, pw_im)
        w_re.append(wr); w_im.append(wi)
        cr = jnp.broadcast_to(c_re[c:c + 1, :], (L, P))
        ci = jnp.broadcast_to(c_im[c:c + 1, :], (L, P))
        gr, gi = _cmul(cr, ci, pg_re, pg_im)
        g_re.append(gr); g_im.append(gi)
    w_re, w_im = jnp.concatenate(w_re, axis=0), jnp.concatenate(w_im, axis=0)
    g_re, g_im = jnp.concatenate(g_re, axis=0), jnp.concatenate(g_im, axis=0)

    u = u_ref[0].astype(BF16)
    x_re = jnp.dot(u, w_re.astype(BF16), preferred_element_type=F32)
    x_im = jnp.dot(u, w_im.astype(BF16), preferred_element_type=F32)
    xs_re, xs_im = [], []
    for b in range(bsz):
        xr = x_re[b * nchunk:(b + 1) * nchunk]
        xi = x_im[b * nchunk:(b + 1) * nchunk]
        d = 1
        while d < nchunk:
            ar_, ai_ = powers(float(L * d))
            sr, si = _cmul(ar_, ai_, _shift_rows(xr, d), _shift_rows(xi, d))
            xr, xi = xr + sr, xi + si
            d *= 2
        xs_re.append(_shift_rows(xr, 1))
        xs_im.append(_shift_rows(xi, 1))
    xs_re, xs_im = jnp.concatenate(xs_re, axis=0), jnp.concatenate(xs_im, axis=0)
    y = jnp.dot(u, m_sc[...], preferred_element_type=F32)
    y = y + _dot_nt(xs_re, g_re) - _dot_nt(xs_im, g_im)
    y_ref[0] = y


def _s5_core(u2d, prm, bsz, seq):
    L, gc, P = CHUNK, S5_GC, S5_P
    groups = u2d.shape[1] // gc
    nchunk = seq // L
    rows = bsz * nchunk
    n = L * gc
    ug = u2d.reshape(bsz, nchunk, L, groups, gc).transpose(3, 0, 1, 4, 2).reshape(groups, rows, n)
    row3 = lambda a: a.reshape(groups, 1, P).astype(F32)
    col3 = lambda a: a.reshape(groups, P, 1).astype(F32)
    dt_b = jnp.broadcast_to(prm["log_dt"][:, None], (groups, P))
    args = [ug, row3(prm["a_re"]), row3(prm["a_im"]), row3(dt_b),
            col3(prm["a_re"]), col3(prm["a_im"]), col3(dt_b),
            prm["b_re"].transpose(0, 2, 1), prm["b_im"].transpose(0, 2, 1),
            prm["c_re"], prm["c_im"]]
    g3 = lambda g: (g, 0, 0)
    specs = [pl.BlockSpec((1, rows, n), g3)]
    specs += [pl.BlockSpec((1, 1, P), g3)] * 3 + [pl.BlockSpec((1, P, 1), g3)] * 3
    specs += [pl.BlockSpec((1, gc, P), g3)] * 4
    yg = pl.pallas_call(
        functools.partial(_s5_kernel, bsz=bsz, nchunk=nchunk),
        grid=(groups,),
        in_specs=specs,
        out_specs=pl.BlockSpec((1, rows, n), g3),
        out_shape=jax.ShapeDtypeStruct((groups, rows, n), F32),
        scratch_shapes=[pltpu.VMEM((n, n), BF16)],
        compiler_params=_cparams("parallel"),
        name="s5_core",
    )(*args)
    return yg.reshape(groups, bsz, nchunk, gc, L).transpose(1, 2, 4, 0, 3).reshape(bsz * seq, groups * gc)


def _gelu_tanh(x):
    return 0.5 * x * (1.0 + jnp.tanh(0.7978845608028654 * (x + 0.044715 * x * x * x)))


def _out_kernel(ya_ref, yb_ref, yc_ref, u_ref, x_ref, gt1_ref, sh2_ref, sc2_ref, g_ref, d_ref,
                gw_ref, gbias_ref, wo_ref, wrh_ref, wrl_ref, br_ref,
                x1_ref, h2_ref, ri_ref, rw_ref, *, da, db):
    yc = _gelu_tanh(yc_ref[...] + d_ref[...] * u_ref[...])
    yc = yc * _sigmoid(_dot(yc, gw_ref[...]) + gbias_ref[...])
    mixed = (_dot(ya_ref[...], wo_ref[0:da, :]) + _dot(yb_ref[...], wo_ref[da:da + db, :])
             + _dot(yc, wo_ref[da + db:, :]))
    x1 = x_ref[...] + (1.0 + gt1_ref[0, 0]) * mixed
    x1_ref[...] = x1
    h2 = _rmsnorm(x1, g_ref[...]) * (1.0 + sc2_ref[0, 0]) + sh2_ref[0, 0]
    h2_ref[...] = h2

    hh, hl = _split(h2)
    logits = (jnp.dot(hh, wrh_ref[...], preferred_element_type=F32)
              + jnp.dot(hh, wrl_ref[...], preferred_element_type=F32)
              + jnp.dot(hl, wrh_ref[...], preferred_element_type=F32)) + br_ref[...]
    lane_i = lax.broadcasted_iota(jnp.int32, logits.shape, 1)
    lane = lane_i.astype(F32)
    big = float(LANES)
    neg = -jnp.inf
    is_g = lane_i < N_GROUPS
    lg = jnp.where(is_g, logits, neg)
    gmax = jnp.max(lg, axis=1, keepdims=True)
    gi = jnp.min(jnp.where(is_g & (lg == gmax), lane, big), axis=1, keepdims=True)
    gp = 1.0 / jnp.sum(jnp.where(is_g, jnp.exp(lg - gmax), 0.0), axis=1, keepdims=True)
    e_lane = lane_i - N_GROUPS
    grp_of_lane = lax.shift_right_arithmetic(e_lane, 3).astype(F32)
    in_grp = (e_lane >= 0) & (e_lane < N_EXPERTS) & (grp_of_lane == gi)
    l1 = jnp.where(in_grp, logits, neg)
    m1 = jnp.max(l1, axis=1, keepdims=True)
    i1 = jnp.min(jnp.where(in_grp & (l1 == m1), lane, big), axis=1, keepdims=True)
    rest = in_grp & (lane != i1)
    l2 = jnp.where(rest, logits, neg)
    m2 = jnp.max(l2, axis=1, keepdims=True)
    i2 = jnp.min(jnp.where(rest & (l2 == m2), lane, big), axis=1, keepdims=True)
    e2 = jnp.exp(m2 - m1)
    w1 = gp / (1.0 + e2)
    w2 = gp * e2 / (1.0 + e2)
    ids = jnp.where(lane_i == 0, i1 - N_GROUPS, jnp.where(lane_i == 1, i2 - N_GROUPS, 0.0))
    ri_ref[...] = ids.astype(jnp.int32)
    rw_ref[...] = jnp.where(lane_i == 0, w1, jnp.where(lane_i == 1, w2, 0.0))


def _out_proj(ya, yb, yc, u, x2d, mod_l, g_ffn, s5_d, glu_w, glu_b, w_out, w_rg, b_rg, w_re, b_re, seq):
    n_tok, d = x2d.shape
    da, db, dc = ya.shape[1], yb.shape[1], yc.shape[1]
    tm = _row_tile(seq, 512)
    per_b = seq // tm
    wr = jnp.zeros((d, LANES), F32).at[:, :N_GROUPS].set(w_rg).at[:, N_GROUPS:N_GROUPS + N_EXPERTS].set(w_re)
    wr_hi = wr.astype(BF16)
    wr_lo = (wr - wr_hi.astype(F32)).astype(BF16)
    br = jnp.zeros((1, LANES), F32).at[0, :N_GROUPS].set(b_rg).at[0, N_GROUPS:N_GROUPS + N_EXPERTS].set(b_re)
    row = lambda i: (i, 0)
    const = lambda i: (0, 0)
    modspec = lambda j: pl.BlockSpec((1, 1, 1, d), lambda i: (i // per_b, j, 0, 0))
    return pl.pallas_call(
        functools.partial(_out_kernel, da=da, db=db),
        grid=(n_tok // tm,),
        in_specs=[pl.BlockSpec((tm, da), row), pl.BlockSpec((tm, db), row), pl.BlockSpec((tm, dc), row),
                  pl.BlockSpec((tm, dc), row), pl.BlockSpec((tm, d), row),
                  modspec(2), modspec(3), modspec(4),
                  pl.BlockSpec((1, d), const), pl.BlockSpec((1, dc), const),
                  pl.BlockSpec((dc, dc), const), pl.BlockSpec((1, dc), const),
                  pl.BlockSpec((d, d), const), pl.BlockSpec((d, LANES), const),
                  pl.BlockSpec((d, LANES), const), pl.BlockSpec((1, LANES), const)],
        out_specs=[pl.BlockSpec((tm, d), row), pl.BlockSpec((tm, d), row),
                   pl.BlockSpec((tm, LANES), row), pl.BlockSpec((tm, LANES), row)],
        out_shape=[jax.ShapeDtypeStruct((n_tok, d), F32), jax.ShapeDtypeStruct((n_tok, d), F32),
                   jax.ShapeDtypeStruct((n_tok, LANES), jnp.int32),
                   jax.ShapeDtypeStruct((n_tok, LANES), F32)],
        compiler_params=_cparams("parallel"),
        name="out_proj_router",
    )(ya, yb, yc, u, x2d, mod_l, mod_l, mod_l, g_ffn, s5_d.reshape(1, dc), glu_w.astype(BF16),
      glu_b.reshape(1, dc), w_out.astype(BF16), wr_hi, wr_lo, br)


def _row_copy(src_hbm, row, dst, slot, r, sem):
    return pltpu.make_async_copy(src_hbm.at[pl.ds(row, 1), :], dst.at[slot, pl.ds(r, 1), :], sem.at[slot])


def _moe_kernel(blk_e_ref, tok_ref, tok_next_ref, roww_ref, h_hbm, w13_ref, w2_ref, y_ref,
                xbuf, sem, *, rows, d_exp):
    del blk_e_ref
    i = pl.program_id(0)
    n = pl.num_programs(0)
    slot = i % 2

    def start_all(idx_ref, s):
        def body(r, carry):
            _row_copy(h_hbm, idx_ref[0, 0, r], xbuf, s, r, sem).start()
            return carry
        lax.fori_loop(0, rows, body, 0)

    @pl.when(i == 0)
    def _():
        start_all(tok_ref, 0)

    @pl.when(i + 1 < n)
    def _():
        start_all(tok_next_ref, 1 - slot)

    def wait_body(r, carry):
        _row_copy(h_hbm, 0, xbuf, slot, r, sem).wait()
        return carry
    lax.fori_loop(0, rows, wait_body, 0)

    xb = xbuf[slot].astype(BF16)
    hcat = jnp.dot(xb, w13_ref[0], preferred_element_type=F32)
    act = _silu(hcat[:, :d_exp]) * hcat[:, d_exp:]
    y = jnp.dot(act.astype(BF16), w2_ref[0], preferred_element_type=F32)
    y_ref[...] = y * roww_ref[...]


def _moe_experts(h2, row_tok, row_w, blk_e, w13, w2):
    n_rows = row_tok.shape[0]
    d = h2.shape[1]
    d_exp = w2.shape[1]
    nblk = n_rows // MOE_ROWS
    tok3 = row_tok.reshape(nblk, 1, MOE_ROWS)
    grid_spec = pltpu.PrefetchScalarGridSpec(
        num_scalar_prefetch=1,
        grid=(nblk,),
        in_specs=[pl.BlockSpec((1, 1, MOE_ROWS), lambda i, e: (i, 0, 0), memory_space=pltpu.SMEM),
                  pl.BlockSpec((1, 1, MOE_ROWS), lambda i, e: (jnp.minimum(i + 1, nblk - 1), 0, 0),
                               memory_space=pltpu.SMEM),
                  pl.BlockSpec((MOE_ROWS, 1), lambda i, e: (i, 0)),
                  pl.BlockSpec(memory_space=pl.ANY),
                  pl.BlockSpec((1, d, 2 * d_exp), lambda i, e: (e[i], 0, 0)),
                  pl.BlockSpec((1, d_exp, d), lambda i, e: (e[i], 0, 0))],
        out_specs=pl.BlockSpec((MOE_ROWS, d), lambda i, e: (i, 0)),
        scratch_shapes=[pltpu.VMEM((2, MOE_ROWS, d), F32), pltpu.SemaphoreType.DMA((2,))],
    )
    return pl.pallas_call(
        functools.partial(_moe_kernel, rows=MOE_ROWS, d_exp=d_exp),
        grid_spec=grid_spec,
        out_shape=jax.ShapeDtypeStruct((n_rows, d), F32),
        compiler_params=_cparams("arbitrary"),
        name="moe_experts",
    )(blk_e, tok3, tok3, row_w.reshape(n_rows, 1), h2, w13, w2)


def _route(ri, rw, n_tok):
    n_assign = 2 * n_tok
    eid = ri[:, :2].reshape(n_assign)
    wflat = rw[:, :2].reshape(n_assign)
    order = jnp.argsort(eid)
    e_s = eid[order]
    counts = jnp.zeros((N_EXPERTS,), jnp.int32).at[eid].add(1)
    starts = jnp.cumsum(counts) - counts
    pcounts = (counts + MOE_ROWS - 1) // MOE_ROWS * MOE_ROWS
    pends = jnp.cumsum(pcounts)
    pstarts = pends - pcounts
    dest_s = pstarts[e_s] + jnp.arange(n_assign, dtype=jnp.int32) - starts[e_s]
    n_rows = n_assign + N_EXPERTS * MOE_ROWS
    nblk = n_rows // MOE_ROWS
    row_tok = jnp.zeros((n_rows,), jnp.int32).at[dest_s].set((order // 2).astype(jnp.int32))
    row_w = jnp.zeros((n_rows,), F32).at[dest_s].set(wflat[order])
    blk_e = jnp.minimum(jnp.searchsorted(pends, jnp.arange(nblk, dtype=jnp.int32) * MOE_ROWS, side="right"),
                        N_EXPERTS - 1).astype(jnp.int32)
    dest = jnp.zeros((n_assign,), jnp.int32).at[order].set(dest_s.astype(jnp.int32))
    return row_tok, row_w, blk_e, dest.reshape(n_tok, 2)


def _comb_kernel(d_ref, d_next_ref, x_ref, gt_ref, y_hbm, g_ref, o_ref, ybuf, sem, *, tm, final):
    i = pl.program_id(0)
    n = pl.num_programs(0)
    slot = i % 2

    def start_all(idx_ref, s):
        def body(r, carry):
            _row_copy(y_hbm, idx_ref[0, 0, r], ybuf, s, r, sem).start()
            return carry
        lax.fori_loop(0, 2 * tm, body, 0)

    @pl.when(i == 0)
    def _():
        start_all(d_ref, 0)

    @pl.when(i + 1 < n)
    def _():
        start_all(d_next_ref, 1 - slot)

    def wait_body(r, carry):
        _row_copy(y_hbm, 0, ybuf, slot, r, sem).wait()
        return carry
    lax.fori_loop(0, 2 * tm, wait_body, 0)

    moe = ybuf[slot, 0:tm, :] + ybuf[slot, tm:2 * tm, :]
    x2 = x_ref[...] + (1.0 + gt_ref[0, 0]) * moe
    o_ref[...] = _rmsnorm(x2, g_ref[...]) if final else x2


def _combine(x1, y_rows, dest, mod_l, g_final, seq, final):
    n_tok, d = x1.shape
    tm = _row_tile(seq, 256)
    per_b = seq // tm
    nblk = n_tok // tm
    idx = dest.reshape(nblk, tm, 2).transpose(0, 2, 1).reshape(nblk, 1, 2 * tm)
    return pl.pallas_call(
        functools.partial(_comb_kernel, tm=tm, final=final),
        grid=(nblk,),
        in_specs=[pl.BlockSpec((1, 1, 2 * tm), lambda i: (i, 0, 0), memory_space=pltpu.SMEM),
                  pl.BlockSpec((1, 1, 2 * tm), lambda i: (jnp.minimum(i + 1, nblk - 1), 0, 0),
                               memory_space=pltpu.SMEM),
                  pl.BlockSpec((tm, d), lambda i: (i, 0)),
                  pl.BlockSpec((1, 1, 1, d), lambda i: (i // per_b, 5, 0, 0)),
                  pl.BlockSpec(memory_space=pl.ANY),
                  pl.BlockSpec((1, d), lambda i: (0, 0))],
        out_specs=pl.BlockSpec((tm, d), lambda i: (i, 0)),
        out_shape=jax.ShapeDtypeStruct((n_tok, d), F32),
        scratch_shapes=[pltpu.VMEM((2, 2 * tm, d), F32), pltpu.SemaphoreType.DMA((2,))],
        compiler_params=_cparams("arbitrary"),
        name="moe_combine",
    )(idx, idx, x1, mod_l, y_rows, g_final)


def kernel(x, c, ada_w, ada_b, norm_mix, norm_ffn, norm_final, w_in, w_out, mlstm_conv_w, mlstm_conv_b, mlstm_w_q, mlstm_w_k, mlstm_b_i, mlstm_b_f, mlstm_norm_w, rwkv_mu, rwkv_w0, rwkv_w_up, rwkv_a0, rwkv_a_up, rwkv_g_up, rwkv_k_k, rwkv_k_a, rwkv_r_k, rwkv_ln_w, rwkv_ln_b, rwkv_v0, rwkv_v_dn, rwkv_v_up, s5_a_re, s5_a_im, s5_log_dt, s5_b_re, s5_b_im, s5_c_re, s5_c_im, s5_d, s5_glu_w, s5_glu_b, moe_w_rg, moe_b_rg, moe_w_re, moe_b_re, moe_w1, moe_w3, moe_w2):
    bsz, seq, d = x.shape
    depth = ada_w.shape[0]
    n_tok = bsz * seq
    heads_a = mlstm_w_q.shape[1]
    da = heads_a * HEAD_DIM
    db = rwkv_w0.shape[1]
    dc = s5_d.shape[1]
    rw_cols = rwkv_mu.shape[1]
    assert seq % CHUNK == 0 and w_in.shape[2] == 3 * da + 2 * heads_a + rw_cols + dc
    widths = (3 * da, LANES, rw_cols, dc)

    mod = _modulation(c, ada_w, ada_b).reshape(depth, bsz, 6, 1, d)
    xc = x.reshape(n_tok, d)
    v_first = None
    for l in range(depth):
        mod_l = mod[l]
        w = w_in[l]
        gate_pad = jnp.zeros((d, LANES - 2 * heads_a), w.dtype)
        w_pad = jnp.concatenate([w[:, :3 * da], w[:, 3 * da:3 * da + 2 * heads_a], gate_pad,
                                 w[:, 3 * da + 2 * heads_a:]], axis=1).astype(BF16)
        qkvo, gates, pcols, u = _in_proj(xc, mod_l, norm_mix[l].reshape(1, d), w_pad, widths, seq)

        ya = _mlstm(qkvo, gates, mlstm_conv_w[l], mlstm_conv_b[l], mlstm_w_q[l], mlstm_w_k[l],
                    mlstm_b_i[l], mlstm_b_f[l], mlstm_norm_w[l], bsz, seq)
        rprm = dict(mu=rwkv_mu[l], w0=rwkv_w0[l], w_up=rwkv_w_up[l], a0=rwkv_a0[l], a_up=rwkv_a_up[l],
                    g_up=rwkv_g_up[l], k_k=rwkv_k_k[l], k_a=rwkv_k_a[l], r_k=rwkv_r_k[l],
                    ln_w=rwkv_ln_w[l], ln_b=rwkv_ln_b[l])
        if l > 0:
            rprm.update(v0=rwkv_v0[l - 1], v_dn=rwkv_v_dn[l - 1], v_up=rwkv_v_up[l - 1])
        yb, v_first = _rwkv(pcols, v_first if l > 0 else None, rprm, bsz, seq)
        sprm = dict(a_re=s5_a_re[l], a_im=s5_a_im[l], log_dt=s5_log_dt[l], b_re=s5_b_re[l],
                    b_im=s5_b_im[l], c_re=s5_c_re[l], c_im=s5_c_im[l])
        yc = _s5_core(u, sprm, bsz, seq)

        x1, h2, ri, rw = _out_proj(ya, yb, yc, u, xc, mod_l, norm_ffn[l].reshape(1, d), s5_d[l],
                                   s5_glu_w[l], s5_glu_b[l], w_out[l], moe_w_rg[l], moe_b_rg[l],
                                   moe_w_re[l], moe_b_re[l], seq)
        row_tok, row_w, blk_e, dest = _route(ri, rw, n_tok)
        w13 = jnp.concatenate([moe_w1[l], moe_w3[l]], axis=2).astype(BF16)
        y_rows = _moe_experts(h2, row_tok, row_w, blk_e, w13, moe_w2[l].astype(BF16))
        xc = _combine(x1, y_rows, dest, mod_l, norm_final.reshape(1, d), seq, final=(l == depth - 1))
    return xc.reshape(bsz, seq, d).astype(x.dtype)
```

```python
import functools

import jax
import jax.numpy as jnp
from jax import lax
from jax.experimental import pallas as pl
from jax.experimental.pallas import tpu as pltpu

F32 = jnp.float32
BF16 = jnp.bfloat16

HEAD_DIM = 64
CHUNK = 64
CONV_K = 4
S5_GC = 16
S5_P = 64
N_GROUPS = 4
EXPERTS_PER_GROUP = 8
N_EXPERTS = N_GROUPS * EXPERTS_PER_GROUP
NORM_EPS = 1e-6
HEAD_NORM_EPS = 1e-5
RWKV_GN_EPS = 64e-5
L2_EPS = 1e-12
LANES = 128
SUBLANES = 8
MOE_ROWS = 256
VMEM_LIMIT = 56 * 1024 * 1024


def _cparams(*sem):
    return pltpu.CompilerParams(dimension_semantics=sem, vmem_limit_bytes=VMEM_LIMIT)


def _row_tile(n, want):
    t = min(n, want)
    assert n % t == 0
    return t


def _dot(a, b):
    return jnp.dot(a.astype(BF16), b.astype(BF16), preferred_element_type=F32)


def _dot_nt(a, b):
    return lax.dot_general(a.astype(BF16), b.astype(BF16), (((1,), (1,)), ((), ())),
                           preferred_element_type=F32)


def _dot_tn(a, b):
    return lax.dot_general(a.astype(BF16), b.astype(BF16), (((0,), (0,)), ((), ())),
                           preferred_element_type=F32)


def _split(a):
    hi = a.astype(BF16)
    lo = (a - hi.astype(F32)).astype(BF16)
    return hi, lo


def _dot_xa(a, b_exact):
    hi, lo = _split(a)
    return (jnp.dot(hi, b_exact, preferred_element_type=F32)
            + jnp.dot(lo, b_exact, preferred_element_type=F32))


def _dot_xb(a_exact, b):
    hi, lo = _split(b)
    return (jnp.dot(a_exact, hi, preferred_element_type=F32)
            + jnp.dot(a_exact, lo, preferred_element_type=F32))


def _dot3(a, b):
    ah, al = _split(a)
    bh, bl = _split(b)
    return (jnp.dot(ah, bh, preferred_element_type=F32)
            + jnp.dot(ah, bl, preferred_element_type=F32)
            + jnp.dot(al, bh, preferred_element_type=F32))


def _dot3_nt(a, b):
    ah, al = _split(a)
    bh, bl = _split(b)
    dn = (((1,), (1,)), ((), ()))
    return (lax.dot_general(ah, bh, dn, preferred_element_type=F32)
            + lax.dot_general(ah, bl, dn, preferred_element_type=F32)
            + lax.dot_general(al, bh, dn, preferred_element_type=F32))


def _sigmoid(x):
    return 1.0 / (1.0 + jnp.exp(-x))


def _silu(x):
    return x * _sigmoid(x)


def _log_sigmoid(x):
    return jnp.minimum(x, 0.0) - jnp.log1p(jnp.exp(-jnp.abs(x)))


def _rmsnorm(x, g):
    ms = jnp.mean(x * x, axis=-1, keepdims=True)
    return x * lax.rsqrt(ms + NORM_EPS) * g


def _tri_incl(n):
    r = lax.broadcasted_iota(jnp.int32, (n, n), 0)
    c = lax.broadcasted_iota(jnp.int32, (n, n), 1)
    return (c <= r).astype(BF16)


def _head_ones(width):
    r = lax.broadcasted_iota(jnp.int32, (width, width), 0) // HEAD_DIM
    c = lax.broadcasted_iota(jnp.int32, (width, width), 1) // HEAD_DIM
    return (r == c).astype(BF16)


def _mod_kernel(c_ref, w_ref, b_ref, o_ref):
    o_ref[0] = _dot(_silu(c_ref[...]), w_ref[0]) + b_ref[0]


def _modulation(c, ada_w, ada_b):
    depth, d, d6 = ada_w.shape
    bsz = c.shape[0]
    tn = _row_tile(d6, 1024)
    return pl.pallas_call(
        _mod_kernel,
        grid=(depth, d6 // tn),
        in_specs=[pl.BlockSpec((bsz, d), lambda l, j: (0, 0)),
                  pl.BlockSpec((1, d, tn), lambda l, j: (l, 0, j)),
                  pl.BlockSpec((1, 1, tn), lambda l, j: (l, 0, j))],
        out_specs=pl.BlockSpec((1, bsz, tn), lambda l, j: (l, 0, j)),
        out_shape=jax.ShapeDtypeStruct((depth, bsz, d6), F32),
        compiler_params=_cparams("parallel", "parallel"),
        name="adaln_mod",
    )(c, ada_w, ada_b.reshape(depth, 1, d6))


def _in_kernel(x_ref, sh_ref, sc_ref, g_ref, w_ref, *out_refs, widths):
    h = _rmsnorm(x_ref[...], g_ref[...]) * (1.0 + sc_ref[0, 0]) + sh_ref[0, 0]
    hb = h.astype(BF16)
    off = 0
    for o_ref, wd in zip(out_refs, widths):
        o_ref[...] = jnp.dot(hb, w_ref[:, off:off + wd], preferred_element_type=F32)
        off += wd


def _in_proj(x2d, mod_l, g, w_pad, widths, seq):
    n_tok, d = x2d.shape
    tm = _row_tile(seq, 512)
    per_b = seq // tm
    tot = sum(widths)
    return pl.pallas_call(
        functools.partial(_in_kernel, widths=widths),
        grid=(n_tok // tm,),
        in_specs=[pl.BlockSpec((tm, d), lambda i: (i, 0)),
                  pl.BlockSpec((1, 1, 1, d), lambda i: (i // per_b, 0, 0, 0)),
                  pl.BlockSpec((1, 1, 1, d), lambda i: (i // per_b, 1, 0, 0)),
                  pl.BlockSpec((1, d), lambda i: (0, 0)),
                  pl.BlockSpec((d, tot), lambda i: (0, 0))],
        out_specs=[pl.BlockSpec((tm, wd), lambda i: (i, 0)) for wd in widths],
        out_shape=[jax.ShapeDtypeStruct((n_tok, wd), F32) for wd in widths],
        compiler_params=_cparams("parallel"),
        name="in_proj",
    )(x2d, mod_l, mod_l, g, w_pad)


def _mlstm_kernel(qkvo_ref, gate_ref, cw_ref, cb_ref, wq_ref, wk_ref, gb_ref, nw_ref, out_ref,
                  xf_sc, q_sc, k_sc, g_sc, c_sc, n_sc, m_sc, *, tb, heads):
    dh, L = HEAD_DIM, CHUNK
    da = heads * dh
    i = pl.program_id(1)

    @pl.when(i == 0)
    def _():
        xf_sc[0:SUBLANES, :] = jnp.zeros((SUBLANES, da), F32)
        c_sc[...] = jnp.zeros_like(c_sc)
        n_sc[...] = jnp.zeros_like(n_sc)
        m_sc[...] = jnp.zeros_like(m_sc)

    xqk = qkvo_ref[:, 0:da]
    xf_sc[pl.ds(SUBLANES, tb), :] = xqk
    acc = xqk * cw_ref[CONV_K - 1:CONV_K, :] + cb_ref[...]
    for j in range(1, CONV_K):
        acc = acc + xf_sc[pl.ds(SUBLANES - j, tb), :] * cw_ref[CONV_K - 1 - j:CONV_K - j, :]
    xf_sc[0:SUBLANES, :] = xf_sc[pl.ds(tb, SUBLANES), :]
    cx = _silu(acc).astype(BF16)
    q_sc[...] = jnp.dot(cx, wq_ref[...], preferred_element_type=F32)
    k_sc[...] = jnp.dot(cx, wk_ref[...], preferred_element_type=F32) * (dh ** -0.5)

    g = gate_ref[...] + gb_ref[...]
    lane = lax.broadcasted_iota(jnp.int32, g.shape, 1)
    is_f = (lane >= heads) & (lane < 2 * heads)
    g_sc[...] = jnp.where(is_f, _log_sigmoid(g), g)

    tri = _tri_incl(L)
    rr = lax.broadcasted_iota(jnp.int32, (L, L), 0)
    cc = lax.broadcasted_iota(jnp.int32, (L, L), 1)
    causal = cc <= rr
    lane_c = lax.broadcasted_iota(jnp.int32, (L, LANES), 1)

    def chunk(c, carry):
        r0 = pl.multiple_of(c * L, L)
        gc = g_sc[pl.ds(r0, L), :]
        bc = _dot_xb(tri, gc)
        zt = jnp.where(lane_c < heads, gc, bc).T
        hr = range(heads)
        rows = pl.ds(r0, L)
        qc = [q_sc[rows, h * dh:(h + 1) * dh] for h in hr]
        kc = [k_sc[rows, h * dh:(h + 1) * dh] for h in hr]
        vc = [qkvo_ref[rows, da + h * dh:da + (h + 1) * dh] for h in hr]
        oc = [qkvo_ref[rows, 2 * da + h * dh:2 * da + (h + 1) * dh] for h in hr]
        c_prev = [c_sc[h] for h in hr]
        n_prev = [n_sc[h] for h in hr]
        m_prev = [m_sc[h][:, 0:1] for h in hr]
        ig_col = [gc[:, h:h + 1] for h in hr]
        bc_col = [bc[:, heads + h:heads + h + 1] for h in hr]
        ig_row = [zt[h:h + 1, :] for h in hr]
        bc_row = [zt[heads + h:heads + h + 1, :] for h in hr]
        gtot = [bc[L - 1:L, heads + h:heads + h + 1] for h in hr]

        lw = [gtot[h] - bc_col[h] + ig_col[h] for h in hr]
        m_loc = [jnp.max(lw[h], axis=0, keepdims=True) for h in hr]
        kw = [kc[h] * jnp.exp(lw[h] - m_loc[h]) for h in hr]
        s_raw = [_dot_nt(qc[h], kc[h]) for h in hr]
        q_c = [_dot(qc[h], c_prev[h]) for h in hr]
        c_loc = [_dot_tn(kw[h], vc[h]) for h in hr]
        m_new = [jnp.maximum(gtot[h] + m_prev[h], m_loc[h]) for h in hr]
        a_old = [jnp.exp(gtot[h] + m_prev[h] - m_new[h]) for h in hr]
        a_loc = [jnp.exp(m_loc[h] - m_new[h]) for h in hr]

        dmat = [jnp.where(causal, bc_col[h] - bc_row[h] + ig_row[h], -jnp.inf) for h in hr]
        inter = [bc_col[h] + m_prev[h] for h in hr]
        m_t = [jnp.maximum(inter[h], jnp.max(dmat[h], axis=1, keepdims=True)) for h in hr]
        s_qk = [s_raw[h] * jnp.exp(dmat[h] - m_t[h]) for h in hr]
        s_v = [_dot(s_qk[h], vc[h]) for h in hr]
        w_inter = [jnp.exp(inter[h] - m_t[h]) for h in hr]
        outs = []
        for h in hr:
            num = w_inter[h] * q_c[h] + s_v[h]
            den = (w_inter[h] * jnp.sum(qc[h] * n_prev[h], axis=1, keepdims=True)
                   + jnp.sum(s_qk[h], axis=1, keepdims=True))
            hh = num / jnp.maximum(jnp.abs(den), jnp.exp(-m_t[h]))
            mu = jnp.mean(hh, axis=1, keepdims=True)
            dlt = hh - mu
            var = jnp.mean(dlt * dlt, axis=1, keepdims=True)
            outs.append(dlt * lax.rsqrt(var + HEAD_NORM_EPS) * _sigmoid(oc[h]))
        out_ref[rows, :] = jnp.concatenate(outs, axis=1) * nw_ref[...]
        for h in hr:
            c_sc[h] = a_old[h] * c_prev[h] + a_loc[h] * c_loc[h]
            n_sc[h] = a_old[h] * n_prev[h] + a_loc[h] * jnp.sum(kw[h], axis=0, keepdims=True)
            m_sc[h] = jnp.broadcast_to(m_new[h], (1, dh))
        return carry

    lax.fori_loop(0, tb // L, chunk, 0)


def _block_diag(w):
    heads, dh, _ = w.shape
    eye = jnp.eye(heads, dtype=w.dtype)
    return (eye[:, None, :, None] * w[:, :, None, :]).reshape(heads * dh, heads * dh)


def _mlstm(qkvo, gates, conv_w, conv_b, w_q, w_k, b_i, b_f, norm_w, bsz, seq):
    heads = w_q.shape[0]
    da = heads * HEAD_DIM
    tb = _row_tile(seq, 512)
    nblk = seq // tb
    gbias = jnp.zeros((1, LANES), F32).at[0, :heads].set(b_i).at[0, heads:2 * heads].set(b_f)
    kern = functools.partial(_mlstm_kernel, tb=tb, heads=heads)
    const = lambda b, i: (0, 0)
    return pl.pallas_call(
        kern,
        grid=(bsz, nblk),
        in_specs=[pl.BlockSpec((tb, 3 * da), lambda b, i: (b * nblk + i, 0)),
                  pl.BlockSpec((tb, LANES), lambda b, i: (b * nblk + i, 0)),
                  pl.BlockSpec((CONV_K, da), const),
                  pl.BlockSpec((1, da), const),
                  pl.BlockSpec((da, da), const),
                  pl.BlockSpec((da, da), const),
                  pl.BlockSpec((1, LANES), const),
                  pl.BlockSpec((1, da), const)],
        out_specs=pl.BlockSpec((tb, da), lambda b, i: (b * nblk + i, 0)),
        out_shape=jax.ShapeDtypeStruct((bsz * seq, da), F32),
        scratch_shapes=[pltpu.VMEM((tb + SUBLANES, da), F32),
                        pltpu.VMEM((tb, da), F32),
                        pltpu.VMEM((tb, da), F32),
                        pltpu.VMEM((tb, LANES), F32),
                        pltpu.VMEM((heads, HEAD_DIM, HEAD_DIM), F32),
                        pltpu.VMEM((heads, 1, HEAD_DIM), F32),
                        pltpu.VMEM((heads, 1, HEAD_DIM), F32)],
        compiler_params=_cparams("parallel", "arbitrary"),
        name="mlstm",
    )(qkvo, gates, conv_w, conv_b.reshape(1, da), _block_diag(w_q).astype(BF16),
      _block_diag(w_k).astype(BF16), gbias, norm_w.reshape(1, da))


def _rwkv_kernel(*refs, tb, heads, lw_dim, la_dim, lg_dim, has_vres):
    dh, L = HEAD_DIM, CHUNK
    db = heads * dh
    it = iter(refs)
    p_ref = next(it)
    vf_ref = next(it) if has_vres else None
    (mu_ref, w0_ref, wup_ref, a0_ref, aup_ref, gup_ref, kk_ref, ka_ref, rk_ref,
     lnw_ref, lnb_ref) = (next(it) for _ in range(11))
    if has_vres:
        v0_ref, vdn_ref, vup_ref = (next(it) for _ in range(3))
    y_ref = next(it)
    vout_ref = None if has_vres else next(it)
    (xf_sc, r_sc, k_sc, v_sc, a_sc, b_sc, lw_sc, y_sc, st_sc, rp_sc, q_sc, z_sc,
     gl_sc) = (next(it) for _ in range(13))

    i = pl.program_id(1)

    @pl.when(i == 0)
    def _():
        xf_sc[0:SUBLANES, :] = jnp.zeros((SUBLANES, xf_sc.shape[1]), F32)
        st_sc[...] = jnp.zeros_like(st_sc)

    p = p_ref[...]
    xf_sc[pl.ds(SUBLANES, tb), :] = p
    prev = xf_sc[pl.ds(SUBLANES - 1, tb), :]
    xf_sc[0:SUBLANES, :] = xf_sc[pl.ds(tb, SUBLANES), :]
    p = p + mu_ref[...] * (prev - p)

    o = 0
    r = p[:, o:o + db]; o += db
    k = p[:, o:o + db]; o += db
    v = p[:, o:o + db]; o += db
    wd = p[:, o:o + lw_dim]; o += lw_dim
    ad = p[:, o:o + la_dim]; o += la_dim
    gd = p[:, o:o + lg_dim]

    ones_h = _head_ones(db)
    wlog = _log_sigmoid(w0_ref[...] + _dot(jnp.tanh(wd), wup_ref[...])) - 0.5
    lw_sc[...] = -jnp.exp(wlog)
    a = _sigmoid(a0_ref[...] + _dot(ad, aup_ref[...]))
    gate = _dot(_sigmoid(gd), gup_ref[...])
    if has_vres:
        v = v + (vf_ref[...] - v) * _sigmoid(v0_ref[...] + _dot(_dot(v, vdn_ref[...]), vup_ref[...]))
    else:
        vout_ref[...] = v
    kk = k * kk_ref[...]
    kk = kk / jnp.maximum(jnp.sqrt(_dot_xa(kk * kk, ones_h)), L2_EPS)
    k2 = k * (1.0 + (a - 1.0) * ka_ref[...])
    bonus = _dot_xa(r * k2 * rk_ref[...], ones_h) * v
    r_sc[...] = r
    k_sc[...] = k2
    v_sc[...] = v
    a_sc[...] = -kk
    b_sc[...] = kk * a

    tri = _tri_incl(L)
    rr = lax.broadcasted_iota(jnp.int32, (L, L), 0)
    cc = lax.broadcasted_iota(jnp.int32, (L, L), 1)
    strict = cc < rr
    incl = cc <= rr

    def chunk(c, carry):
        r0 = pl.multiple_of(c * L, L)
        rows = pl.ds(r0, L)
        lwc = lw_sc[rows, :]
        cum = _dot_xb(tri, lwc)
        cum_l = cum[L - 1:L, :]
        e_in = jnp.exp(cum)
        e_ex = jnp.exp(cum - lwc)
        e_inv = jnp.exp(-cum)
        e_end = jnp.exp(cum_l - cum)
        at = a_sc[rows, :] * e_ex
        rt = r_sc[rows, :] * e_in
        bv = b_sc[rows, :]
        kv = k_sc[rows, :]
        bt = bv * e_inv
        kt = kv * e_inv
        bg = bv * e_end
        kg = kv * e_end
        gl_sc[c] = jnp.exp(cum_l)
        vch = v_sc[rows, :]
        hr = range(heads)
        hsl = [slice(h * dh, (h + 1) * dh) for h in hr]
        vh = [vch[:, hsl[h]] for h in hr]
        g4 = [_dot_nt(jnp.concatenate([at[:, hsl[h]], rt[:, hsl[h]]], axis=0),
                      jnp.concatenate([bt[:, hsl[h]], kt[:, hsl[h]]], axis=0)) for h in hr]
        pw = [jnp.where(strict, g4[h][:L, :L], 0.0) for h in hr]
        n_ak = [jnp.where(strict, g4[h][:L, L:], 0.0) for h in hr]
        m_rb = [jnp.where(incl, g4[h][L:, :L], 0.0) for h in hr]
        m_rk = [jnp.where(incl, g4[h][L:, L:], 0.0) for h in hr]
        x = [jnp.concatenate([at[:, hsl[h]], _dot(n_ak[h], vh[h])], axis=1) for h in hr]
        for step in range(6):
            x = [x[h] + _dot(pw[h], x[h]) for h in hr]
            if step < 5:
                pw = [_dot(pw[h], pw[h]) for h in hr]
        ry = [jnp.concatenate([rt[:, hsl[h]], _dot(m_rk[h], vh[h])], axis=1) + _dot(m_rb[h], x[h])
              for h in hr]
        qz = [_dot_tn(x[h], bg[:, hsl[h]]) for h in hr]
        z2 = [_dot_tn(vh[h], kg[:, hsl[h]]) for h in hr]
        rp_sc[rows, :] = jnp.concatenate([ry[h][:, :dh] for h in hr], axis=1)
        y_sc[rows, :] = jnp.concatenate([ry[h][:, dh:] for h in hr], axis=1)
        for h in hr:
            q_sc[c, h] = qz[h][:dh]
            z_sc[c, h] = qz[h][dh:] + z2[h]
        return carry

    lax.fori_loop(0, tb // L, chunk, 0)

    def carry_state(c, carry):
        rows = pl.ds(pl.multiple_of(c * L, L), L)
        hr = range(heads)
        st = [st_sc[h] for h in hr]
        rp = rp_sc[rows, :]
        g_l = gl_sc[c]
        ys = [_dot_nt(rp[:, h * dh:(h + 1) * dh], st[h]) for h in hr]
        sq = [_dot_xa(st[h], q_sc[c, h].astype(BF16)) for h in hr]
        y_sc[rows, :] = y_sc[rows, :] + jnp.concatenate(ys, axis=1)
        for h in hr:
            st_sc[h] = st[h] * g_l[:, h * dh:(h + 1) * dh] + sq[h] + z_sc[c, h]
        return carry

    lax.fori_loop(0, tb // L, carry_state, 0)

    y = y_sc[...]
    mean = _dot_xa(y, ones_h) * (1.0 / dh)
    dlt = y - mean
    var = _dot_xa(dlt * dlt, ones_h) * (1.0 / dh)
    yn = dlt * lax.rsqrt(var + RWKV_GN_EPS) * lnw_ref[...] + lnb_ref[...]
    y_ref[...] = (yn + bonus) * gate


def _rwkv(pcols, v_first, prm, bsz, seq):
    db = prm["w0"].shape[0]
    heads = db // HEAD_DIM
    cols = pcols.shape[1]
    lw_dim, la_dim, lg_dim = prm["w_up"].shape[0], prm["a_up"].shape[0], prm["g_up"].shape[0]
    has_vres = v_first is not None
    tb = _row_tile(seq, 512)
    nblk = seq // tb
    row = lambda b, i: (b * nblk + i, 0)
    const = lambda b, i: (0, 0)
    vec = lambda a: a.reshape(1, -1).astype(F32)

    args = [pcols]
    specs = [pl.BlockSpec((tb, cols), row)]
    if has_vres:
        args.append(v_first)
        specs.append(pl.BlockSpec((tb, db), row))
    small = [vec(prm["mu"]), vec(prm["w0"]), prm["w_up"].astype(BF16), vec(prm["a0"]),
             prm["a_up"].astype(BF16), prm["g_up"].astype(BF16), vec(prm["k_k"]), vec(prm["k_a"]),
             vec(prm["r_k"]), vec(prm["ln_w"]), vec(prm["ln_b"])]
    if has_vres:
        lv = prm["v_dn"].shape[1]
        lvp = -(-lv // LANES) * LANES
        v_dn = jnp.zeros((db, lvp), F32).at[:, :lv].set(prm["v_dn"]).astype(BF16)
        v_up = jnp.zeros((lvp, db), F32).at[:lv, :].set(prm["v_up"]).astype(BF16)
        small += [vec(prm["v0"]), v_dn, v_up]
    args += small
    specs += [pl.BlockSpec(a.shape, const) for a in small]

    out_shape = [jax.ShapeDtypeStruct((bsz * seq, db), F32)]
    out_specs = [pl.BlockSpec((tb, db), row)]
    if not has_vres:
        out_shape.append(jax.ShapeDtypeStruct((bsz * seq, db), F32))
        out_specs.append(pl.BlockSpec((tb, db), row))

    kern = functools.partial(_rwkv_kernel, tb=tb, heads=heads, lw_dim=lw_dim, la_dim=la_dim,
                             lg_dim=lg_dim, has_vres=has_vres)
    res = pl.pallas_call(
        kern,
        grid=(bsz, nblk),
        in_specs=specs,
        out_specs=out_specs,
        out_shape=out_shape,
        scratch_shapes=[pltpu.VMEM((tb + SUBLANES, cols), F32)]
        + [pltpu.VMEM((tb, db), F32) for _ in range(7)]
        + [pltpu.VMEM((heads, HEAD_DIM, HEAD_DIM), F32),
           pltpu.VMEM((tb, db), F32),
           pltpu.VMEM((tb // CHUNK, heads, HEAD_DIM, HEAD_DIM), F32),
           pltpu.VMEM((tb // CHUNK, heads, HEAD_DIM, HEAD_DIM), F32),
           pltpu.VMEM((tb // CHUNK, 1, db), F32)],
        compiler_params=_cparams("parallel", "arbitrary"),
        name="rwkv7",
    )(*args)
    return (res[0], v_first) if has_vres else (res[0], res[1])


def _cmul(ar, ai, br, bi):
    return ar * br - ai * bi, ar * bi + ai * br


def _shift_rows(x, d):
    row = lax.broadcasted_iota(jnp.int32, x.shape, 0)
    return jnp.where(row >= d, pltpu.roll(x, d, 0), 0.0)


def _s5_kernel(u_ref, are_r, aim_r, dt_r, are_c, aim_c, dt_c, bre_ref, bim_ref, cre_ref, cim_ref,
               y_ref, m_sc, *, bsz, nchunk):
    L, gc, P = CHUNK, S5_GC, S5_P
    n = L * gc
    a_re, a_im, dt = are_r[0], aim_r[0], jnp.exp(dt_r[0])
    mag, ang = jnp.exp(a_re * dt), a_im * dt
    ab_re, ab_im = mag * jnp.cos(ang), mag * jnp.sin(ang)
    inv = 1.0 / (a_re * a_re + a_im * a_im)
    co_re = ((ab_re - 1.0) * a_re + ab_im * a_im) * inv
    co_im = (ab_im * a_re - (ab_re - 1.0) * a_im) * inv
    bb_re, bb_im = _cmul(co_re, co_im, bre_ref[0], bim_ref[0])
    c_re, c_im = cre_ref[0], cim_ref[0]

    def powers(tau):
        m = jnp.exp(tau * (a_re * dt))
        return m * jnp.cos(tau * ang), m * jnp.sin(tau * ang)

    a_re_c, a_im_c, dt_c_ = are_c[0], aim_c[0], jnp.exp(dt_c[0])
    tau_row = lax.broadcasted_iota(jnp.int32, (P, L), 1).astype(F32)
    pm = jnp.exp(tau_row * (a_re_c * dt_c_))
    pt_re = pm * jnp.cos(tau_row * (a_im_c * dt_c_))
    pt_im = pm * jnp.sin(tau_row * (a_im_c * dt_c_))
    pair = lax.broadcasted_iota(jnp.int32, (gc * gc, gc), 0)
    col = lax.broadcasted_iota(jnp.int32, (gc * gc, gc), 1)
    rep_c = (pair // gc == col).astype(BF16)
    rep_b = (pair % gc == col).astype(BF16)
    cb_re, cb_im = _cmul(_dot_xb(rep_c, c_re), _dot_xb(rep_c, c_im),
                         _dot_xb(rep_b, bb_re), _dot_xb(rep_b, bb_im))
    kap = _dot3(cb_re, pt_re) - _dot3(cb_im, pt_im)

    kap_pad = jnp.concatenate([kap, jnp.zeros_like(kap)], axis=1)
    srow = lax.broadcasted_iota(jnp.int32, (L, LANES), 0)
    tcol = lax.broadcasted_iota(jnp.int32, (L, LANES), 1)
    for cp in range(gc):
        for c2 in range(0, gc, 2):
            k0 = jnp.broadcast_to(kap_pad[c2 * gc + cp:c2 * gc + cp + 1, :], (L, LANES))
            k1 = jnp.broadcast_to(kap_pad[(c2 + 1) * gc + cp:(c2 + 1) * gc + cp + 1, :], (L, LANES))
            t0 = pltpu.roll(k0, 0, 1, stride=1, stride_axis=0)
            t1 = pltpu.roll(k1, L, 1, stride=1, stride_axis=0)
            blk = jnp.where(tcol < L, jnp.where(tcol >= srow, t0, 0.0),
                            jnp.where(tcol - L >= srow, t1, 0.0))
            m_sc[cp * L:(cp + 1) * L, c2 * L:(c2 + 2) * L] = blk.astype(BF16)

    s_col = lax.broadcasted_iota(jnp.int32, (L, P), 0).astype(F32)
    pw_re, pw_im = powers((L - 1.0) - s_col)
    pg_re, pg_im = powers(s_col + 1.0)
    w_re, w_im, g_re, g_im = [], [], [], []
    for c in range(gc):
        br = jnp.broadcast_to(bb_re[c:c + 1, :], (L, P))
        bi = jnp.broadcast_to(bb_im[c:c + 1, :], (L, P))
        wr, wi = _cmul(br, bi, pw_re, pw_im)
        w_re.append(wr); w_im.append(wi)
        cr = jnp.broadcast_to(c_re[c:c + 1, :], (L, P))
        ci = jnp.broadcast_to(c_im[c:c + 1, :], (L, P))
        gr, gi = _cmul(cr, ci, pg_re, pg_im)
        g_re.append(gr); g_im.append(gi)
    w_re, w_im = jnp.concatenate(w_re, axis=0), jnp.concatenate(w_im, axis=0)
    g_re, g_im = jnp.concatenate(g_re, axis=0), jnp.concatenate(g_im, axis=0)

    u = u_ref[0].astype(BF16)
    x_re = jnp.dot(u, w_re.astype(BF16), preferred_element_type=F32)
    x_im = jnp.dot(u, w_im.astype(BF16), preferred_element_type=F32)
    xs_re, xs_im = [], []
    for b in range(bsz):
        xr = x_re[b * nchunk:(b + 1) * nchunk]
        xi = x_im[b * nchunk:(b + 1) * nchunk]
        d = 1
        while d < nchunk:
            ar_, ai_ = powers(float(L * d))
            sr, si = _cmul(ar_, ai_, _shift_rows(xr, d), _shift_rows(xi, d))
            xr, xi = xr + sr, xi + si
            d *= 2
        xs_re.append(_shift_rows(xr, 1))
        xs_im.append(_shift_rows(xi, 1))
    xs_re, xs_im = jnp.concatenate(xs_re, axis=0), jnp.concatenate(xs_im, axis=0)
    y = jnp.dot(u, m_sc[...], preferred_element_type=F32)
    y = y + _dot_nt(xs_re, g_re) - _dot_nt(xs_im, g_im)
    y_ref[0] = y


def _s5_core(u2d, prm, bsz, seq):
    L, gc, P = CHUNK, S5_GC, S5_P
    groups = u2d.shape[1] // gc
    nchunk = seq // L
    rows = bsz * nchunk
    n = L * gc
    ug = u2d.reshape(bsz, nchunk, L, groups, gc).transpose(3, 0, 1, 4, 2).reshape(groups, rows, n)
    row3 = lambda a: a.reshape(groups, 1, P).astype(F32)
    col3 = lambda a: a.reshape(groups, P, 1).astype(F32)
    dt_b = jnp.broadcast_to(prm["log_dt"][:, None], (groups, P))
    args = [ug, row3(prm["a_re"]), row3(prm["a_im"]), row3(dt_b),
            col3(prm["a_re"]), col3(prm["a_im"]), col3(dt_b),
            prm["b_re"].transpose(0, 2, 1), prm["b_im"].transpose(0, 2, 1),
            prm["c_re"], prm["c_im"]]
    g3 = lambda g: (g, 0, 0)
    specs = [pl.BlockSpec((1, rows, n), g3)]
    specs += [pl.BlockSpec((1, 1, P), g3)] * 3 + [pl.BlockSpec((1, P, 1), g3)] * 3
    specs += [pl.BlockSpec((1, gc, P), g3)] * 4
    yg = pl.pallas_call(
        functools.partial(_s5_kernel, bsz=bsz, nchunk=nchunk),
        grid=(groups,),
        in_specs=specs,
        out_specs=pl.BlockSpec((1, rows, n), g3),
        out_shape=jax.ShapeDtypeStruct((groups, rows, n), F32),
        scratch_shapes=[pltpu.VMEM((n, n), BF16)],
        compiler_params=_cparams("parallel"),
        name="s5_core",
    )(*args)
    return yg.reshape(groups, bsz, nchunk, gc, L).transpose(1, 2, 4, 0, 3).reshape(bsz * seq, groups * gc)


def _gelu_tanh(x):
    return 0.5 * x * (1.0 + jnp.tanh(0.7978845608028654 * (x + 0.044715 * x * x * x)))


def _out_kernel(ya_ref, yb_ref, yc_ref, u_ref, x_ref, gt1_ref, sh2_ref, sc2_ref, g_ref, d_ref,
                gw_ref, gbias_ref, wo_ref, wrh_ref, wrl_ref, br_ref,
                x1_ref, h2_ref, ri_ref, rw_ref, hist_ref, *, da, db):
    yc = _gelu_tanh(yc_ref[...] + d_ref[...] * u_ref[...])
    yc = yc * _sigmoid(_dot(yc, gw_ref[...]) + gbias_ref[...])
    mixed = (_dot(ya_ref[...], wo_ref[0:da, :]) + _dot(yb_ref[...], wo_ref[da:da + db, :])
             + _dot(yc, wo_ref[da + db:, :]))
    x1 = x_ref[...] + (1.0 + gt1_ref[0, 0]) * mixed
    x1_ref[...] = x1
    h2 = _rmsnorm(x1, g_ref[...]) * (1.0 + sc2_ref[0, 0]) + sh2_ref[0, 0]
    h2_ref[...] = h2

    hh, hl = _split(h2)
    logits = (jnp.dot(hh, wrh_ref[...], preferred_element_type=F32)
              + jnp.dot(hh, wrl_ref[...], preferred_element_type=F32)
              + jnp.dot(hl, wrh_ref[...], preferred_element_type=F32)) + br_ref[...]
    lane_i = lax.broadcasted_iota(jnp.int32, logits.shape, 1)
    lane = lane_i.astype(F32)
    big = float(LANES)
    neg = -jnp.inf
    is_g = lane_i < N_GROUPS
    lg = jnp.where(is_g, logits, neg)
    gmax = jnp.max(lg, axis=1, keepdims=True)
    gi = jnp.min(jnp.where(is_g & (lg == gmax), lane, big), axis=1, keepdims=True)
    gp = 1.0 / jnp.sum(jnp.where(is_g, jnp.exp(lg - gmax), 0.0), axis=1, keepdims=True)
    e_lane = lane_i - N_GROUPS
    grp_of_lane = lax.shift_right_arithmetic(e_lane, 3).astype(F32)
    in_grp = (e_lane >= 0) & (e_lane < N_EXPERTS) & (grp_of_lane == gi)
    l1 = jnp.where(in_grp, logits, neg)
    m1 = jnp.max(l1, axis=1, keepdims=True)
    i1 = jnp.min(jnp.where(in_grp & (l1 == m1), lane, big), axis=1, keepdims=True)
    rest = in_grp & (lane != i1)
    l2 = jnp.where(rest, logits, neg)
    m2 = jnp.max(l2, axis=1, keepdims=True)
    i2 = jnp.min(jnp.where(rest & (l2 == m2), lane, big), axis=1, keepdims=True)
    e2 = jnp.exp(m2 - m1)
    w1 = gp / (1.0 + e2)
    w2 = gp * e2 / (1.0 + e2)
    hot1 = (lane == i1).astype(F32)
    hot2 = (lane == i2).astype(F32)
    both = (hot1 + hot2).astype(BF16)
    tm = logits.shape[0]
    rr = lax.broadcasted_iota(jnp.int32, (tm, tm), 0)
    cc = lax.broadcasted_iota(jnp.int32, (tm, tm), 1)
    before = jnp.dot((cc < rr).astype(BF16), both, preferred_element_type=F32)
    rank1 = jnp.sum(before * hot1, axis=1, keepdims=True)
    rank2 = jnp.sum(before * hot2, axis=1, keepdims=True)
    hist_ref[0] = jnp.sum(hot1 + hot2, axis=0, keepdims=True)
    ids = jnp.where(lane_i == 0, i1, jnp.where(lane_i == 1, i2, jnp.where(lane_i == 2, rank1, rank2)))
    ri_ref[...] = ids.astype(jnp.int32)
    rw_ref[...] = jnp.where(lane_i == 0, w1, jnp.where(lane_i == 1, w2, 0.0))


def _out_proj(ya, yb, yc, u, x2d, mod_l, g_ffn, s5_d, glu_w, glu_b, w_out, w_rg, b_rg, w_re, b_re, seq):
    n_tok, d = x2d.shape
    da, db, dc = ya.shape[1], yb.shape[1], yc.shape[1]
    tm = _row_tile(seq, 512)
    per_b = seq // tm
    wr = jnp.zeros((d, LANES), F32).at[:, :N_GROUPS].set(w_rg).at[:, N_GROUPS:N_GROUPS + N_EXPERTS].set(w_re)
    wr_hi = wr.astype(BF16)
    wr_lo = (wr - wr_hi.astype(F32)).astype(BF16)
    br = jnp.zeros((1, LANES), F32).at[0, :N_GROUPS].set(b_rg).at[0, N_GROUPS:N_GROUPS + N_EXPERTS].set(b_re)
    row = lambda i: (i, 0)
    const = lambda i: (0, 0)
    modspec = lambda j: pl.BlockSpec((1, 1, 1, d), lambda i: (i // per_b, j, 0, 0))
    return pl.pallas_call(
        functools.partial(_out_kernel, da=da, db=db),
        grid=(n_tok // tm,),
        in_specs=[pl.BlockSpec((tm, da), row), pl.BlockSpec((tm, db), row), pl.BlockSpec((tm, dc), row),
                  pl.BlockSpec((tm, dc), row), pl.BlockSpec((tm, d), row),
                  modspec(2), modspec(3), modspec(4),
                  pl.BlockSpec((1, d), const), pl.BlockSpec((1, dc), const),
                  pl.BlockSpec((dc, dc), const), pl.BlockSpec((1, dc), const),
                  pl.BlockSpec((d, d), const), pl.BlockSpec((d, LANES), const),
                  pl.BlockSpec((d, LANES), const), pl.BlockSpec((1, LANES), const)],
        out_specs=[pl.BlockSpec((tm, d), row), pl.BlockSpec((tm, d), row),
                   pl.BlockSpec((tm, LANES), row), pl.BlockSpec((tm, LANES), row),
                   pl.BlockSpec((1, 1, LANES), lambda i: (i, 0, 0))],
        out_shape=[jax.ShapeDtypeStruct((n_tok, d), F32), jax.ShapeDtypeStruct((n_tok, d), F32),
                   jax.ShapeDtypeStruct((n_tok, LANES), jnp.int32),
                   jax.ShapeDtypeStruct((n_tok, LANES), F32),
                   jax.ShapeDtypeStruct((n_tok // tm, 1, LANES), F32)],
        compiler_params=_cparams("parallel"),
        name="out_proj_router",
    )(ya, yb, yc, u, x2d, mod_l, mod_l, mod_l, g_ffn, s5_d.reshape(1, dc), glu_w.astype(BF16),
      glu_b.reshape(1, dc), w_out.astype(BF16), wr_hi, wr_lo, br)


ROW_UNROLL = 8


def _route(ri, hist, n_tok, tm):
    ntile = n_tok // tm
    h = hist.reshape(ntile, LANES)[:, N_GROUPS:N_GROUPS + N_EXPERTS].astype(jnp.int32)
    counts = jnp.sum(h, axis=0)
    pcounts = (counts + MOE_ROWS - 1) // MOE_ROWS * MOE_ROWS
    pends = jnp.cumsum(pcounts)
    base = (pends - pcounts)[None, :] + jnp.cumsum(h, axis=0) - h
    n_rows = 2 * n_tok + N_EXPERTS * MOE_ROWS
    nblk = n_rows // MOE_ROWS
    blk_e = jnp.minimum(jnp.searchsorted(pends, jnp.arange(nblk, dtype=jnp.int32) * MOE_ROWS, side="right"),
                        N_EXPERTS - 1).astype(jnp.int32)
    eid = (ri[:, 0:2] - N_GROUPS).reshape(ntile, tm, 2, 1)
    hot = eid == jnp.arange(N_EXPERTS, dtype=jnp.int32)
    dest = jnp.sum(jnp.where(hot, base[:, None, None, :], 0), axis=-1) + ri[:, 2:4].reshape(ntile, tm, 2)
    return dest.reshape(n_tok, 2).astype(jnp.int32), blk_e, n_rows


def _tile_rows(dest, tm):
    nblk = dest.shape[0] // tm
    return dest.reshape(nblk, tm, 2).transpose(0, 2, 1).reshape(nblk, 1, 2 * tm)


def _dispatch_kernel(d_ref, h_hbm, zero_hbm, xs_hbm, sem, *, tm):
    del zero_hbm
    i = pl.program_id(0)
    n = pl.num_programs(0)
    slot = i % 2

    def wait_tile(s):
        pltpu.make_async_copy(h_hbm.at[pl.ds(0, 2 * tm), :], xs_hbm.at[pl.ds(0, 2 * tm), :], sem.at[s]).wait()

    @pl.when(i > 0)
    def _():
        wait_tile(1 - slot)

    def body(g, carry):
        for u in range(ROW_UNROLL):
            r = g * ROW_UNROLL + u
            src = h_hbm.at[pl.ds(i * tm + r, 1), :]
            pltpu.make_async_copy(src, xs_hbm.at[pl.ds(d_ref[0, 0, r], 1), :], sem.at[slot]).start(priority=0)
            pltpu.make_async_copy(src, xs_hbm.at[pl.ds(d_ref[0, 0, tm + r], 1), :], sem.at[slot]).start(priority=1)
        return carry
    lax.fori_loop(0, tm // ROW_UNROLL, body, 0)

    @pl.when(i == n - 1)
    def _():
        wait_tile(slot)


def _dispatch(h2, dest, n_rows, seq):
    n_tok, d = h2.shape
    tm = _row_tile(seq, 512)
    nblk = n_tok // tm
    return pl.pallas_call(
        functools.partial(_dispatch_kernel, tm=tm),
        grid=(nblk,),
        in_specs=[pl.BlockSpec((1, 1, 2 * tm), lambda i: (i, 0, 0), memory_space=pltpu.SMEM),
                  pl.BlockSpec(memory_space=pl.ANY),
                  pl.BlockSpec(memory_space=pl.ANY)],
        out_specs=pl.BlockSpec(memory_space=pl.ANY),
        out_shape=jax.ShapeDtypeStruct((n_rows, d), F32),
        scratch_shapes=[pltpu.SemaphoreType.DMA((2,))],
        input_output_aliases={2: 0},
        compiler_params=_cparams("arbitrary"),
        name="moe_dispatch",
    )(_tile_rows(dest, tm), h2, jnp.zeros((n_rows, d), F32))


def _moe_kernel(blk_e_ref, x_ref, w13_ref, w2_ref, y_ref, *, d_exp):
    del blk_e_ref
    hcat = jnp.dot(x_ref[...].astype(BF16), w13_ref[0], preferred_element_type=F32)
    act = _silu(hcat[:, :d_exp]) * hcat[:, d_exp:]
    y_ref[...] = jnp.dot(act.astype(BF16), w2_ref[0], preferred_element_type=F32)


def _moe_experts(xs, blk_e, w13, w2):
    n_rows, d = xs.shape
    d_exp = w2.shape[1]
    grid_spec = pltpu.PrefetchScalarGridSpec(
        num_scalar_prefetch=1,
        grid=(n_rows // MOE_ROWS,),
        in_specs=[pl.BlockSpec((MOE_ROWS, d), lambda i, e: (i, 0)),
                  pl.BlockSpec((1, d, 2 * d_exp), lambda i, e: (e[i], 0, 0)),
                  pl.BlockSpec((1, d_exp, d), lambda i, e: (e[i], 0, 0))],
        out_specs=pl.BlockSpec((MOE_ROWS, d), lambda i, e: (i, 0)),
    )
    return pl.pallas_call(
        functools.partial(_moe_kernel, d_exp=d_exp),
        grid_spec=grid_spec,
        out_shape=jax.ShapeDtypeStruct((n_rows, d), F32),
        compiler_params=_cparams("arbitrary"),
        name="moe_experts",
    )(blk_e, xs, w13, w2)


def _comb_kernel(d_ref, d_next_ref, x_ref, rw_ref, gt_ref, y_hbm, g_ref, o_ref, ybuf, sem, *, tm, final):
    i = pl.program_id(0)
    n = pl.num_programs(0)
    slot = i % 2

    def start_all(idx_ref, s):
        def body(g, carry):
            for u in range(ROW_UNROLL):
                r = g * ROW_UNROLL + u
                pltpu.make_async_copy(y_hbm.at[pl.ds(idx_ref[0, 0, r], 1), :], ybuf.at[s, pl.ds(r, 1), :],
                                      sem.at[s]).start(priority=u % 2)
            return carry
        lax.fori_loop(0, 2 * tm // ROW_UNROLL, body, 0)

    @pl.when(i == 0)
    def _():
        start_all(d_ref, 0)

    @pl.when(i + 1 < n)
    def _():
        start_all(d_next_ref, 1 - slot)

    pltpu.make_async_copy(y_hbm.at[pl.ds(0, 2 * tm), :], ybuf.at[slot], sem.at[slot]).wait()

    w = rw_ref[...]
    moe = w[:, 0:1] * ybuf[slot, 0:tm, :] + w[:, 1:2] * ybuf[slot, tm:2 * tm, :]
    x2 = x_ref[...] + (1.0 + gt_ref[0, 0]) * moe
    o_ref[...] = _rmsnorm(x2, g_ref[...]) if final else x2


def _combine(x1, y_rows, dest, rw, mod_l, g_final, seq, final):
    n_tok, d = x1.shape
    tm = _row_tile(seq, 512)
    per_b = seq // tm
    nblk = n_tok // tm
    idx = _tile_rows(dest, tm)
    return pl.pallas_call(
        functools.partial(_comb_kernel, tm=tm, final=final),
        grid=(nblk,),
        in_specs=[pl.BlockSpec((1, 1, 2 * tm), lambda i: (i, 0, 0), memory_space=pltpu.SMEM),
                  pl.BlockSpec((1, 1, 2 * tm), lambda i: (jnp.minimum(i + 1, nblk - 1), 0, 0),
                               memory_space=pltpu.SMEM),
                  pl.BlockSpec((tm, d), lambda i: (i, 0)),
                  pl.BlockSpec((tm, LANES), lambda i: (i, 0)),
                  pl.BlockSpec((1, 1, 1, d), lambda i: (i // per_b, 5, 0, 0)),
                  pl.BlockSpec(memory_space=pl.ANY),
                  pl.BlockSpec((1, d), lambda i: (0, 0))],
        out_specs=pl.BlockSpec((tm, d), lambda i: (i, 0)),
        out_shape=jax.ShapeDtypeStruct((n_tok, d), F32),
        scratch_shapes=[pltpu.VMEM((2, 2 * tm, d), F32), pltpu.SemaphoreType.DMA((2,))],
        compiler_params=_cparams("arbitrary"),
        name="moe_combine",
    )(idx, idx, x1, rw, mod_l, y_rows, g_final)


def kernel(x, c, ada_w, ada_b, norm_mix, norm_ffn, norm_final, w_in, w_out, mlstm_conv_w, mlstm_conv_b, mlstm_w_q, mlstm_w_k, mlstm_b_i, mlstm_b_f, mlstm_norm_w, rwkv_mu, rwkv_w0, rwkv_w_up, rwkv_a0, rwkv_a_up, rwkv_g_up, rwkv_k_k, rwkv_k_a, rwkv_r_k, rwkv_ln_w, rwkv_ln_b, rwkv_v0, rwkv_v_dn, rwkv_v_up, s5_a_re, s5_a_im, s5_log_dt, s5_b_re, s5_b_im, s5_c_re, s5_c_im, s5_d, s5_glu_w, s5_glu_b, moe_w_rg, moe_b_rg, moe_w_re, moe_b_re, moe_w1, moe_w3, moe_w2):
    bsz, seq, d = x.shape
    depth = ada_w.shape[0]
    n_tok = bsz * seq
    heads_a = mlstm_w_q.shape[1]
    da = heads_a * HEAD_DIM
    db = rwkv_w0.shape[1]
    dc = s5_d.shape[1]
    rw_cols = rwkv_mu.shape[1]
    assert seq % CHUNK == 0 and w_in.shape[2] == 3 * da + 2 * heads_a + rw_cols + dc
    widths = (3 * da, LANES, rw_cols, dc)

    mod = _modulation(c, ada_w, ada_b).reshape(depth, bsz, 6, 1, d)
    xc = x.reshape(n_tok, d)
    v_first = None
    for l in range(depth):
        mod_l = mod[l]
        w = w_in[l]
        gate_pad = jnp.zeros((d, LANES - 2 * heads_a), w.dtype)
        w_pad = jnp.concatenate([w[:, :3 * da], w[:, 3 * da:3 * da + 2 * heads_a], gate_pad,
                                 w[:, 3 * da + 2 * heads_a:]], axis=1).astype(BF16)
        qkvo, gates, pcols, u = _in_proj(xc, mod_l, norm_mix[l].reshape(1, d), w_pad, widths, seq)

        ya = _mlstm(qkvo, gates, mlstm_conv_w[l], mlstm_conv_b[l], mlstm_w_q[l], mlstm_w_k[l],
                    mlstm_b_i[l], mlstm_b_f[l], mlstm_norm_w[l], bsz, seq)
        rprm = dict(mu=rwkv_mu[l], w0=rwkv_w0[l], w_up=rwkv_w_up[l], a0=rwkv_a0[l], a_up=rwkv_a_up[l],
                    g_up=rwkv_g_up[l], k_k=rwkv_k_k[l], k_a=rwkv_k_a[l], r_k=rwkv_r_k[l],
                    ln_w=rwkv_ln_w[l], ln_b=rwkv_ln_b[l])
        if l > 0:
            rprm.update(v0=rwkv_v0[l - 1], v_dn=rwkv_v_dn[l - 1], v_up=rwkv_v_up[l - 1])
        yb, v_first = _rwkv(pcols, v_first if l > 0 else None, rprm, bsz, seq)
        sprm = dict(a_re=s5_a_re[l], a_im=s5_a_im[l], log_dt=s5_log_dt[l], b_re=s5_b_re[l],
                    b_im=s5_b_im[l], c_re=s5_c_re[l], c_im=s5_c_im[l])
        yc = _s5_core(u, sprm, bsz, seq)

        x1, h2, ri, rw, hist = _out_proj(ya, yb, yc, u, xc, mod_l, norm_ffn[l].reshape(1, d), s5_d[l],
                                         s5_glu_w[l], s5_glu_b[l], w_out[l], moe_w_rg[l], moe_b_rg[l],
                                         moe_w_re[l], moe_b_re[l], seq)
        dest, blk_e, n_rows = _route(ri, hist, n_tok, n_tok // hist.shape[0])
        w13 = jnp.concatenate([moe_w1[l], moe_w3[l]], axis=2).astype(BF16)
        y_rows = _moe_experts(_dispatch(h2, dest, n_rows, seq), blk_e, w13, moe_w2[l].astype(BF16))
        xc = _combine(x1, y_rows, dest, rw, mod_l, norm_final.reshape(1, d), seq, final=(l == depth - 1))
    return xc.reshape(bsz, seq, d).astype(x.dtype)
```

```python
import functools

import jax
import jax.numpy as jnp
from jax import lax
from jax.experimental import pallas as pl
from jax.experimental.pallas import tpu as pltpu

F32 = jnp.float32
BF16 = jnp.bfloat16

HEAD_DIM = 64
CHUNK = 64
CONV_K = 4
S5_GC = 16
S5_P = 64
N_GROUPS = 4
EXPERTS_PER_GROUP = 8
N_EXPERTS = N_GROUPS * EXPERTS_PER_GROUP
NORM_EPS = 1e-6
HEAD_NORM_EPS = 1e-5
RWKV_GN_EPS = 64e-5
L2_EPS = 1e-12
LANES = 128
SUBLANES = 8
MOE_ROWS = 256
VMEM_LIMIT = 56 * 1024 * 1024


def _cparams(*sem):
    return pltpu.CompilerParams(dimension_semantics=sem, vmem_limit_bytes=VMEM_LIMIT)


def _row_tile(n, want):
    t = min(n, want)
    assert n % t == 0
    return t


def _dot(a, b):
    return jnp.dot(a.astype(BF16), b.astype(BF16), preferred_element_type=F32)


def _dot_nt(a, b):
    return lax.dot_general(a.astype(BF16), b.astype(BF16), (((1,), (1,)), ((), ())),
                           preferred_element_type=F32)


def _dot_tn(a, b):
    return lax.dot_general(a.astype(BF16), b.astype(BF16), (((0,), (0,)), ((), ())),
                           preferred_element_type=F32)


def _split(a):
    hi = a.astype(BF16)
    lo = (a - hi.astype(F32)).astype(BF16)
    return hi, lo


def _dot_xa(a, b_exact):
    hi, lo = _split(a)
    return (jnp.dot(hi, b_exact, preferred_element_type=F32)
            + jnp.dot(lo, b_exact, preferred_element_type=F32))


def _dot_xb(a_exact, b):
    hi, lo = _split(b)
    return (jnp.dot(a_exact, hi, preferred_element_type=F32)
            + jnp.dot(a_exact, lo, preferred_element_type=F32))


def _dot_x3(a, b_exact):
    hi = a.astype(BF16)
    r1 = a - hi.astype(F32)
    mid = r1.astype(BF16)
    lo = (r1 - mid.astype(F32)).astype(BF16)
    return (jnp.dot(hi, b_exact, preferred_element_type=F32)
            + jnp.dot(mid, b_exact, preferred_element_type=F32)
            + jnp.dot(lo, b_exact, preferred_element_type=F32))


def _dot3(a, b):
    ah, al = _split(a)
    bh, bl = _split(b)
    return (jnp.dot(ah, bh, preferred_element_type=F32)
            + jnp.dot(ah, bl, preferred_element_type=F32)
            + jnp.dot(al, bh, preferred_element_type=F32))


def _dot3_nt(a, b):
    ah, al = _split(a)
    bh, bl = _split(b)
    dn = (((1,), (1,)), ((), ()))
    return (lax.dot_general(ah, bh, dn, preferred_element_type=F32)
            + lax.dot_general(ah, bl, dn, preferred_element_type=F32)
            + lax.dot_general(al, bh, dn, preferred_element_type=F32))


def _sigmoid(x):
    return 1.0 / (1.0 + jnp.exp(-x))


def _silu(x):
    return x * _sigmoid(x)


def _log_sigmoid(x):
    return jnp.minimum(x, 0.0) - jnp.log1p(jnp.exp(-jnp.abs(x)))


def _rmsnorm(x, g):
    ms = jnp.mean(x * x, axis=-1, keepdims=True)
    return x * lax.rsqrt(ms + NORM_EPS) * g


def _tri_incl(n):
    r = lax.broadcasted_iota(jnp.int32, (n, n), 0)
    c = lax.broadcasted_iota(jnp.int32, (n, n), 1)
    return (c <= r).astype(BF16)


def _head_ones(width):
    r = lax.broadcasted_iota(jnp.int32, (width, width), 0) // HEAD_DIM
    c = lax.broadcasted_iota(jnp.int32, (width, width), 1) // HEAD_DIM
    return (r == c).astype(BF16)


def _mod_kernel(c_ref, w_ref, b_ref, o_ref):
    o_ref[0] = _dot(_silu(c_ref[...]), w_ref[0]) + b_ref[0]


def _modulation(c, ada_w, ada_b):
    depth, d, d6 = ada_w.shape
    bsz = c.shape[0]
    tn = _row_tile(d6, 1024)
    return pl.pallas_call(
        _mod_kernel,
        grid=(depth, d6 // tn),
        in_specs=[pl.BlockSpec((bsz, d), lambda l, j: (0, 0)),
                  pl.BlockSpec((1, d, tn), lambda l, j: (l, 0, j)),
                  pl.BlockSpec((1, 1, tn), lambda l, j: (l, 0, j))],
        out_specs=pl.BlockSpec((1, bsz, tn), lambda l, j: (l, 0, j)),
        out_shape=jax.ShapeDtypeStruct((depth, bsz, d6), F32),
        compiler_params=_cparams("parallel", "parallel"),
        name="adaln_mod",
    )(c, ada_w, ada_b.reshape(depth, 1, d6))


def _in_kernel(x_ref, sh_ref, sc_ref, g_ref, w_ref, *out_refs, widths):
    h = _rmsnorm(x_ref[...], g_ref[...]) * (1.0 + sc_ref[0, 0]) + sh_ref[0, 0]
    hb = h.astype(BF16)
    off = 0
    for o_ref, wd in zip(out_refs, widths):
        o_ref[...] = jnp.dot(hb, w_ref[:, off:off + wd], preferred_element_type=F32)
        off += wd


def _in_proj(x2d, mod_l, g, w_pad, widths, seq):
    n_tok, d = x2d.shape
    tm = _row_tile(seq, 512)
    per_b = seq // tm
    tot = sum(widths)
    return pl.pallas_call(
        functools.partial(_in_kernel, widths=widths),
        grid=(n_tok // tm,),
        in_specs=[pl.BlockSpec((tm, d), lambda i: (i, 0)),
                  pl.BlockSpec((1, 1, 1, d), lambda i: (i // per_b, 0, 0, 0)),
                  pl.BlockSpec((1, 1, 1, d), lambda i: (i // per_b, 1, 0, 0)),
                  pl.BlockSpec((1, d), lambda i: (0, 0)),
                  pl.BlockSpec((d, tot), lambda i: (0, 0))],
        out_specs=[pl.BlockSpec((tm, wd), lambda i: (i, 0)) for wd in widths],
        out_shape=[jax.ShapeDtypeStruct((n_tok, wd), F32) for wd in widths],
        compiler_params=_cparams("parallel"),
        name="in_proj",
    )(x2d, mod_l, mod_l, g, w_pad)


def _mlstm_kernel(qkvo_ref, gate_ref, cw_ref, cb_ref, wq_ref, wk_ref, gb_ref, nw_ref, out_ref,
                  xf_sc, q_sc, k_sc, gi_sc, gf_sc, cn_sc, m_sc, *, tb, heads):
    dh, L = HEAD_DIM, CHUNK
    da = heads * dh
    i = pl.program_id(1)

    @pl.when(i == 0)
    def _():
        xf_sc[0:SUBLANES, :] = jnp.zeros((SUBLANES, da), F32)
        cn_sc[...] = jnp.zeros_like(cn_sc)
        m_sc[...] = jnp.zeros_like(m_sc)

    xqk = qkvo_ref[:, 0:da]
    xf_sc[pl.ds(SUBLANES, tb), :] = xqk
    acc = xqk * cw_ref[CONV_K - 1:CONV_K, :] + cb_ref[...]
    for j in range(1, CONV_K):
        acc = acc + xf_sc[pl.ds(SUBLANES - j, tb), :] * cw_ref[CONV_K - 1 - j:CONV_K - j, :]
    xf_sc[0:SUBLANES, :] = xf_sc[pl.ds(tb, SUBLANES), :]
    cx = _silu(acc).astype(BF16)
    q_sc[...] = jnp.dot(cx, wq_ref[...], preferred_element_type=F32)
    k_sc[...] = jnp.dot(cx, wk_ref[...], preferred_element_type=F32) * (dh ** -0.5)

    g = gate_ref[...] + gb_ref[...]
    gi_sc[...] = g[:, :LANES]
    gf_sc[...] = _log_sigmoid(g[:, LANES:])

    tri = _tri_incl(L)
    rr = lax.broadcasted_iota(jnp.int32, (L, L), 0)
    cc = lax.broadcasted_iota(jnp.int32, (L, L), 1)
    causal = cc <= rr
    row_l = lax.broadcasted_iota(jnp.int32, (L, LANES), 0)
    sel = (lax.broadcasted_iota(jnp.int32, (LANES, heads * LANES), 0)
           == lax.broadcasted_iota(jnp.int32, (LANES, heads * LANES), 1) // LANES).astype(BF16)
    ones_v = jnp.ones((L, dh), F32)
    mean_m = jnp.full((dh, dh), 1.0 / dh, BF16)

    def chunk(c, carry):
        rows = pl.ds(pl.multiple_of(c * L, L), L)
        hr = range(heads)
        bc = _dot_xb(tri, gf_sc[rows, :])
        gtot = bc[L - 1:L, :]
        g = gi_sc[rows, :] - bc
        gmax = jnp.max(g, axis=0, keepdims=True)
        m_prev = m_sc[...]
        m_loc = gtot + gmax
        m_new = jnp.maximum(gtot + m_prev, m_loc)
        a_old = jnp.exp(gtot + m_prev - m_new)
        a_loc = jnp.exp(m_loc - m_new)
        m_sc[...] = m_new
        pm = g
        d = 1
        while d < L:
            pm = jnp.maximum(pm, jnp.where(row_l >= d, pltpu.roll(pm, d, 0), -jnp.inf))
            d *= 2
        mm = jnp.maximum(m_prev, pm)
        stack = jnp.concatenate([jnp.exp(g - gmax), mm, jnp.exp(m_prev - mm), jnp.exp(-(bc + mm)),
                                 a_old, a_loc, jnp.zeros((SUBLANES - 2, LANES), F32)], axis=0)
        ex = _dot_x3(stack, sel)
        g_t = g.T

        def part(k, h, width=dh):
            return ex[k * L:(k + 1) * L, h * LANES:h * LANES + width]

        qc = [q_sc[rows, h * dh:(h + 1) * dh] for h in hr]
        kc = [k_sc[rows, h * dh:(h + 1) * dh] for h in hr]
        vo = [jnp.concatenate([qkvo_ref[rows, da + h * dh:da + (h + 1) * dh], ones_v], axis=1) for h in hr]
        oc = [qkvo_ref[rows, 2 * da + h * dh:2 * da + (h + 1) * dh] for h in hr]
        cn_prev = [cn_sc[h] for h in hr]
        s_raw = [_dot_nt(qc[h], kc[h]) for h in hr]
        q_cn = [_dot(qc[h], cn_prev[h]) for h in hr]
        cn_loc = [_dot_tn(kc[h] * part(0, h), vo[h]) for h in hr]
        s_qk = [s_raw[h] * jnp.where(causal, jnp.exp(g_t[h:h + 1, :] - part(1, h)), 0.0) for h in hr]
        s_vn = [_dot(s_qk[h], vo[h]) for h in hr]
        nd = [part(2, h, 2 * dh) * q_cn[h] + s_vn[h] for h in hr]
        hh = [nd[h][:, :dh] / jnp.maximum(jnp.abs(nd[h][:, dh:]), part(3, h)) for h in hr]
        mu = [_dot_xa(hh[h], mean_m) for h in hr]
        dlt = [hh[h] - mu[h] for h in hr]
        var = [_dot_xa(dlt[h] * dlt[h], mean_m) for h in hr]
        outs = [dlt[h] * lax.rsqrt(var[h] + HEAD_NORM_EPS) * _sigmoid(oc[h]) for h in hr]
        out_ref[rows, :] = jnp.concatenate(outs, axis=1) * nw_ref[...]
        for h in hr:
            cn_sc[h] = (ex[4 * L:4 * L + 1, h * LANES:(h + 1) * LANES] * cn_prev[h]
                        + ex[4 * L + 1:4 * L + 2, h * LANES:(h + 1) * LANES] * cn_loc[h])
        return carry

    lax.fori_loop(0, tb // L, chunk, 0)


def _block_diag(w):
    heads, dh, _ = w.shape
    eye = jnp.eye(heads, dtype=w.dtype)
    return (eye[:, None, :, None] * w[:, :, None, :]).reshape(heads * dh, heads * dh)


def _mlstm(qkvo, gates, conv_w, conv_b, w_q, w_k, b_i, b_f, norm_w, bsz, seq):
    heads = w_q.shape[0]
    da = heads * HEAD_DIM
    tb = _row_tile(seq, 512)
    nblk = seq // tb
    gbias = jnp.zeros((1, 2 * LANES), F32).at[0, :heads].set(b_i).at[0, LANES:LANES + heads].set(b_f)
    kern = functools.partial(_mlstm_kernel, tb=tb, heads=heads)
    const = lambda b, i: (0, 0)
    return pl.pallas_call(
        kern,
        grid=(bsz, nblk),
        in_specs=[pl.BlockSpec((tb, 3 * da), lambda b, i: (b * nblk + i, 0)),
                  pl.BlockSpec((tb, 2 * LANES), lambda b, i: (b * nblk + i, 0)),
                  pl.BlockSpec((CONV_K, da), const),
                  pl.BlockSpec((1, da), const),
                  pl.BlockSpec((da, da), const),
                  pl.BlockSpec((da, da), const),
                  pl.BlockSpec((1, 2 * LANES), const),
                  pl.BlockSpec((1, da), const)],
        out_specs=pl.BlockSpec((tb, da), lambda b, i: (b * nblk + i, 0)),
        out_shape=jax.ShapeDtypeStruct((bsz * seq, da), F32),
        scratch_shapes=[pltpu.VMEM((tb + SUBLANES, da), F32),
                        pltpu.VMEM((tb, da), F32),
                        pltpu.VMEM((tb, da), F32),
                        pltpu.VMEM((tb, LANES), F32),
                        pltpu.VMEM((tb, LANES), F32),
                        pltpu.VMEM((heads, HEAD_DIM, 2 * HEAD_DIM), F32),
                        pltpu.VMEM((1, LANES), F32)],
        compiler_params=_cparams("parallel", "arbitrary"),
        name="mlstm",
    )(qkvo, gates, conv_w, conv_b.reshape(1, da), _block_diag(w_q).astype(BF16),
      _block_diag(w_k).astype(BF16), gbias, norm_w.reshape(1, da))


def _rwkv_kernel(*refs, tb, heads, lw_dim, la_dim, lg_dim, has_vres):
    dh, L = HEAD_DIM, CHUNK
    db = heads * dh
    it = iter(refs)
    p_ref = next(it)
    vf_ref = next(it) if has_vres else None
    (mu_ref, w0_ref, wup_ref, a0_ref, aup_ref, gup_ref, kk_ref, ka_ref, rk_ref,
     lnw_ref, lnb_ref) = (next(it) for _ in range(11))
    if has_vres:
        v0_ref, vdn_ref, vup_ref = (next(it) for _ in range(3))
    y_ref = next(it)
    vout_ref = None if has_vres else next(it)
    (xf_sc, r_sc, k_sc, v_sc, a_sc, b_sc, lw_sc, y_sc, st_sc, rp_sc, q_sc, z_sc,
     gl_sc) = (next(it) for _ in range(13))

    i = pl.program_id(1)

    @pl.when(i == 0)
    def _():
        xf_sc[0:SUBLANES, :] = jnp.zeros((SUBLANES, xf_sc.shape[1]), F32)
        st_sc[...] = jnp.zeros_like(st_sc)

    p = p_ref[...]
    xf_sc[pl.ds(SUBLANES, tb), :] = p
    prev = xf_sc[pl.ds(SUBLANES - 1, tb), :]
    xf_sc[0:SUBLANES, :] = xf_sc[pl.ds(tb, SUBLANES), :]
    p = p + mu_ref[...] * (prev - p)

    o = 0
    r = p[:, o:o + db]; o += db
    k = p[:, o:o + db]; o += db
    v = p[:, o:o + db]; o += db
    wd = p[:, o:o + lw_dim]; o += lw_dim
    ad = p[:, o:o + la_dim]; o += la_dim
    gd = p[:, o:o + lg_dim]

    ones_h = _head_ones(db)
    wlog = _log_sigmoid(w0_ref[...] + _dot(jnp.tanh(wd), wup_ref[...])) - 0.5
    lw_sc[...] = -jnp.exp(wlog)
    a = _sigmoid(a0_ref[...] + _dot(ad, aup_ref[...]))
    gate = _dot(_sigmoid(gd), gup_ref[...])
    if has_vres:
        v = v + (vf_ref[...] - v) * _sigmoid(v0_ref[...] + _dot(_dot(v, vdn_ref[...]), vup_ref[...]))
    else:
        vout_ref[...] = v
    kk = k * kk_ref[...]
    kk = kk / jnp.maximum(jnp.sqrt(_dot_xa(kk * kk, ones_h)), L2_EPS)
    k2 = k * (1.0 + (a - 1.0) * ka_ref[...])
    bonus = _dot_xa(r * k2 * rk_ref[...], ones_h) * v
    r_sc[...] = r
    k_sc[...] = k2
    v_sc[...] = v
    a_sc[...] = -kk
    b_sc[...] = kk * a

    tri = _tri_incl(L)
    rr = lax.broadcasted_iota(jnp.int32, (L, L), 0)
    cc = lax.broadcasted_iota(jnp.int32, (L, L), 1)
    strict = cc < rr
    incl = cc <= rr

    def chunk(c, carry):
        r0 = pl.multiple_of(c * L, L)
        rows = pl.ds(r0, L)
        lwc = lw_sc[rows, :]
        cum = _dot_xb(tri, lwc)
        cum_l = cum[L - 1:L, :]
        e_in = jnp.exp(cum)
        e_ex = jnp.exp(cum - lwc)
        e_inv = jnp.exp(-cum)
        e_end = jnp.exp(cum_l - cum)
        at = a_sc[rows, :] * e_ex
        rt = r_sc[rows, :] * e_in
        bv = b_sc[rows, :]
        kv = k_sc[rows, :]
        bt = bv * e_inv
        kt = kv * e_inv
        bg = bv * e_end
        kg = kv * e_end
        gl_sc[c] = jnp.exp(cum_l)
        vch = v_sc[rows, :]
        hr = range(heads)
        hsl = [slice(h * dh, (h + 1) * dh) for h in hr]
        vh = [vch[:, hsl[h]] for h in hr]
        g4 = [_dot_nt(jnp.concatenate([at[:, hsl[h]], rt[:, hsl[h]]], axis=0),
                      jnp.concatenate([bt[:, hsl[h]], kt[:, hsl[h]]], axis=0)) for h in hr]
        pw = [jnp.where(strict, g4[h][:L, :L], 0.0) for h in hr]
        n_ak = [jnp.where(strict, g4[h][:L, L:], 0.0) for h in hr]
        m_rb = [jnp.where(incl, g4[h][L:, :L], 0.0) for h in hr]
        m_rk = [jnp.where(incl, g4[h][L:, L:], 0.0) for h in hr]
        x = [jnp.concatenate([at[:, hsl[h]], _dot(n_ak[h], vh[h])], axis=1) for h in hr]
        for step in range(6):
            x = [x[h] + _dot(pw[h], x[h]) for h in hr]
            if step < 5:
                pw = [_dot(pw[h], pw[h]) for h in hr]
        ry = [jnp.concatenate([rt[:, hsl[h]], _dot(m_rk[h], vh[h])], axis=1) + _dot(m_rb[h], x[h])
              for h in hr]
        qz = [_dot_tn(x[h], bg[:, hsl[h]]) for h in hr]
        z2 = [_dot_tn(vh[h], kg[:, hsl[h]]) for h in hr]
        rp_sc[rows, :] = jnp.concatenate([ry[h][:, :dh] for h in hr], axis=1)
        y_sc[rows, :] = jnp.concatenate([ry[h][:, dh:] for h in hr], axis=1)
        for h in hr:
            q_sc[c, h] = qz[h][:dh]
            z_sc[c, h] = qz[h][dh:] + z2[h]
        return carry

    lax.fori_loop(0, tb // L, chunk, 0)

    def carry_state(c, carry):
        rows = pl.ds(pl.multiple_of(c * L, L), L)
        hr = range(heads)
        st = [st_sc[h] for h in hr]
        rp = rp_sc[rows, :]
        g_l = gl_sc[c]
        ys = [_dot_nt(rp[:, h * dh:(h + 1) * dh], st[h]) for h in hr]
        sq = [_dot_xa(st[h], q_sc[c, h].astype(BF16)) for h in hr]
        y_sc[rows, :] = y_sc[rows, :] + jnp.concatenate(ys, axis=1)
        for h in hr:
            st_sc[h] = st[h] * g_l[:, h * dh:(h + 1) * dh] + sq[h] + z_sc[c, h]
        return carry

    lax.fori_loop(0, tb // L, carry_state, 0)

    y = y_sc[...]
    mean = _dot_xa(y, ones_h) * (1.0 / dh)
    dlt = y - mean
    var = _dot_xa(dlt * dlt, ones_h) * (1.0 / dh)
    yn = dlt * lax.rsqrt(var + RWKV_GN_EPS) * lnw_ref[...] + lnb_ref[...]
    y_ref[...] = (yn + bonus) * gate


def _rwkv(pcols, v_first, prm, bsz, seq):
    db = prm["w0"].shape[0]
    heads = db // HEAD_DIM
    cols = pcols.shape[1]
    lw_dim, la_dim, lg_dim = prm["w_up"].shape[0], prm["a_up"].shape[0], prm["g_up"].shape[0]
    has_vres = v_first is not None
    tb = _row_tile(seq, 512)
    nblk = seq // tb
    row = lambda b, i: (b * nblk + i, 0)
    const = lambda b, i: (0, 0)
    vec = lambda a: a.reshape(1, -1).astype(F32)

    args = [pcols]
    specs = [pl.BlockSpec((tb, cols), row)]
    if has_vres:
        args.append(v_first)
        specs.append(pl.BlockSpec((tb, db), row))
    small = [vec(prm["mu"]), vec(prm["w0"]), prm["w_up"].astype(BF16), vec(prm["a0"]),
             prm["a_up"].astype(BF16), prm["g_up"].astype(BF16), vec(prm["k_k"]), vec(prm["k_a"]),
             vec(prm["r_k"]), vec(prm["ln_w"]), vec(prm["ln_b"])]
    if has_vres:
        lv = prm["v_dn"].shape[1]
        lvp = -(-lv // LANES) * LANES
        v_dn = jnp.zeros((db, lvp), F32).at[:, :lv].set(prm["v_dn"]).astype(BF16)
        v_up = jnp.zeros((lvp, db), F32).at[:lv, :].set(prm["v_up"]).astype(BF16)
        small += [vec(prm["v0"]), v_dn, v_up]
    args += small
    specs += [pl.BlockSpec(a.shape, const) for a in small]

    out_shape = [jax.ShapeDtypeStruct((bsz * seq, db), F32)]
    out_specs = [pl.BlockSpec((tb, db), row)]
    if not has_vres:
        out_shape.append(jax.ShapeDtypeStruct((bsz * seq, db), F32))
        out_specs.append(pl.BlockSpec((tb, db), row))

    kern = functools.partial(_rwkv_kernel, tb=tb, heads=heads, lw_dim=lw_dim, la_dim=la_dim,
                             lg_dim=lg_dim, has_vres=has_vres)
    res = pl.pallas_call(
        kern,
        grid=(bsz, nblk),
        in_specs=specs,
        out_specs=out_specs,
        out_shape=out_shape,
        scratch_shapes=[pltpu.VMEM((tb + SUBLANES, cols), F32)]
        + [pltpu.VMEM((tb, db), F32) for _ in range(7)]
        + [pltpu.VMEM((heads, HEAD_DIM, HEAD_DIM), F32),
           pltpu.VMEM((tb, db), F32),
           pltpu.VMEM((tb // CHUNK, heads, HEAD_DIM, HEAD_DIM), F32),
           pltpu.VMEM((tb // CHUNK, heads, HEAD_DIM, HEAD_DIM), F32),
           pltpu.VMEM((tb // CHUNK, 1, db), F32)],
        compiler_params=_cparams("parallel", "arbitrary"),
        name="rwkv7",
    )(*args)
    return (res[0], v_first) if has_vres else (res[0], res[1])


def _cmul(ar, ai, br, bi):
    return ar * br - ai * bi, ar * bi + ai * br


def _shift_rows(x, d):
    row = lax.broadcasted_iota(jnp.int32, x.shape, 0)
    return jnp.where(row >= d, pltpu.roll(x, d, 0), 0.0)


def _s5_kernel(u_ref, are_r, aim_r, dt_r, are_c, aim_c, dt_c, bre_ref, bim_ref, cre_ref, cim_ref,
               y_ref, m_sc, *, bsz, nchunk):
    L, gc, P = CHUNK, S5_GC, S5_P
    n = L * gc
    a_re, a_im, dt = are_r[0], aim_r[0], jnp.exp(dt_r[0])
    mag, ang = jnp.exp(a_re * dt), a_im * dt
    ab_re, ab_im = mag * jnp.cos(ang), mag * jnp.sin(ang)
    inv = 1.0 / (a_re * a_re + a_im * a_im)
    co_re = ((ab_re - 1.0) * a_re + ab_im * a_im) * inv
    co_im = (ab_im * a_re - (ab_re - 1.0) * a_im) * inv
    bb_re, bb_im = _cmul(co_re, co_im, bre_ref[0], bim_ref[0])
    c_re, c_im = cre_ref[0], cim_ref[0]

    def powers(tau):
        m = jnp.exp(tau * (a_re * dt))
        return m * jnp.cos(tau * ang), m * jnp.sin(tau * ang)

    a_re_c, a_im_c, dt_c_ = are_c[0], aim_c[0], jnp.exp(dt_c[0])
    tau_row = lax.broadcasted_iota(jnp.int32, (P, L), 1).astype(F32)
    pm = jnp.exp(tau_row * (a_re_c * dt_c_))
    pt_re = pm * jnp.cos(tau_row * (a_im_c * dt_c_))
    pt_im = pm * jnp.sin(tau_row * (a_im_c * dt_c_))
    pair = lax.broadcasted_iota(jnp.int32, (gc * gc, gc), 0)
    col = lax.broadcasted_iota(jnp.int32, (gc * gc, gc), 1)
    rep_c = (pair // gc == col).astype(BF16)
    rep_b = (pair % gc == col).astype(BF16)
    cb_re, cb_im = _cmul(_dot_xb(rep_c, c_re), _dot_xb(rep_c, c_im),
                         _dot_xb(rep_b, bb_re), _dot_xb(rep_b, bb_im))
    kap = _dot3(cb_re, pt_re) - _dot3(cb_im, pt_im)

    kap_pad = jnp.concatenate([kap, jnp.zeros_like(kap)], axis=1)
    srow = lax.broadcasted_iota(jnp.int32, (L, LANES), 0)
    tcol = lax.broadcasted_iota(jnp.int32, (L, LANES), 1)
    for cp in range(gc):
        for c2 in range(0, gc, 2):
            k0 = jnp.broadcast_to(kap_pad[c2 * gc + cp:c2 * gc + cp + 1, :], (L, LANES))
            k1 = jnp.broadcast_to(kap_pad[(c2 + 1) * gc + cp:(c2 + 1) * gc + cp + 1, :], (L, LANES))
            t0 = pltpu.roll(k0, 0, 1, stride=1, stride_axis=0)
            t1 = pltpu.roll(k1, L, 1, stride=1, stride_axis=0)
            blk = jnp.where(tcol < L, jnp.where(tcol >= srow, t0, 0.0),
                            jnp.where(tcol - L >= srow, t1, 0.0))
            m_sc[cp * L:(cp + 1) * L, c2 * L:(c2 + 2) * L] = blk.astype(BF16)

    s_col = lax.broadcasted_iota(jnp.int32, (L, P), 0).astype(F32)
    pw_re, pw_im = powers((L - 1.0) - s_col)
    pg_re, pg_im = powers(s_col + 1.0)
    w_re, w_im, g_re, g_im = [], [], [], []
    for c in range(gc):
        br = jnp.broadcast_to(bb_re[c:c + 1, :], (L, P))
        bi = jnp.broadcast_to(bb_im[c:c + 1, :], (L, P))
        wr, wi = _cmul(br, bi, pw_re, pw_im)
        w_re.append(wr); w_im.append(wi)
        cr = jnp.broadcast_to(c_re[c:c + 1, :], (L, P))
        ci = jnp.broadcast_to(c_im[c:c + 1, :], (L, P))
        gr, gi = _cmul(cr, ci, pg_re, pg_im)
        g_re.append(gr); g_im.append(gi)
    w_re, w_im = jnp.concatenate(w_re, axis=0), jnp.concatenate(w_im, axis=0)
    g_re, g_im = jnp.concatenate(g_re, axis=0), jnp.concatenate(g_im, axis=0)

    u = u_ref[0].astype(BF16)
    x_re = jnp.dot(u, w_re.astype(BF16), preferred_element_type=F32)
    x_im = jnp.dot(u, w_im.astype(BF16), preferred_element_type=F32)
    xs_re, xs_im = [], []
    for b in range(bsz):
        xr = x_re[b * nchunk:(b + 1) * nchunk]
        xi = x_im[b * nchunk:(b + 1) * nchunk]
        d = 1
        while d < nchunk:
            ar_, ai_ = powers(float(L * d))
            sr, si = _cmul(ar_, ai_, _shift_rows(xr, d), _shift_rows(xi, d))
            xr, xi = xr + sr, xi + si
            d *= 2
        xs_re.append(_shift_rows(xr, 1))
        xs_im.append(_shift_rows(xi, 1))
    xs_re, xs_im = jnp.concatenate(xs_re, axis=0), jnp.concatenate(xs_im, axis=0)
    y = jnp.dot(u, m_sc[...], preferred_element_type=F32)
    y = y + _dot_nt(xs_re, g_re) - _dot_nt(xs_im, g_im)
    y_ref[0] = y


def _s5_core(u2d, prm, bsz, seq):
    L, gc, P = CHUNK, S5_GC, S5_P
    groups = u2d.shape[1] // gc
    nchunk = seq // L
    rows = bsz * nchunk
    n = L * gc
    ug = u2d.reshape(bsz, nchunk, L, groups, gc).transpose(3, 0, 1, 4, 2).reshape(groups, rows, n)
    row3 = lambda a: a.reshape(groups, 1, P).astype(F32)
    col3 = lambda a: a.reshape(groups, P, 1).astype(F32)
    dt_b = jnp.broadcast_to(prm["log_dt"][:, None], (groups, P))
    args = [ug, row3(prm["a_re"]), row3(prm["a_im"]), row3(dt_b),
            col3(prm["a_re"]), col3(prm["a_im"]), col3(dt_b),
            prm["b_re"].transpose(0, 2, 1), prm["b_im"].transpose(0, 2, 1),
            prm["c_re"], prm["c_im"]]
    g3 = lambda g: (g, 0, 0)
    specs = [pl.BlockSpec((1, rows, n), g3)]
    specs += [pl.BlockSpec((1, 1, P), g3)] * 3 + [pl.BlockSpec((1, P, 1), g3)] * 3
    specs += [pl.BlockSpec((1, gc, P), g3)] * 4
    yg = pl.pallas_call(
        functools.partial(_s5_kernel, bsz=bsz, nchunk=nchunk),
        grid=(groups,),
        in_specs=specs,
        out_specs=pl.BlockSpec((1, rows, n), g3),
        out_shape=jax.ShapeDtypeStruct((groups, rows, n), F32),
        scratch_shapes=[pltpu.VMEM((n, n), BF16)],
        compiler_params=_cparams("parallel"),
        name="s5_core",
    )(*args)
    return yg.reshape(groups, bsz, nchunk, gc, L).transpose(1, 2, 4, 0, 3).reshape(bsz * seq, groups * gc)


def _gelu_tanh(x):
    return 0.5 * x * (1.0 + jnp.tanh(0.7978845608028654 * (x + 0.044715 * x * x * x)))


def _out_kernel(ya_ref, yb_ref, yc_ref, u_ref, x_ref, gt1_ref, sh2_ref, sc2_ref, g_ref, d_ref,
                gw_ref, gbias_ref, wo_ref, wrh_ref, wrl_ref, br_ref,
                x1_ref, h2_ref, ri_ref, rw_ref, hist_ref, *, da, db):
    yc = _gelu_tanh(yc_ref[...] + d_ref[...] * u_ref[...])
    yc = yc * _sigmoid(_dot(yc, gw_ref[...]) + gbias_ref[...])
    mixed = (_dot(ya_ref[...], wo_ref[0:da, :]) + _dot(yb_ref[...], wo_ref[da:da + db, :])
             + _dot(yc, wo_ref[da + db:, :]))
    x1 = x_ref[...] + (1.0 + gt1_ref[0, 0]) * mixed
    x1_ref[...] = x1
    h2 = _rmsnorm(x1, g_ref[...]) * (1.0 + sc2_ref[0, 0]) + sh2_ref[0, 0]
    h2_ref[...] = h2

    hh, hl = _split(h2)
    logits = (jnp.dot(hh, wrh_ref[...], preferred_element_type=F32)
              + jnp.dot(hh, wrl_ref[...], preferred_element_type=F32)
              + jnp.dot(hl, wrh_ref[...], preferred_element_type=F32)) + br_ref[...]
    lane_i = lax.broadcasted_iota(jnp.int32, logits.shape, 1)
    lane = lane_i.astype(F32)
    big = float(LANES)
    neg = -jnp.inf
    is_g = lane_i < N_GROUPS
    lg = jnp.where(is_g, logits, neg)
    gmax = jnp.max(lg, axis=1, keepdims=True)
    gi = jnp.min(jnp.where(is_g & (lg == gmax), lane, big), axis=1, keepdims=True)
    gp = 1.0 / jnp.sum(jnp.where(is_g, jnp.exp(lg - gmax), 0.0), axis=1, keepdims=True)
    e_lane = lane_i - N_GROUPS
    grp_of_lane = lax.shift_right_arithmetic(e_lane, 3).astype(F32)
    in_grp = (e_lane >= 0) & (e_lane < N_EXPERTS) & (grp_of_lane == gi)
    l1 = jnp.where(in_grp, logits, neg)
    m1 = jnp.max(l1, axis=1, keepdims=True)
    i1 = jnp.min(jnp.where(in_grp & (l1 == m1), lane, big), axis=1, keepdims=True)
    rest = in_grp & (lane != i1)
    l2 = jnp.where(rest, logits, neg)
    m2 = jnp.max(l2, axis=1, keepdims=True)
    i2 = jnp.min(jnp.where(rest & (l2 == m2), lane, big), axis=1, keepdims=True)
    e2 = jnp.exp(m2 - m1)
    w1 = gp / (1.0 + e2)
    w2 = gp * e2 / (1.0 + e2)
    hot1 = (lane == i1).astype(F32)
    hot2 = (lane == i2).astype(F32)
    both = (hot1 + hot2).astype(BF16)
    tm = logits.shape[0]
    rr = lax.broadcasted_iota(jnp.int32, (tm, tm), 0)
    cc = lax.broadcasted_iota(jnp.int32, (tm, tm), 1)
    before = jnp.dot((cc < rr).astype(BF16), both, preferred_element_type=F32)
    rank1 = jnp.sum(before * hot1, axis=1, keepdims=True)
    rank2 = jnp.sum(before * hot2, axis=1, keepdims=True)
    hist_ref[0] = jnp.sum(hot1 + hot2, axis=0, keepdims=True)
    ids = jnp.where(lane_i == 0, i1, jnp.where(lane_i == 1, i2, jnp.where(lane_i == 2, rank1, rank2)))
    ri_ref[...] = ids.astype(jnp.int32)
    rw_ref[...] = jnp.where(lane_i == 0, w1, jnp.where(lane_i == 1, w2, 0.0))


def _out_proj(ya, yb, yc, u, x2d, mod_l, g_ffn, s5_d, glu_w, glu_b, w_out, w_rg, b_rg, w_re, b_re, seq):
    n_tok, d = x2d.shape
    da, db, dc = ya.shape[1], yb.shape[1], yc.shape[1]
    tm = _row_tile(seq, 512)
    per_b = seq // tm
    wr = jnp.zeros((d, LANES), F32).at[:, :N_GROUPS].set(w_rg).at[:, N_GROUPS:N_GROUPS + N_EXPERTS].set(w_re)
    wr_hi = wr.astype(BF16)
    wr_lo = (wr - wr_hi.astype(F32)).astype(BF16)
    br = jnp.zeros((1, LANES), F32).at[0, :N_GROUPS].set(b_rg).at[0, N_GROUPS:N_GROUPS + N_EXPERTS].set(b_re)
    row = lambda i: (i, 0)
    const = lambda i: (0, 0)
    modspec = lambda j: pl.BlockSpec((1, 1, 1, d), lambda i: (i // per_b, j, 0, 0))
    return pl.pallas_call(
        functools.partial(_out_kernel, da=da, db=db),
        grid=(n_tok // tm,),
        in_specs=[pl.BlockSpec((tm, da), row), pl.BlockSpec((tm, db), row), pl.BlockSpec((tm, dc), row),
                  pl.BlockSpec((tm, dc), row), pl.BlockSpec((tm, d), row),
                  modspec(2), modspec(3), modspec(4),
                  pl.BlockSpec((1, d), const), pl.BlockSpec((1, dc), const),
                  pl.BlockSpec((dc, dc), const), pl.BlockSpec((1, dc), const),
                  pl.BlockSpec((d, d), const), pl.BlockSpec((d, LANES), const),
                  pl.BlockSpec((d, LANES), const), pl.BlockSpec((1, LANES), const)],
        out_specs=[pl.BlockSpec((tm, d), row), pl.BlockSpec((tm, d), row),
                   pl.BlockSpec((tm, LANES), row), pl.BlockSpec((tm, LANES), row),
                   pl.BlockSpec((1, 1, LANES), lambda i: (i, 0, 0))],
        out_shape=[jax.ShapeDtypeStruct((n_tok, d), F32), jax.ShapeDtypeStruct((n_tok, d), F32),
                   jax.ShapeDtypeStruct((n_tok, LANES), jnp.int32),
                   jax.ShapeDtypeStruct((n_tok, LANES), F32),
                   jax.ShapeDtypeStruct((n_tok // tm, 1, LANES), F32)],
        compiler_params=_cparams("parallel"),
        name="out_proj_router",
    )(ya, yb, yc, u, x2d, mod_l, mod_l, mod_l, g_ffn, s5_d.reshape(1, dc), glu_w.astype(BF16),
      glu_b.reshape(1, dc), w_out.astype(BF16), wr_hi, wr_lo, br)


ROW_UNROLL = 8


def _route(ri, hist, n_tok, tm):
    ntile = n_tok // tm
    h = hist.reshape(ntile, LANES)[:, N_GROUPS:N_GROUPS + N_EXPERTS].astype(jnp.int32)
    counts = jnp.sum(h, axis=0)
    pcounts = (counts + MOE_ROWS - 1) // MOE_ROWS * MOE_ROWS
    pends = jnp.cumsum(pcounts)
    base = (pends - pcounts)[None, :] + jnp.cumsum(h, axis=0) - h
    n_rows = 2 * n_tok + N_EXPERTS * MOE_ROWS
    nblk = n_rows // MOE_ROWS
    blk_start = jnp.arange(nblk, dtype=jnp.int32)[:, None] * MOE_ROWS
    blk_e = jnp.minimum(jnp.sum((pends[None, :] <= blk_start).astype(jnp.int32), axis=1), N_EXPERTS - 1)
    eid = (ri[:, 0:2] - N_GROUPS).reshape(ntile, tm, 2, 1)
    hot = eid == jnp.arange(N_EXPERTS, dtype=jnp.int32)
    dest = jnp.sum(jnp.where(hot, base[:, None, None, :], 0), axis=-1) + ri[:, 2:4].reshape(ntile, tm, 2)
    return dest.reshape(n_tok, 2).astype(jnp.int32), blk_e, n_rows


def _tile_rows(dest, tm):
    nblk = dest.shape[0] // tm
    return dest.reshape(nblk, tm, 2).transpose(0, 2, 1).reshape(nblk, 1, 2 * tm)


def _dispatch_kernel(d_ref, h_ref, zero_hbm, xs_hbm, sem, *, tm):
    del zero_hbm

    def body(g, carry):
        for u in range(ROW_UNROLL):
            r = g * ROW_UNROLL + u
            src = h_ref.at[pl.ds(r, 1), :]
            pltpu.make_async_copy(src, xs_hbm.at[pl.ds(d_ref[0, 0, r], 1), :], sem.at[0]).start(priority=0)
            pltpu.make_async_copy(src, xs_hbm.at[pl.ds(d_ref[0, 0, tm + r], 1), :], sem.at[0]).start(priority=1)
        return carry
    lax.fori_loop(0, tm // ROW_UNROLL, body, 0)
    for _ in range(2):
        pltpu.make_async_copy(h_ref, xs_hbm.at[pl.ds(0, tm), :], sem.at[0]).wait()


def _dispatch(h2, dest, n_rows, seq):
    n_tok, d = h2.shape
    tm = _row_tile(seq, 512)
    nblk = n_tok // tm
    return pl.pallas_call(
        functools.partial(_dispatch_kernel, tm=tm),
        grid=(nblk,),
        in_specs=[pl.BlockSpec((1, 1, 2 * tm), lambda i: (i, 0, 0), memory_space=pltpu.SMEM),
                  pl.BlockSpec((tm, d), lambda i: (i, 0)),
                  pl.BlockSpec(memory_space=pl.ANY)],
        out_specs=pl.BlockSpec(memory_space=pl.ANY),
        out_shape=jax.ShapeDtypeStruct((n_rows, d), F32),
        scratch_shapes=[pltpu.SemaphoreType.DMA((1,))],
        input_output_aliases={2: 0},
        compiler_params=_cparams("arbitrary"),
        name="moe_dispatch",
    )(_tile_rows(dest, tm), h2, jnp.zeros((n_rows, d), F32))


def _moe_kernel(blk_e_ref, x_ref, w13_ref, w2_ref, y_ref, *, d_exp):
    del blk_e_ref
    hcat = jnp.dot(x_ref[...].astype(BF16), w13_ref[0], preferred_element_type=F32)
    act = _silu(hcat[:, :d_exp]) * hcat[:, d_exp:]
    y_ref[...] = jnp.dot(act.astype(BF16), w2_ref[0], preferred_element_type=F32)


def _moe_experts(xs, blk_e, w13, w2):
    n_rows, d = xs.shape
    d_exp = w2.shape[1]
    grid_spec = pltpu.PrefetchScalarGridSpec(
        num_scalar_prefetch=1,
        grid=(n_rows // MOE_ROWS,),
        in_specs=[pl.BlockSpec((MOE_ROWS, d), lambda i, e: (i, 0)),
                  pl.BlockSpec((1, d, 2 * d_exp), lambda i, e: (e[i], 0, 0)),
                  pl.BlockSpec((1, d_exp, d), lambda i, e: (e[i], 0, 0))],
        out_specs=pl.BlockSpec((MOE_ROWS, d), lambda i, e: (i, 0)),
    )
    return pl.pallas_call(
        functools.partial(_moe_kernel, d_exp=d_exp),
        grid_spec=grid_spec,
        out_shape=jax.ShapeDtypeStruct((n_rows, d), F32),
        compiler_params=_cparams("arbitrary"),
        name="moe_experts",
    )(blk_e, xs, w13, w2)


def _comb_kernel(d_ref, d_next_ref, x_ref, rw_ref, gt_ref, y_hbm, g_ref, o_ref, ybuf, sem, *, tm, final):
    i = pl.program_id(0)
    n = pl.num_programs(0)
    slot = i % 2

    def start_all(idx_ref, s):
        def body(g, carry):
            for u in range(ROW_UNROLL):
                r = g * ROW_UNROLL + u
                pltpu.make_async_copy(y_hbm.at[pl.ds(idx_ref[0, 0, r], 1), :], ybuf.at[s, pl.ds(r, 1), :],
                                      sem.at[s]).start(priority=u % 2)
            return carry
        lax.fori_loop(0, 2 * tm // ROW_UNROLL, body, 0)

    @pl.when(i == 0)
    def _():
        start_all(d_ref, 0)

    @pl.when(i + 1 < n)
    def _():
        start_all(d_next_ref, 1 - slot)

    pltpu.make_async_copy(y_hbm.at[pl.ds(0, 2 * tm), :], ybuf.at[slot], sem.at[slot]).wait()

    w = rw_ref[...]
    moe = w[:, 0:1] * ybuf[slot, 0:tm, :] + w[:, 1:2] * ybuf[slot, tm:2 * tm, :]
    x2 = x_ref[...] + (1.0 + gt_ref[0, 0]) * moe
    o_ref[...] = _rmsnorm(x2, g_ref[...]) if final else x2


def _combine(x1, y_rows, dest, rw, mod_l, g_final, seq, final):
    n_tok, d = x1.shape
    tm = _row_tile(seq, 512)
    per_b = seq // tm
    nblk = n_tok // tm
    idx = _tile_rows(dest, tm)
    return pl.pallas_call(
        functools.partial(_comb_kernel, tm=tm, final=final),
        grid=(nblk,),
        in_specs=[pl.BlockSpec((1, 1, 2 * tm), lambda i: (i, 0, 0), memory_space=pltpu.SMEM),
                  pl.BlockSpec((1, 1, 2 * tm), lambda i: (jnp.minimum(i + 1, nblk - 1), 0, 0),
                               memory_space=pltpu.SMEM),
                  pl.BlockSpec((tm, d), lambda i: (i, 0)),
                  pl.BlockSpec((tm, LANES), lambda i: (i, 0)),
                  pl.BlockSpec((1, 1, 1, d), lambda i: (i // per_b, 5, 0, 0)),
                  pl.BlockSpec(memory_space=pl.ANY),
                  pl.BlockSpec((1, d), lambda i: (0, 0))],
        out_specs=pl.BlockSpec((tm, d), lambda i: (i, 0)),
        out_shape=jax.ShapeDtypeStruct((n_tok, d), F32),
        scratch_shapes=[pltpu.VMEM((2, 2 * tm, d), F32), pltpu.SemaphoreType.DMA((2,))],
        compiler_params=_cparams("arbitrary"),
        name="moe_combine",
    )(idx, idx, x1, rw, mod_l, y_rows, g_final)


def kernel(x, c, ada_w, ada_b, norm_mix, norm_ffn, norm_final, w_in, w_out, mlstm_conv_w, mlstm_conv_b, mlstm_w_q, mlstm_w_k, mlstm_b_i, mlstm_b_f, mlstm_norm_w, rwkv_mu, rwkv_w0, rwkv_w_up, rwkv_a0, rwkv_a_up, rwkv_g_up, rwkv_k_k, rwkv_k_a, rwkv_r_k, rwkv_ln_w, rwkv_ln_b, rwkv_v0, rwkv_v_dn, rwkv_v_up, s5_a_re, s5_a_im, s5_log_dt, s5_b_re, s5_b_im, s5_c_re, s5_c_im, s5_d, s5_glu_w, s5_glu_b, moe_w_rg, moe_b_rg, moe_w_re, moe_b_re, moe_w1, moe_w3, moe_w2):
    bsz, seq, d = x.shape
    depth = ada_w.shape[0]
    n_tok = bsz * seq
    heads_a = mlstm_w_q.shape[1]
    da = heads_a * HEAD_DIM
    db = rwkv_w0.shape[1]
    dc = s5_d.shape[1]
    rw_cols = rwkv_mu.shape[1]
    assert seq % CHUNK == 0 and w_in.shape[2] == 3 * da + 2 * heads_a + rw_cols + dc
    assert 2 * HEAD_DIM == LANES and CHUNK == HEAD_DIM and EXPERTS_PER_GROUP == 8
    widths = (3 * da, 2 * LANES, rw_cols, dc)

    mod = _modulation(c, ada_w, ada_b).reshape(depth, bsz, 6, 1, d)
    xc = x.reshape(n_tok, d)
    v_first = None
    for l in range(depth):
        mod_l = mod[l]
        w = w_in[l]
        gate_pad = jnp.zeros((d, LANES - heads_a), w.dtype)
        w_pad = jnp.concatenate([w[:, :3 * da], w[:, 3 * da:3 * da + heads_a], gate_pad,
                                 w[:, 3 * da + heads_a:3 * da + 2 * heads_a], gate_pad,
                                 w[:, 3 * da + 2 * heads_a:]], axis=1).astype(BF16)
        qkvo, gates, pcols, u = _in_proj(xc, mod_l, norm_mix[l].reshape(1, d), w_pad, widths, seq)

        ya = _mlstm(qkvo, gates, mlstm_conv_w[l], mlstm_conv_b[l], mlstm_w_q[l], mlstm_w_k[l],
                    mlstm_b_i[l], mlstm_b_f[l], mlstm_norm_w[l], bsz, seq)
        rprm = dict(mu=rwkv_mu[l], w0=rwkv_w0[l], w_up=rwkv_w_up[l], a0=rwkv_a0[l], a_up=rwkv_a_up[l],
                    g_up=rwkv_g_up[l], k_k=rwkv_k_k[l], k_a=rwkv_k_a[l], r_k=rwkv_r_k[l],
                    ln_w=rwkv_ln_w[l], ln_b=rwkv_ln_b[l])
        if l > 0:
            rprm.update(v0=rwkv_v0[l - 1], v_dn=rwkv_v_dn[l - 1], v_up=rwkv_v_up[l - 1])
        yb, v_first = _rwkv(pcols, v_first if l > 0 else None, rprm, bsz, seq)
        sprm = dict(a_re=s5_a_re[l], a_im=s5_a_im[l], log_dt=s5_log_dt[l], b_re=s5_b_re[l],
                    b_im=s5_b_im[l], c_re=s5_c_re[l], c_im=s5_c_im[l])
        yc = _s5_core(u, sprm, bsz, seq)

        x1, h2, ri, rw, hist = _out_proj(ya, yb, yc, u, xc, mod_l, norm_ffn[l].reshape(1, d), s5_d[l],
                                         s5_glu_w[l], s5_glu_b[l], w_out[l], moe_w_rg[l], moe_b_rg[l],
                                         moe_w_re[l], moe_b_re[l], seq)
        dest, blk_e, n_rows = _route(ri, hist, n_tok, n_tok // hist.shape[0])
        w13 = jnp.concatenate([moe_w1[l], moe_w3[l]], axis=2).astype(BF16)
        y_rows = _moe_experts(_dispatch(h2, dest, n_rows, seq), blk_e, w13, moe_w2[l].astype(BF16))
        xc = _combine(x1, y_rows, dest, rw, mod_l, norm_final.reshape(1, d), seq, final=(l == depth - 1))
    return xc.reshape(bsz, seq, d).astype(x.dtype)
```

```python
import functools

import jax
import jax.numpy as jnp
from jax import lax
from jax.experimental import pallas as pl
from jax.experimental.pallas import tpu as pltpu

F32 = jnp.float32
BF16 = jnp.bfloat16

HEAD_DIM = 64
CHUNK = 64
CONV_K = 4
S5_GC = 16
S5_P = 64
N_GROUPS = 4
EXPERTS_PER_GROUP = 8
N_EXPERTS = N_GROUPS * EXPERTS_PER_GROUP
NORM_EPS = 1e-6
HEAD_NORM_EPS = 1e-5
RWKV_GN_EPS = 64e-5
L2_EPS = 1e-12
LANES = 128
SUBLANES = 8
MOE_ROWS = 256
VMEM_LIMIT = 56 * 1024 * 1024


def _cparams(*sem):
    return pltpu.CompilerParams(dimension_semantics=sem, vmem_limit_bytes=VMEM_LIMIT)


def _row_tile(n, want):
    t = min(n, want)
    assert n % t == 0
    return t


def _dot(a, b):
    return jnp.dot(a.astype(BF16), b.astype(BF16), preferred_element_type=F32)


def _dot_nt(a, b):
    return lax.dot_general(a.astype(BF16), b.astype(BF16), (((1,), (1,)), ((), ())),
                           preferred_element_type=F32)


def _dot_tn(a, b):
    return lax.dot_general(a.astype(BF16), b.astype(BF16), (((0,), (0,)), ((), ())),
                           preferred_element_type=F32)


def _split(a):
    hi = a.astype(BF16)
    lo = (a - hi.astype(F32)).astype(BF16)
    return hi, lo


def _dot_xa(a, b_exact):
    hi, lo = _split(a)
    return (jnp.dot(hi, b_exact, preferred_element_type=F32)
            + jnp.dot(lo, b_exact, preferred_element_type=F32))


def _dot_xb(a_exact, b):
    hi, lo = _split(b)
    return (jnp.dot(a_exact, hi, preferred_element_type=F32)
            + jnp.dot(a_exact, lo, preferred_element_type=F32))


def _dot_x3(a, b_exact):
    hi = a.astype(BF16)
    r1 = a - hi.astype(F32)
    mid = r1.astype(BF16)
    lo = (r1 - mid.astype(F32)).astype(BF16)
    return (jnp.dot(hi, b_exact, preferred_element_type=F32)
            + jnp.dot(mid, b_exact, preferred_element_type=F32)
            + jnp.dot(lo, b_exact, preferred_element_type=F32))


def _dot3(a, b):
    ah, al = _split(a)
    bh, bl = _split(b)
    return (jnp.dot(ah, bh, preferred_element_type=F32)
            + jnp.dot(ah, bl, preferred_element_type=F32)
            + jnp.dot(al, bh, preferred_element_type=F32))


def _dot3_nt(a, b):
    ah, al = _split(a)
    bh, bl = _split(b)
    dn = (((1,), (1,)), ((), ()))
    return (lax.dot_general(ah, bh, dn, preferred_element_type=F32)
            + lax.dot_general(ah, bl, dn, preferred_element_type=F32)
            + lax.dot_general(al, bh, dn, preferred_element_type=F32))


def _sigmoid(x):
    return 1.0 / (1.0 + jnp.exp(-x))


def _silu(x):
    return x * _sigmoid(x)


def _log_sigmoid(x):
    return jnp.minimum(x, 0.0) - jnp.log1p(jnp.exp(-jnp.abs(x)))


def _rmsnorm(x, g):
    ms = jnp.mean(x * x, axis=-1, keepdims=True)
    return x * lax.rsqrt(ms + NORM_EPS) * g


def _tri_incl(n):
    r = lax.broadcasted_iota(jnp.int32, (n, n), 0)
    c = lax.broadcasted_iota(jnp.int32, (n, n), 1)
    return (c <= r).astype(BF16)


def _head_ones(width):
    r = lax.broadcasted_iota(jnp.int32, (width, width), 0) // HEAD_DIM
    c = lax.broadcasted_iota(jnp.int32, (width, width), 1) // HEAD_DIM
    return (r == c).astype(BF16)


def _mod_kernel(c_ref, w_ref, b_ref, o_ref):
    o_ref[0] = _dot(_silu(c_ref[...]), w_ref[0]) + b_ref[0]


def _modulation(c, ada_w, ada_b):
    depth, d, d6 = ada_w.shape
    bsz = c.shape[0]
    tn = _row_tile(d6, 1024)
    return pl.pallas_call(
        _mod_kernel,
        grid=(depth, d6 // tn),
        in_specs=[pl.BlockSpec((bsz, d), lambda l, j: (0, 0)),
                  pl.BlockSpec((1, d, tn), lambda l, j: (l, 0, j)),
                  pl.BlockSpec((1, 1, tn), lambda l, j: (l, 0, j))],
        out_specs=pl.BlockSpec((1, bsz, tn), lambda l, j: (l, 0, j)),
        out_shape=jax.ShapeDtypeStruct((depth, bsz, d6), F32),
        compiler_params=_cparams("parallel", "parallel"),
        name="adaln_mod",
    )(c, ada_w, ada_b.reshape(depth, 1, d6))


def _in_kernel(x_ref, sh_ref, sc_ref, g_ref, w_ref, *refs, widths, da, heads):
    out_refs, w_sc = refs[:-1], refs[-1]
    d = w_ref.shape[1]
    rw_cols, dc = widths[2], widths[3]

    @pl.when(pl.program_id(0) == 0)
    def _():
        rb = 256
        g0 = 3 * da
        r_src = g0 + 2 * heads
        u_src = r_src + rw_cols
        r_win = -(-(2 * heads + rw_cols) // LANES) * LANES
        u_al = u_src // LANES * LANES
        lane = lax.broadcasted_iota(jnp.int32, (rb, LANES), 1)
        for r0 in range(0, d, rb):
            rs = slice(r0, r0 + rb)
            w_sc[rs, 0:g0] = w_ref[0, rs, 0:g0].astype(BF16)
            gblk = w_ref[0, rs, g0:g0 + LANES]
            w_sc[rs, g0:g0 + LANES] = jnp.where(lane < heads, gblk, 0.0).astype(BF16)
            w_sc[rs, g0 + LANES:g0 + 2 * LANES] = jnp.where(
                lane < heads, pltpu.roll(gblk, LANES - heads, 1), 0.0).astype(BF16)
            win = w_ref[0, rs, g0:g0 + r_win]
            w_sc[rs, g0 + 2 * LANES:g0 + 2 * LANES + rw_cols] = win[:, 2 * heads:2 * heads + rw_cols].astype(BF16)
            win2 = w_ref[0, rs, u_al:u_src + dc]
            w_sc[rs, g0 + 2 * LANES + rw_cols:] = win2[:, u_src - u_al:u_src - u_al + dc].astype(BF16)

    h = _rmsnorm(x_ref[...], g_ref[...]) * (1.0 + sc_ref[0, 0]) + sh_ref[0, 0]
    hb = h.astype(BF16)
    off = 0
    for o_ref, wd in zip(out_refs, widths):
        o_ref[...] = jnp.dot(hb, w_sc[:, off:off + wd], preferred_element_type=F32)
        off += wd


def _in_proj(x2d, mod_l, g, w_in, layer, widths, da, heads, seq):
    n_tok, d = x2d.shape
    cols = w_in.shape[2]
    tm = _row_tile(seq, 512)
    per_b = seq // tm
    return pl.pallas_call(
        functools.partial(_in_kernel, widths=widths, da=da, heads=heads),
        grid=(n_tok // tm,),
        in_specs=[pl.BlockSpec((tm, d), lambda i: (i, 0)),
                  pl.BlockSpec((1, 1, 1, d), lambda i: (i // per_b, 0, 0, 0)),
                  pl.BlockSpec((1, 1, 1, d), lambda i: (i // per_b, 1, 0, 0)),
                  pl.BlockSpec((1, d), lambda i: (0, 0)),
                  pl.BlockSpec((1, d, cols), lambda i: (layer, 0, 0))],
        out_specs=[pl.BlockSpec((tm, wd), lambda i: (i, 0)) for wd in widths],
        out_shape=[jax.ShapeDtypeStruct((n_tok, wd), F32) for wd in widths],
        scratch_shapes=[pltpu.VMEM((d, sum(widths)), BF16)],
        compiler_params=_cparams("arbitrary"),
        name="in_proj",
    )(x2d, mod_l, mod_l, g, w_in)


def _mlstm_kernel(qkvo_ref, gate_ref, cw_ref, cb_ref, wq_ref, wk_ref, gb_ref, nw_ref, out_ref,
                  xf_sc, q_sc, k_sc, gi_sc, gf_sc, cn_sc, m_sc, *, nb, tb, heads):
    dh, L = HEAD_DIM, CHUNK
    da = heads * dh
    i = pl.program_id(0)

    @pl.when(i == 0)
    def _():
        xf_sc[:, 0:SUBLANES, :] = jnp.zeros((nb, SUBLANES, da), F32)
        cn_sc[...] = jnp.zeros_like(cn_sc)
        m_sc[...] = jnp.zeros_like(m_sc)

    for b in range(nb):
        xqk = qkvo_ref[b, :, 0:da]
        xf_sc[b, pl.ds(SUBLANES, tb), :] = xqk
        acc = xqk * cw_ref[CONV_K - 1:CONV_K, :] + cb_ref[...]
        for j in range(1, CONV_K):
            acc = acc + xf_sc[b, pl.ds(SUBLANES - j, tb), :] * cw_ref[CONV_K - 1 - j:CONV_K - j, :]
        xf_sc[b, 0:SUBLANES, :] = xf_sc[b, pl.ds(tb, SUBLANES), :]
        cx = _silu(acc).astype(BF16)
        q_sc[b] = jnp.dot(cx, wq_ref[...], preferred_element_type=F32)
        k_sc[b] = jnp.dot(cx, wk_ref[...], preferred_element_type=F32) * (dh ** -0.5)

        g = gate_ref[b] + gb_ref[...]
        gi_sc[b] = g[:, :LANES]
        gf_sc[b] = _log_sigmoid(g[:, LANES:])

    tri = _tri_incl(L)
    rr = lax.broadcasted_iota(jnp.int32, (L, L), 0)
    cc = lax.broadcasted_iota(jnp.int32, (L, L), 1)
    causal = cc <= rr
    row_l = lax.broadcasted_iota(jnp.int32, (L, LANES), 0)
    sel = (lax.broadcasted_iota(jnp.int32, (LANES, heads * LANES), 0)
           == lax.broadcasted_iota(jnp.int32, (LANES, heads * LANES), 1) // LANES).astype(BF16)
    ones_v = jnp.ones((L, dh), F32)
    mean_m = jnp.full((dh, dh), 1.0 / dh, BF16)

    units = [(b, h) for b in range(nb) for h in range(heads)]
    ur = range(len(units))

    def hcol(h, base=0):
        return slice(base + h * dh, base + (h + 1) * dh)

    def chunk(c, carry):
        rows = pl.ds(pl.multiple_of(c * L, L), L)
        ex, g_t = [], []
        for b in range(nb):
            bc = _dot_xb(tri, gf_sc[b, rows, :])
            gtot = bc[L - 1:L, :]
            g = gi_sc[b, rows, :] - bc
            gmax = jnp.max(g, axis=0, keepdims=True)
            m_prev = m_sc[b]
            m_loc = gtot + gmax
            m_new = jnp.maximum(gtot + m_prev, m_loc)
            a_old = jnp.exp(gtot + m_prev - m_new)
            a_loc = jnp.exp(m_loc - m_new)
            m_sc[b] = m_new
            pm = g
            d = 1
            while d < L:
                pm = jnp.maximum(pm, jnp.where(row_l >= d, pltpu.roll(pm, d, 0), -jnp.inf))
                d *= 2
            mm = jnp.maximum(m_prev, pm)
            stack = jnp.concatenate([jnp.exp(g - gmax), mm, jnp.exp(m_prev - mm), jnp.exp(-(bc + mm)),
                                     a_old, a_loc, jnp.zeros((SUBLANES - 2, LANES), F32)], axis=0)
            ex.append(_dot_x3(stack, sel))
            g_t.append(g.T)

        def part(k, b, h, width=dh):
            return ex[b][k * L:(k + 1) * L, h * LANES:h * LANES + width]

        qc = [q_sc[b, rows, hcol(h)] for b, h in units]
        kc = [k_sc[b, rows, hcol(h)] for b, h in units]
        vo = [jnp.concatenate([qkvo_ref[b, rows, hcol(h, da)], ones_v], axis=1) for b, h in units]
        oc = [qkvo_ref[b, rows, hcol(h, 2 * da)] for b, h in units]
        cn_prev = [cn_sc[b, h] for b, h in units]
        s_raw = [_dot_nt(qc[u], kc[u]) for u in ur]
        q_cn = [_dot(qc[u], cn_prev[u]) for u in ur]
        cn_loc = [_dot_tn(kc[u] * part(0, b, h), vo[u]) for u, (b, h) in enumerate(units)]
        s_qk = [s_raw[u] * jnp.where(causal, jnp.exp(g_t[b][h:h + 1, :] - part(1, b, h)), 0.0)
                for u, (b, h) in enumerate(units)]
        s_vn = [_dot(s_qk[u], vo[u]) for u in ur]
        nd = [part(2, b, h, 2 * dh) * q_cn[u] + s_vn[u] for u, (b, h) in enumerate(units)]
        hh = [nd[u][:, :dh] / jnp.maximum(jnp.abs(nd[u][:, dh:]), part(3, b, h))
              for u, (b, h) in enumerate(units)]
        mu = [_dot_xa(hh[u], mean_m) for u in ur]
        dlt = [hh[u] - mu[u] for u in ur]
        var = [_dot_xa(dlt[u] * dlt[u], mean_m) for u in ur]
        outs = [dlt[u] * lax.rsqrt(var[u] + HEAD_NORM_EPS) * _sigmoid(oc[u]) for u in ur]
        for b in range(nb):
            mine = [u for u in ur if units[u][0] == b]
            out_ref[b, rows, :] = jnp.concatenate([outs[u] for u in mine], axis=1) * nw_ref[...]
        for u, (b, h) in enumerate(units):
            cn_sc[b, h] = (ex[b][4 * L:4 * L + 1, h * LANES:(h + 1) * LANES] * cn_prev[u]
                           + ex[b][4 * L + 1:4 * L + 2, h * LANES:(h + 1) * LANES] * cn_loc[u])
        return carry

    lax.fori_loop(0, tb // L, chunk, 0)


def _block_diag(w):
    heads, dh, _ = w.shape
    eye = jnp.eye(heads, dtype=w.dtype)
    return (eye[:, None, :, None] * w[:, :, None, :]).reshape(heads * dh, heads * dh)


def _mlstm(qkvo, gates, conv_w, conv_b, w_q, w_k, b_i, b_f, norm_w, bsz, seq):
    heads = w_q.shape[0]
    da = heads * HEAD_DIM
    tb = _row_tile(seq, 256)
    nblk = seq // tb
    gbias = jnp.zeros((1, 2 * LANES), F32).at[0, :heads].set(b_i).at[0, LANES:LANES + heads].set(b_f)
    kern = functools.partial(_mlstm_kernel, nb=bsz, tb=tb, heads=heads)
    row = lambda i: (0, i, 0)
    const = lambda i: (0, 0)
    out = pl.pallas_call(
        kern,
        grid=(nblk,),
        in_specs=[pl.BlockSpec((bsz, tb, 3 * da), row),
                  pl.BlockSpec((bsz, tb, 2 * LANES), row),
                  pl.BlockSpec((CONV_K, da), const),
                  pl.BlockSpec((1, da), const),
                  pl.BlockSpec((da, da), const),
                  pl.BlockSpec((da, da), const),
                  pl.BlockSpec((1, 2 * LANES), const),
                  pl.BlockSpec((1, da), const)],
        out_specs=pl.BlockSpec((bsz, tb, da), row),
        out_shape=jax.ShapeDtypeStruct((bsz, seq, da), F32),
        scratch_shapes=[pltpu.VMEM((bsz, tb + SUBLANES, da), F32),
                        pltpu.VMEM((bsz, tb, da), F32),
                        pltpu.VMEM((bsz, tb, da), F32),
                        pltpu.VMEM((bsz, tb, LANES), F32),
                        pltpu.VMEM((bsz, tb, LANES), F32),
                        pltpu.VMEM((bsz, heads, HEAD_DIM, 2 * HEAD_DIM), F32),
                        pltpu.VMEM((bsz, 1, LANES), F32)],
        compiler_params=_cparams("arbitrary"),
        name="mlstm",
    )(qkvo.reshape(bsz, seq, 3 * da), gates.reshape(bsz, seq, 2 * LANES), conv_w, conv_b.reshape(1, da),
      _block_diag(w_q).astype(BF16), _block_diag(w_k).astype(BF16), gbias, norm_w.reshape(1, da))
    return out.reshape(bsz * seq, da)


def _rwkv_kernel(*refs, nb, tb, heads, lw_dim, la_dim, lg_dim, has_vres):
    dh, L = HEAD_DIM, CHUNK
    db = heads * dh
    it = iter(refs)
    p_ref = next(it)
    vf_ref = next(it) if has_vres else None
    (mu_ref, w0_ref, wup_ref, a0_ref, aup_ref, gup_ref, kk_ref, ka_ref, rk_ref,
     lnw_ref, lnb_ref) = (next(it) for _ in range(11))
    if has_vres:
        v0_ref, vdn_ref, vup_ref = (next(it) for _ in range(3))
    y_ref = next(it)
    vout_ref = None if has_vres else next(it)
    (xf_sc, r_sc, k_sc, v_sc, a_sc, b_sc, lw_sc, y_sc, gate_sc, rp_sc, st_sc, q_sc, z_sc,
     gl_sc) = (next(it) for _ in range(14))

    i = pl.program_id(0)

    @pl.when(i == 0)
    def _():
        xf_sc[:, 0:SUBLANES, :] = jnp.zeros((nb, SUBLANES, xf_sc.shape[2]), F32)
        st_sc[...] = jnp.zeros_like(st_sc)

    ones_h = _head_ones(db)
    for b in range(nb):
        p = p_ref[b]
        xf_sc[b, pl.ds(SUBLANES, tb), :] = p
        prev = xf_sc[b, pl.ds(SUBLANES - 1, tb), :]
        xf_sc[b, 0:SUBLANES, :] = xf_sc[b, pl.ds(tb, SUBLANES), :]
        p = p + mu_ref[...] * (prev - p)

        o = 0
        r = p[:, o:o + db]; o += db
        k = p[:, o:o + db]; o += db
        v = p[:, o:o + db]; o += db
        wd = p[:, o:o + lw_dim]; o += lw_dim
        ad = p[:, o:o + la_dim]; o += la_dim
        gd = p[:, o:o + lg_dim]

        wlog = _log_sigmoid(w0_ref[...] + _dot(jnp.tanh(wd), wup_ref[...])) - 0.5
        lw_sc[b] = -jnp.exp(wlog)
        a = _sigmoid(a0_ref[...] + _dot(ad, aup_ref[...]))
        gate_sc[b] = _dot(_sigmoid(gd), gup_ref[...])
        if has_vres:
            v = v + (vf_ref[b] - v) * _sigmoid(v0_ref[...] + _dot(_dot(v, vdn_ref[...]), vup_ref[...]))
        else:
            vout_ref[b] = v
        kk = k * kk_ref[...]
        kk = kk / jnp.maximum(jnp.sqrt(_dot_xa(kk * kk, ones_h)), L2_EPS)
        r_sc[b] = r
        k_sc[b] = k * (1.0 + (a - 1.0) * ka_ref[...])
        v_sc[b] = v
        a_sc[b] = -kk
        b_sc[b] = kk * a

    tri = _tri_incl(L)
    rr = lax.broadcasted_iota(jnp.int32, (L, L), 0)
    cc = lax.broadcasted_iota(jnp.int32, (L, L), 1)
    strict = cc < rr
    incl = cc <= rr

    units = [(b, h) for b in range(nb) for h in range(heads)]
    ur = range(len(units))

    def hcol(h):
        return slice(h * dh, (h + 1) * dh)

    def chunk(c, carry):
        rows = pl.ds(pl.multiple_of(c * L, L), L)
        at, rt, bt, kt, bg, kg, vch = ([] for _ in range(7))
        for b in range(nb):
            lwc = lw_sc[b, rows, :]
            cum = _dot_xb(tri, lwc)
            cum_l = cum[L - 1:L, :]
            e_inv = jnp.exp(-cum)
            e_end = jnp.exp(cum_l - cum)
            bv = b_sc[b, rows, :]
            kv = k_sc[b, rows, :]
            at.append(a_sc[b, rows, :] * jnp.exp(cum - lwc))
            rt.append(r_sc[b, rows, :] * jnp.exp(cum))
            bt.append(bv * e_inv)
            kt.append(kv * e_inv)
            bg.append(bv * e_end)
            kg.append(kv * e_end)
            vch.append(v_sc[b, rows, :])
            gl_sc[c, b] = jnp.exp(cum_l)
        vh = [vch[b][:, hcol(h)] for b, h in units]
        g4 = [_dot_nt(jnp.concatenate([at[b][:, hcol(h)], rt[b][:, hcol(h)]], axis=0),
                      jnp.concatenate([bt[b][:, hcol(h)], kt[b][:, hcol(h)]], axis=0))
              for b, h in units]
        pw = [jnp.where(strict, g4[u][:L, :L], 0.0) for u in ur]
        n_ak = [jnp.where(strict, g4[u][:L, L:], 0.0) for u in ur]
        m_rb = [jnp.where(incl, g4[u][L:, :L], 0.0) for u in ur]
        m_rk = [jnp.where(incl, g4[u][L:, L:], 0.0) for u in ur]
        x = [jnp.concatenate([at[b][:, hcol(h)], _dot(n_ak[u], vh[u])], axis=1)
             for u, (b, h) in enumerate(units)]
        for step in range(6):
            x = [x[u] + _dot(pw[u], x[u]) for u in ur]
            if step < 5:
                pw = [_dot(pw[u], pw[u]) for u in ur]
        ry = [jnp.concatenate([rt[b][:, hcol(h)], _dot(m_rk[u], vh[u])], axis=1) + _dot(m_rb[u], x[u])
              for u, (b, h) in enumerate(units)]
        qz = [_dot_tn(x[u], bg[b][:, hcol(h)]) for u, (b, h) in enumerate(units)]
        z2 = [_dot_tn(vh[u], kg[b][:, hcol(h)]) for u, (b, h) in enumerate(units)]
        for b in range(nb):
            mine = [u for u in ur if units[u][0] == b]
            rp_sc[b, rows, :] = jnp.concatenate([ry[u][:, :dh] for u in mine], axis=1)
            y_sc[b, rows, :] = jnp.concatenate([ry[u][:, dh:] for u in mine], axis=1)
        for u, (b, h) in enumerate(units):
            q_sc[c, b, h] = qz[u][:dh]
            z_sc[c, b, h] = qz[u][dh:] + z2[u]
        return carry

    lax.fori_loop(0, tb // L, chunk, 0)

    def carry_state(c, carry):
        rows = pl.ds(pl.multiple_of(c * L, L), L)
        st = [st_sc[b, h] for b, h in units]
        ys = [_dot_nt(rp_sc[b, rows, hcol(h)], st[u]) for u, (b, h) in enumerate(units)]
        sq = [_dot_xa(st[u], q_sc[c, b, h].astype(BF16)) for u, (b, h) in enumerate(units)]
        for b in range(nb):
            mine = [u for u in ur if units[u][0] == b]
            y_sc[b, rows, :] = y_sc[b, rows, :] + jnp.concatenate([ys[u] for u in mine], axis=1)
        for u, (b, h) in enumerate(units):
            st_sc[b, h] = st[u] * gl_sc[c, b][:, hcol(h)] + sq[u] + z_sc[c, b, h]
        return carry

    lax.fori_loop(0, tb // L, carry_state, 0)

    for b in range(nb):
        y = y_sc[b]
        mean = _dot_xa(y, ones_h) * (1.0 / dh)
        dlt = y - mean
        var = _dot_xa(dlt * dlt, ones_h) * (1.0 / dh)
        yn = dlt * lax.rsqrt(var + RWKV_GN_EPS) * lnw_ref[...] + lnb_ref[...]
        bonus = _dot_xa(r_sc[b] * k_sc[b] * rk_ref[...], ones_h) * v_sc[b]
        y_ref[b] = (yn + bonus) * gate_sc[b]


def _rwkv(pcols, v_first, prm, bsz, seq):
    db = prm["w0"].shape[0]
    heads = db // HEAD_DIM
    cols = pcols.shape[1]
    lw_dim, la_dim, lg_dim = prm["w_up"].shape[0], prm["a_up"].shape[0], prm["g_up"].shape[0]
    has_vres = v_first is not None
    tb = _row_tile(seq, 256)
    nblk = seq // tb
    row = lambda i: (0, i, 0)
    const = lambda i: (0, 0)
    vec = lambda a: a.reshape(1, -1).astype(F32)

    args = [pcols.reshape(bsz, seq, cols)]
    specs = [pl.BlockSpec((bsz, tb, cols), row)]
    if has_vres:
        args.append(v_first.reshape(bsz, seq, db))
        specs.append(pl.BlockSpec((bsz, tb, db), row))
    small = [vec(prm["mu"]), vec(prm["w0"]), prm["w_up"].astype(BF16), vec(prm["a0"]),
             prm["a_up"].astype(BF16), prm["g_up"].astype(BF16), vec(prm["k_k"]), vec(prm["k_a"]),
             vec(prm["r_k"]), vec(prm["ln_w"]), vec(prm["ln_b"])]
    if has_vres:
        lv = prm["v_dn"].shape[1]
        lvp = -(-lv // LANES) * LANES
        v_dn = jnp.zeros((db, lvp), F32).at[:, :lv].set(prm["v_dn"]).astype(BF16)
        v_up = jnp.zeros((lvp, db), F32).at[:lv, :].set(prm["v_up"]).astype(BF16)
        small += [vec(prm["v0"]), v_dn, v_up]
    args += small
    specs += [pl.BlockSpec(a.shape, const) for a in small]

    out_shape = [jax.ShapeDtypeStruct((bsz, seq, db), F32)]
    out_specs = [pl.BlockSpec((bsz, tb, db), row)]
    if not has_vres:
        out_shape.append(jax.ShapeDtypeStruct((bsz, seq, db), F32))
        out_specs.append(pl.BlockSpec((bsz, tb, db), row))

    kern = functools.partial(_rwkv_kernel, nb=bsz, tb=tb, heads=heads, lw_dim=lw_dim, la_dim=la_dim,
                             lg_dim=lg_dim, has_vres=has_vres)
    nch = tb // CHUNK
    res = pl.pallas_call(
        kern,
        grid=(nblk,),
        in_specs=specs,
        out_specs=out_specs,
        out_shape=out_shape,
        scratch_shapes=[pltpu.VMEM((bsz, tb + SUBLANES, cols), F32)]
        + [pltpu.VMEM((bsz, tb, db), F32) for _ in range(9)]
        + [pltpu.VMEM((bsz, heads, HEAD_DIM, HEAD_DIM), F32),
           pltpu.VMEM((nch, bsz, heads, HEAD_DIM, HEAD_DIM), F32),
           pltpu.VMEM((nch, bsz, heads, HEAD_DIM, HEAD_DIM), F32),
           pltpu.VMEM((nch, bsz, 1, db), F32)],
        compiler_params=_cparams("arbitrary"),
        name="rwkv7",
    )(*args)
    res = [a.reshape(bsz * seq, db) for a in res]
    return (res[0], v_first) if has_vres else (res[0], res[1])


def _cmul(ar, ai, br, bi):
    return ar * br - ai * bi, ar * bi + ai * br


def _shift_rows(x, d):
    row = lax.broadcasted_iota(jnp.int32, x.shape, 0)
    return jnp.where(row >= d, pltpu.roll(x, d, 0), 0.0)


def _s5_kernel(u_ref, are_r, aim_r, dt_r, are_c, aim_c, dt_c, bre_ref, bim_ref, cre_ref, cim_ref,
               y_ref, m_sc, *, bsz, nchunk):
    L, gc, P = CHUNK, S5_GC, S5_P
    n = L * gc
    a_re, a_im, dt = are_r[0], aim_r[0], jnp.exp(dt_r[0])
    mag, ang = jnp.exp(a_re * dt), a_im * dt
    ab_re, ab_im = mag * jnp.cos(ang), mag * jnp.sin(ang)
    inv = 1.0 / (a_re * a_re + a_im * a_im)
    co_re = ((ab_re - 1.0) * a_re + ab_im * a_im) * inv
    co_im = (ab_im * a_re - (ab_re - 1.0) * a_im) * inv
    bb_re, bb_im = _cmul(co_re, co_im, bre_ref[0], bim_ref[0])
    c_re, c_im = cre_ref[0], cim_ref[0]

    def powers(tau):
        m = jnp.exp(tau * (a_re * dt))
        return m * jnp.cos(tau * ang), m * jnp.sin(tau * ang)

    a_re_c, a_im_c, dt_c_ = are_c[0], aim_c[0], jnp.exp(dt_c[0])
    tau_row = lax.broadcasted_iota(jnp.int32, (P, L), 1).astype(F32)
    pm = jnp.exp(tau_row * (a_re_c * dt_c_))
    pt_re = pm * jnp.cos(tau_row * (a_im_c * dt_c_))
    pt_im = pm * jnp.sin(tau_row * (a_im_c * dt_c_))
    pair = lax.broadcasted_iota(jnp.int32, (gc * gc, gc), 0)
    col = lax.broadcasted_iota(jnp.int32, (gc * gc, gc), 1)
    rep_c = (pair // gc == col).astype(BF16)
    rep_b = (pair % gc == col).astype(BF16)
    cb_re, cb_im = _cmul(_dot_xb(rep_c, c_re), _dot_xb(rep_c, c_im),
                         _dot_xb(rep_b, bb_re), _dot_xb(rep_b, bb_im))
    kap = _dot3(cb_re, pt_re) - _dot3(cb_im, pt_im)

    kap_pad = jnp.concatenate([kap, jnp.zeros_like(kap)], axis=1)
    srow = lax.broadcasted_iota(jnp.int32, (L, LANES), 0)
    tcol = lax.broadcasted_iota(jnp.int32, (L, LANES), 1)
    for cp in range(gc):
        for c2 in range(0, gc, 2):
            k0 = jnp.broadcast_to(kap_pad[c2 * gc + cp:c2 * gc + cp + 1, :], (L, LANES))
            k1 = jnp.broadcast_to(kap_pad[(c2 + 1) * gc + cp:(c2 + 1) * gc + cp + 1, :], (L, LANES))
            t0 = pltpu.roll(k0, 0, 1, stride=1, stride_axis=0)
            t1 = pltpu.roll(k1, L, 1, stride=1, stride_axis=0)
            blk = jnp.where(tcol < L, jnp.where(tcol >= srow, t0, 0.0),
                            jnp.where(tcol - L >= srow, t1, 0.0))
            m_sc[cp * L:(cp + 1) * L, c2 * L:(c2 + 2) * L] = blk.astype(BF16)

    s_col = lax.broadcasted_iota(jnp.int32, (L, P), 0).astype(F32)
    pw_re, pw_im = powers((L - 1.0) - s_col)
    pg_re, pg_im = powers(s_col + 1.0)
    w_re, w_im, g_re, g_im = [], [], [], []
    for c in range(gc):
        br = jnp.broadcast_to(bb_re[c:c + 1, :], (L, P))
        bi = jnp.broadcast_to(bb_im[c:c + 1, :], (L, P))
        wr, wi = _cmul(br, bi, pw_re, pw_im)
        w_re.append(wr); w_im.append(wi)
        cr = jnp.broadcast_to(c_re[c:c + 1, :], (L, P))
        ci = jnp.broadcast_to(c_im[c:c + 1, :], (L, P))
        gr, gi = _cmul(cr, ci, pg_re, pg_im)
        g_re.append(gr); g_im.append(gi)
    w_re, w_im = jnp.concatenate(w_re, axis=0), jnp.concatenate(w_im, axis=0)
    g_re, g_im = jnp.concatenate(g_re, axis=0), jnp.concatenate(g_im, axis=0)

    u = u_ref[0].astype(BF16)
    x_re = jnp.dot(u, w_re.astype(BF16), preferred_element_type=F32)
    x_im = jnp.dot(u, w_im.astype(BF16), preferred_element_type=F32)
    xs_re, xs_im = [], []
    for b in range(bsz):
        xr = x_re[b * nchunk:(b + 1) * nchunk]
        xi = x_im[b * nchunk:(b + 1) * nchunk]
        d = 1
        while d < nchunk:
            ar_, ai_ = powers(float(L * d))
            sr, si = _cmul(ar_, ai_, _shift_rows(xr, d), _shift_rows(xi, d))
            xr, xi = xr + sr, xi + si
            d *= 2
        xs_re.append(_shift_rows(xr, 1))
        xs_im.append(_shift_rows(xi, 1))
    xs_re, xs_im = jnp.concatenate(xs_re, axis=0), jnp.concatenate(xs_im, axis=0)
    y = jnp.dot(u, m_sc[...], preferred_element_type=F32)
    y = y + _dot_nt(xs_re, g_re) - _dot_nt(xs_im, g_im)
    y_ref[0] = y


def _s5_core(u2d, prm, bsz, seq):
    L, gc, P = CHUNK, S5_GC, S5_P
    groups = u2d.shape[1] // gc
    nchunk = seq // L
    rows = bsz * nchunk
    n = L * gc
    ug = u2d.reshape(bsz, nchunk, L, groups, gc).transpose(3, 0, 1, 4, 2).reshape(groups, rows, n)
    row3 = lambda a: a.reshape(groups, 1, P).astype(F32)
    col3 = lambda a: a.reshape(groups, P, 1).astype(F32)
    dt_b = jnp.broadcast_to(prm["log_dt"][:, None], (groups, P))
    args = [ug, row3(prm["a_re"]), row3(prm["a_im"]), row3(dt_b),
            col3(prm["a_re"]), col3(prm["a_im"]), col3(dt_b),
            prm["b_re"].transpose(0, 2, 1), prm["b_im"].transpose(0, 2, 1),
            prm["c_re"], prm["c_im"]]
    g3 = lambda g: (g, 0, 0)
    specs = [pl.BlockSpec((1, rows, n), g3)]
    specs += [pl.BlockSpec((1, 1, P), g3)] * 3 + [pl.BlockSpec((1, P, 1), g3)] * 3
    specs += [pl.BlockSpec((1, gc, P), g3)] * 4
    yg = pl.pallas_call(
        functools.partial(_s5_kernel, bsz=bsz, nchunk=nchunk),
        grid=(groups,),
        in_specs=specs,
        out_specs=pl.BlockSpec((1, rows, n), g3),
        out_shape=jax.ShapeDtypeStruct((groups, rows, n), F32),
        scratch_shapes=[pltpu.VMEM((n, n), BF16)],
        compiler_params=_cparams("parallel"),
        name="s5_core",
    )(*args)
    return yg.reshape(groups, bsz, nchunk, gc, L).transpose(1, 2, 4, 0, 3).reshape(bsz * seq, groups * gc)


def _gelu_tanh(x):
    return 0.5 * x * (1.0 + jnp.tanh(0.7978845608028654 * (x + 0.044715 * x * x * x)))


def _out_kernel(ya_ref, yb_ref, yc_ref, u_ref, x_ref, gt1_ref, sh2_ref, sc2_ref, g_ref, d_ref,
                gw_ref, gbias_ref, wo_ref, wrh_ref, wrl_ref, br_ref,
                x1_ref, h2_ref, ri_ref, rw_ref, hist_ref, *, da, db):
    yc = _gelu_tanh(yc_ref[...] + d_ref[...] * u_ref[...])
    yc = yc * _sigmoid(_dot(yc, gw_ref[...]) + gbias_ref[...])
    mixed = (_dot(ya_ref[...], wo_ref[0:da, :]) + _dot(yb_ref[...], wo_ref[da:da + db, :])
             + _dot(yc, wo_ref[da + db:, :]))
    x1 = x_ref[...] + (1.0 + gt1_ref[0, 0]) * mixed
    x1_ref[...] = x1
    h2 = _rmsnorm(x1, g_ref[...]) * (1.0 + sc2_ref[0, 0]) + sh2_ref[0, 0]
    h2_ref[...] = h2

    hh, hl = _split(h2)
    logits = (jnp.dot(hh, wrh_ref[...], preferred_element_type=F32)
              + jnp.dot(hh, wrl_ref[...], preferred_element_type=F32)
              + jnp.dot(hl, wrh_ref[...], preferred_element_type=F32)) + br_ref[...]
    lane_i = lax.broadcasted_iota(jnp.int32, logits.shape, 1)
    lane = lane_i.astype(F32)
    big = float(LANES)
    neg = -jnp.inf
    is_g = lane_i < N_GROUPS
    lg = jnp.where(is_g, logits, neg)
    gmax = jnp.max(lg, axis=1, keepdims=True)
    gi = jnp.min(jnp.where(is_g & (lg == gmax), lane, big), axis=1, keepdims=True)
    gp = 1.0 / jnp.sum(jnp.where(is_g, jnp.exp(lg - gmax), 0.0), axis=1, keepdims=True)
    e_lane = lane_i - N_GROUPS
    grp_of_lane = lax.shift_right_arithmetic(e_lane, 3).astype(F32)
    in_grp = (e_lane >= 0) & (e_lane < N_EXPERTS) & (grp_of_lane == gi)
    l1 = jnp.where(in_grp, logits, neg)
    m1 = jnp.max(l1, axis=1, keepdims=True)
    i1 = jnp.min(jnp.where(in_grp & (l1 == m1), lane, big), axis=1, keepdims=True)
    rest = in_grp & (lane != i1)
    l2 = jnp.where(rest, logits, neg)
    m2 = jnp.max(l2, axis=1, keepdims=True)
    i2 = jnp.min(jnp.where(rest & (l2 == m2), lane, big), axis=1, keepdims=True)
    e2 = jnp.exp(m2 - m1)
    w1 = gp / (1.0 + e2)
    w2 = gp * e2 / (1.0 + e2)
    hot1 = (lane == i1).astype(F32)
    hot2 = (lane == i2).astype(F32)
    both = (hot1 + hot2).astype(BF16)
    tm = logits.shape[0]
    rr = lax.broadcasted_iota(jnp.int32, (tm, tm), 0)
    cc = lax.broadcasted_iota(jnp.int32, (tm, tm), 1)
    before = jnp.dot((cc < rr).astype(BF16), both, preferred_element_type=F32)
    rank1 = jnp.sum(before * hot1, axis=1, keepdims=True)
    rank2 = jnp.sum(before * hot2, axis=1, keepdims=True)
    hist_ref[0] = jnp.sum(hot1 + hot2, axis=0, keepdims=True)
    ids = jnp.where(lane_i == 0, i1, jnp.where(lane_i == 1, i2, jnp.where(lane_i == 2, rank1, rank2)))
    ri_ref[...] = ids.astype(jnp.int32)
    rw_ref[...] = jnp.where(lane_i == 0, w1, jnp.where(lane_i == 1, w2, 0.0))


def _out_proj(ya, yb, yc, u, x2d, mod_l, g_ffn, s5_d, glu_w, glu_b, w_out, w_rg, b_rg, w_re, b_re, seq):
    n_tok, d = x2d.shape
    da, db, dc = ya.shape[1], yb.shape[1], yc.shape[1]
    tm = _row_tile(seq, 512)
    per_b = seq // tm
    wr = jnp.zeros((d, LANES), F32).at[:, :N_GROUPS].set(w_rg).at[:, N_GROUPS:N_GROUPS + N_EXPERTS].set(w_re)
    wr_hi = wr.astype(BF16)
    wr_lo = (wr - wr_hi.astype(F32)).astype(BF16)
    br = jnp.zeros((1, LANES), F32).at[0, :N_GROUPS].set(b_rg).at[0, N_GROUPS:N_GROUPS + N_EXPERTS].set(b_re)
    row = lambda i: (i, 0)
    const = lambda i: (0, 0)
    modspec = lambda j: pl.BlockSpec((1, 1, 1, d), lambda i: (i // per_b, j, 0, 0))
    return pl.pallas_call(
        functools.partial(_out_kernel, da=da, db=db),
        grid=(n_tok // tm,),
        in_specs=[pl.BlockSpec((tm, da), row), pl.BlockSpec((tm, db), row), pl.BlockSpec((tm, dc), row),
                  pl.BlockSpec((tm, dc), row), pl.BlockSpec((tm, d), row),
                  modspec(2), modspec(3), modspec(4),
                  pl.BlockSpec((1, d), const), pl.BlockSpec((1, dc), const),
                  pl.BlockSpec((dc, dc), const), pl.BlockSpec((1, dc), const),
                  pl.BlockSpec((d, d), const), pl.BlockSpec((d, LANES), const),
                  pl.BlockSpec((d, LANES), const), pl.BlockSpec((1, LANES), const)],
        out_specs=[pl.BlockSpec((tm, d), row), pl.BlockSpec((tm, d), row),
                   pl.BlockSpec((tm, LANES), row), pl.BlockSpec((tm, LANES), row),
                   pl.BlockSpec((1, 1, LANES), lambda i: (i, 0, 0))],
        out_shape=[jax.ShapeDtypeStruct((n_tok, d), F32), jax.ShapeDtypeStruct((n_tok, d), F32),
                   jax.ShapeDtypeStruct((n_tok, LANES), jnp.int32),
                   jax.ShapeDtypeStruct((n_tok, LANES), F32),
                   jax.ShapeDtypeStruct((n_tok // tm, 1, LANES), F32)],
        compiler_params=_cparams("parallel"),
        name="out_proj_router",
    )(ya, yb, yc, u, x2d, mod_l, mod_l, mod_l, g_ffn, s5_d.reshape(1, dc), glu_w.astype(BF16),
      glu_b.reshape(1, dc), w_out.astype(BF16), wr_hi, wr_lo, br)


ROW_UNROLL = 8


def _route(ri, hist, n_tok, tm):
    ntile = n_tok // tm
    h = hist.reshape(ntile, LANES)[:, N_GROUPS:N_GROUPS + N_EXPERTS].astype(jnp.int32)
    counts = jnp.sum(h, axis=0)
    pcounts = (counts + MOE_ROWS - 1) // MOE_ROWS * MOE_ROWS
    pends = jnp.cumsum(pcounts)
    base = (pends - pcounts)[None, :] + jnp.cumsum(h, axis=0) - h
    n_rows = 2 * n_tok + N_EXPERTS * MOE_ROWS
    nblk = n_rows // MOE_ROWS
    blk_start = jnp.arange(nblk, dtype=jnp.int32)[:, None] * MOE_ROWS
    blk_e = jnp.minimum(jnp.sum((pends[None, :] <= blk_start).astype(jnp.int32), axis=1), N_EXPERTS - 1)
    eid = (ri[:, 0:2] - N_GROUPS).reshape(ntile, tm, 2, 1)
    hot = eid == jnp.arange(N_EXPERTS, dtype=jnp.int32)
    dest = jnp.sum(jnp.where(hot, base[:, None, None, :], 0), axis=-1) + ri[:, 2:4].reshape(ntile, tm, 2)
    return dest.reshape(n_tok, 2).astype(jnp.int32), blk_e, n_rows


def _tile_rows(dest, tm):
    nblk = dest.shape[0] // tm
    return dest.reshape(nblk, tm, 2).transpose(0, 2, 1).reshape(nblk, 1, 2 * tm)


def _dispatch_kernel(d_ref, h_ref, zero_hbm, xs_hbm, sem, *, tm):
    del zero_hbm

    def body(g, carry):
        for u in range(ROW_UNROLL):
            r = g * ROW_UNROLL + u
            src = h_ref.at[pl.ds(r, 1), :]
            pltpu.make_async_copy(src, xs_hbm.at[pl.ds(d_ref[0, 0, r], 1), :], sem.at[0]).start(priority=0)
            pltpu.make_async_copy(src, xs_hbm.at[pl.ds(d_ref[0, 0, tm + r], 1), :], sem.at[0]).start(priority=1)
        return carry
    lax.fori_loop(0, tm // ROW_UNROLL, body, 0)
    for _ in range(2):
        pltpu.make_async_copy(h_ref, xs_hbm.at[pl.ds(0, tm), :], sem.at[0]).wait()


def _dispatch(h2, dest, n_rows, seq):
    n_tok, d = h2.shape
    tm = _row_tile(seq, 512)
    nblk = n_tok // tm
    return pl.pallas_call(
        functools.partial(_dispatch_kernel, tm=tm),
        grid=(nblk,),
        in_specs=[pl.BlockSpec((1, 1, 2 * tm), lambda i: (i, 0, 0), memory_space=pltpu.SMEM),
                  pl.BlockSpec((tm, d), lambda i: (i, 0)),
                  pl.BlockSpec(memory_space=pl.ANY)],
        out_specs=pl.BlockSpec(memory_space=pl.ANY),
        out_shape=jax.ShapeDtypeStruct((n_rows, d), F32),
        scratch_shapes=[pltpu.SemaphoreType.DMA((1,))],
        input_output_aliases={2: 0},
        compiler_params=_cparams("arbitrary"),
        name="moe_dispatch",
    )(_tile_rows(dest, tm), h2, jnp.zeros((n_rows, d), F32))


def _moe_kernel(blk_e_ref, x_ref, w13_ref, w2_ref, y_ref, *, d_exp):
    del blk_e_ref
    hcat = jnp.dot(x_ref[...].astype(BF16), w13_ref[0], preferred_element_type=F32)
    act = _silu(hcat[:, :d_exp]) * hcat[:, d_exp:]
    y_ref[...] = jnp.dot(act.astype(BF16), w2_ref[0], preferred_element_type=F32)


def _moe_experts(xs, blk_e, w13, w2):
    n_rows, d = xs.shape
    d_exp = w2.shape[1]
    grid_spec = pltpu.PrefetchScalarGridSpec(
        num_scalar_prefetch=1,
        grid=(n_rows // MOE_ROWS,),
        in_specs=[pl.BlockSpec((MOE_ROWS, d), lambda i, e: (i, 0)),
                  pl.BlockSpec((1, d, 2 * d_exp), lambda i, e: (e[i], 0, 0)),
                  pl.BlockSpec((1, d_exp, d), lambda i, e: (e[i], 0, 0))],
        out_specs=pl.BlockSpec((MOE_ROWS, d), lambda i, e: (i, 0)),
    )
    return pl.pallas_call(
        functools.partial(_moe_kernel, d_exp=d_exp),
        grid_spec=grid_spec,
        out_shape=jax.ShapeDtypeStruct((n_rows, d), F32),
        compiler_params=_cparams("arbitrary"),
        name="moe_experts",
    )(blk_e, xs, w13, w2)


def _comb_kernel(d_ref, d_next_ref, x_ref, rw_ref, gt_ref, y_hbm, g_ref, o_ref, ybuf, sem, *, tm, final):
    i = pl.program_id(0)
    n = pl.num_programs(0)
    slot = i % 2

    def start_all(idx_ref, s):
        def body(g, carry):
            for u in range(ROW_UNROLL):
                r = g * ROW_UNROLL + u
                pltpu.make_async_copy(y_hbm.at[pl.ds(idx_ref[0, 0, r], 1), :], ybuf.at[s, pl.ds(r, 1), :],
                                      sem.at[s]).start(priority=u % 2)
            return carry
        lax.fori_loop(0, 2 * tm // ROW_UNROLL, body, 0)

    @pl.when(i == 0)
    def _():
        start_all(d_ref, 0)

    @pl.when(i + 1 < n)
    def _():
        start_all(d_next_ref, 1 - slot)

    pltpu.make_async_copy(y_hbm.at[pl.ds(0, 2 * tm), :], ybuf.at[slot], sem.at[slot]).wait()

    w = rw_ref[...]
    moe = w[:, 0:1] * ybuf[slot, 0:tm, :] + w[:, 1:2] * ybuf[slot, tm:2 * tm, :]
    x2 = x_ref[...] + (1.0 + gt_ref[0, 0]) * moe
    o_ref[...] = _rmsnorm(x2, g_ref[...]) if final else x2


def _combine(x1, y_rows, dest, rw, mod_l, g_final, seq, final):
    n_tok, d = x1.shape
    tm = _row_tile(seq, 512)
    per_b = seq // tm
    nblk = n_tok // tm
    idx = _tile_rows(dest, tm)
    return pl.pallas_call(
        functools.partial(_comb_kernel, tm=tm, final=final),
        grid=(nblk,),
        in_specs=[pl.BlockSpec((1, 1, 2 * tm), lambda i: (i, 0, 0), memory_space=pltpu.SMEM),
                  pl.BlockSpec((1, 1, 2 * tm), lambda i: (jnp.minimum(i + 1, nblk - 1), 0, 0),
                               memory_space=pltpu.SMEM),
                  pl.BlockSpec((tm, d), lambda i: (i, 0)),
                  pl.BlockSpec((tm, LANES), lambda i: (i, 0)),
                  pl.BlockSpec((1, 1, 1, d), lambda i: (i // per_b, 5, 0, 0)),
                  pl.BlockSpec(memory_space=pl.ANY),
                  pl.BlockSpec((1, d), lambda i: (0, 0))],
        out_specs=pl.BlockSpec((tm, d), lambda i: (i, 0)),
        out_shape=jax.ShapeDtypeStruct((n_tok, d), F32),
        scratch_shapes=[pltpu.VMEM((2, 2 * tm, d), F32), pltpu.SemaphoreType.DMA((2,))],
        compiler_params=_cparams("arbitrary"),
        name="moe_combine",
    )(idx, idx, x1, rw, mod_l, y_rows, g_final)


def kernel(x, c, ada_w, ada_b, norm_mix, norm_ffn, norm_final, w_in, w_out, mlstm_conv_w, mlstm_conv_b, mlstm_w_q, mlstm_w_k, mlstm_b_i, mlstm_b_f, mlstm_norm_w, rwkv_mu, rwkv_w0, rwkv_w_up, rwkv_a0, rwkv_a_up, rwkv_g_up, rwkv_k_k, rwkv_k_a, rwkv_r_k, rwkv_ln_w, rwkv_ln_b, rwkv_v0, rwkv_v_dn, rwkv_v_up, s5_a_re, s5_a_im, s5_log_dt, s5_b_re, s5_b_im, s5_c_re, s5_c_im, s5_d, s5_glu_w, s5_glu_b, moe_w_rg, moe_b_rg, moe_w_re, moe_b_re, moe_w1, moe_w3, moe_w2):
    bsz, seq, d = x.shape
    depth = ada_w.shape[0]
    n_tok = bsz * seq
    heads_a = mlstm_w_q.shape[1]
    da = heads_a * HEAD_DIM
    db = rwkv_w0.shape[1]
    dc = s5_d.shape[1]
    rw_cols = rwkv_mu.shape[1]
    assert seq % CHUNK == 0 and w_in.shape[2] == 3 * da + 2 * heads_a + rw_cols + dc
    assert 2 * HEAD_DIM == LANES and CHUNK == HEAD_DIM and EXPERTS_PER_GROUP == 8
    widths = (3 * da, 2 * LANES, rw_cols, dc)

    mod = _modulation(c, ada_w, ada_b).reshape(depth, bsz, 6, 1, d)
    xc = x.reshape(n_tok, d)
    v_first = None
    for l in range(depth):
        mod_l = mod[l]
        qkvo, gates, pcols, u = _in_proj(xc, mod_l, norm_mix[l].reshape(1, d), w_in, l, widths, da,
                                         heads_a, seq)

        ya = _mlstm(qkvo, gates, mlstm_conv_w[l], mlstm_conv_b[l], mlstm_w_q[l], mlstm_w_k[l],
                    mlstm_b_i[l], mlstm_b_f[l], mlstm_norm_w[l], bsz, seq)
        rprm = dict(mu=rwkv_mu[l], w0=rwkv_w0[l], w_up=rwkv_w_up[l], a0=rwkv_a0[l], a_up=rwkv_a_up[l],
                    g_up=rwkv_g_up[l], k_k=rwkv_k_k[l], k_a=rwkv_k_a[l], r_k=rwkv_r_k[l],
                    ln_w=rwkv_ln_w[l], ln_b=rwkv_ln_b[l])
        if l > 0:
            rprm.update(v0=rwkv_v0[l - 1], v_dn=rwkv_v_dn[l - 1], v_up=rwkv_v_up[l - 1])
        yb, v_first = _rwkv(pcols, v_first if l > 0 else None, rprm, bsz, seq)
        sprm = dict(a_re=s5_a_re[l], a_im=s5_a_im[l], log_dt=s5_log_dt[l], b_re=s5_b_re[l],
                    b_im=s5_b_im[l], c_re=s5_c_re[l], c_im=s5_c_im[l])
        yc = _s5_core(u, sprm, bsz, seq)

        x1, h2, ri, rw, hist = _out_proj(ya, yb, yc, u, xc, mod_l, norm_ffn[l].reshape(1, d), s5_d[l],
                                         s5_glu_w[l], s5_glu_b[l], w_out[l], moe_w_rg[l], moe_b_rg[l],
                                         moe_w_re[l], moe_b_re[l], seq)
        dest, blk_e, n_rows = _route(ri, hist, n_tok, n_tok // hist.shape[0])
        w13 = jnp.concatenate([moe_w1[l], moe_w3[l]], axis=2).astype(BF16)
        y_rows = _moe_experts(_dispatch(h2, dest, n_rows, seq), blk_e, w13, moe_w2[l].astype(BF16))
        xc = _combine(x1, y_rows, dest, rw, mod_l, norm_final.reshape(1, d), seq, final=(l == depth - 1))
    return xc.reshape(bsz, seq, d).astype(x.dtype)
```

```python
import functools

import jax
import jax.numpy as jnp
from jax import lax
from jax.experimental import pallas as pl
from jax.experimental.pallas import tpu as pltpu

F32 = jnp.float32
BF16 = jnp.bfloat16

HEAD_DIM = 64
CHUNK = 64
CONV_K = 4
S5_GC = 16
S5_P = 64
N_GROUPS = 4
EXPERTS_PER_GROUP = 8
N_EXPERTS = N_GROUPS * EXPERTS_PER_GROUP
NORM_EPS = 1e-6
HEAD_NORM_EPS = 1e-5
RWKV_GN_EPS = 64e-5
L2_EPS = 1e-12
LANES = 128
SUBLANES = 8
MOE_ROWS = 256
VMEM_LIMIT = 56 * 1024 * 1024


def _cparams(*sem):
    return pltpu.CompilerParams(dimension_semantics=sem, vmem_limit_bytes=VMEM_LIMIT)


def _row_tile(n, want):
    t = min(n, want)
    assert n % t == 0
    return t


def _dot(a, b):
    return jnp.dot(a.astype(BF16), b.astype(BF16), preferred_element_type=F32)


def _dot_nt(a, b):
    return lax.dot_general(a.astype(BF16), b.astype(BF16), (((1,), (1,)), ((), ())),
                           preferred_element_type=F32)


def _dot_tn(a, b):
    return lax.dot_general(a.astype(BF16), b.astype(BF16), (((0,), (0,)), ((), ())),
                           preferred_element_type=F32)


def _split(a):
    hi = a.astype(BF16)
    lo = (a - hi.astype(F32)).astype(BF16)
    return hi, lo


def _dot_xa(a, b_exact):
    hi, lo = _split(a)
    return (jnp.dot(hi, b_exact, preferred_element_type=F32)
            + jnp.dot(lo, b_exact, preferred_element_type=F32))


def _dot_xb(a_exact, b):
    hi, lo = _split(b)
    return (jnp.dot(a_exact, hi, preferred_element_type=F32)
            + jnp.dot(a_exact, lo, preferred_element_type=F32))


def _dot_x3(a, b_exact):
    hi = a.astype(BF16)
    r1 = a - hi.astype(F32)
    mid = r1.astype(BF16)
    lo = (r1 - mid.astype(F32)).astype(BF16)
    return (jnp.dot(hi, b_exact, preferred_element_type=F32)
            + jnp.dot(mid, b_exact, preferred_element_type=F32)
            + jnp.dot(lo, b_exact, preferred_element_type=F32))


def _dot3(a, b):
    ah, al = _split(a)
    bh, bl = _split(b)
    return (jnp.dot(ah, bh, preferred_element_type=F32)
            + jnp.dot(ah, bl, preferred_element_type=F32)
            + jnp.dot(al, bh, preferred_element_type=F32))


def _dot3_nt(a, b):
    ah, al = _split(a)
    bh, bl = _split(b)
    dn = (((1,), (1,)), ((), ()))
    return (lax.dot_general(ah, bh, dn, preferred_element_type=F32)
            + lax.dot_general(ah, bl, dn, preferred_element_type=F32)
            + lax.dot_general(al, bh, dn, preferred_element_type=F32))


def _sigmoid(x):
    return 1.0 / (1.0 + jnp.exp(-x))


def _silu(x):
    return x * _sigmoid(x)


def _log_sigmoid(x):
    return jnp.minimum(x, 0.0) - jnp.log1p(jnp.exp(-jnp.abs(x)))


def _rmsnorm(x, g):
    ms = jnp.mean(x * x, axis=-1, keepdims=True)
    return x * lax.rsqrt(ms + NORM_EPS) * g


def _tri_incl(n):
    r = lax.broadcasted_iota(jnp.int32, (n, n), 0)
    c = lax.broadcasted_iota(jnp.int32, (n, n), 1)
    return (c <= r).astype(BF16)


def _head_ones(width):
    r = lax.broadcasted_iota(jnp.int32, (width, width), 0) // HEAD_DIM
    c = lax.broadcasted_iota(jnp.int32, (width, width), 1) // HEAD_DIM
    return (r == c).astype(BF16)


def _mod_kernel(c_ref, w_ref, b_ref, o_ref):
    o_ref[0] = _dot(_silu(c_ref[...]), w_ref[0]) + b_ref[0]


def _modulation(c, ada_w, ada_b):
    depth, d, d6 = ada_w.shape
    bsz = c.shape[0]
    tn = _row_tile(d6, 1024)
    return pl.pallas_call(
        _mod_kernel,
        grid=(depth, d6 // tn),
        in_specs=[pl.BlockSpec((bsz, d), lambda l, j: (0, 0)),
                  pl.BlockSpec((1, d, tn), lambda l, j: (l, 0, j)),
                  pl.BlockSpec((1, 1, tn), lambda l, j: (l, 0, j))],
        out_specs=pl.BlockSpec((1, bsz, tn), lambda l, j: (l, 0, j)),
        out_shape=jax.ShapeDtypeStruct((depth, bsz, d6), F32),
        compiler_params=_cparams("parallel", "parallel"),
        name="adaln_mod",
    )(c, ada_w, ada_b.reshape(depth, 1, d6))


def _in_kernel(x_ref, sh_ref, sc_ref, g_ref, w_ref, *refs, widths, da, heads):
    out_refs, w_sc = refs[:-1], refs[-1]
    d = w_ref.shape[1]
    rw_cols, dc = widths[2], widths[3]

    @pl.when(pl.program_id(0) == 0)
    def _():
        rb = 256
        g0 = 3 * da
        r_src = g0 + 2 * heads
        u_src = r_src + rw_cols
        r_win = -(-(2 * heads + rw_cols) // LANES) * LANES
        u_al = u_src // LANES * LANES
        lane = lax.broadcasted_iota(jnp.int32, (rb, LANES), 1)
        for r0 in range(0, d, rb):
            rs = slice(r0, r0 + rb)
            w_sc[rs, 0:g0] = w_ref[0, rs, 0:g0].astype(BF16)
            gblk = w_ref[0, rs, g0:g0 + LANES]
            w_sc[rs, g0:g0 + LANES] = jnp.where(lane < heads, gblk, 0.0).astype(BF16)
            w_sc[rs, g0 + LANES:g0 + 2 * LANES] = jnp.where(
                lane < heads, pltpu.roll(gblk, LANES - heads, 1), 0.0).astype(BF16)
            win = w_ref[0, rs, g0:g0 + r_win]
            w_sc[rs, g0 + 2 * LANES:g0 + 2 * LANES + rw_cols] = win[:, 2 * heads:2 * heads + rw_cols].astype(BF16)
            win2 = w_ref[0, rs, u_al:u_src + dc]
            w_sc[rs, g0 + 2 * LANES + rw_cols:] = win2[:, u_src - u_al:u_src - u_al + dc].astype(BF16)

    h = _rmsnorm(x_ref[...], g_ref[...]) * (1.0 + sc_ref[0, 0]) + sh_ref[0, 0]
    hb = h.astype(BF16)
    off = 0
    for o_ref, wd in zip(out_refs, widths):
        o_ref[...] = jnp.dot(hb, w_sc[:, off:off + wd], preferred_element_type=F32)
        off += wd


def _in_proj(x2d, mod_l, g, w_in, layer, widths, da, heads, seq):
    n_tok, d = x2d.shape
    cols = w_in.shape[2]
    tm = _row_tile(seq, 512)
    per_b = seq // tm
    return pl.pallas_call(
        functools.partial(_in_kernel, widths=widths, da=da, heads=heads),
        grid=(n_tok // tm,),
        in_specs=[pl.BlockSpec((tm, d), lambda i: (i, 0)),
                  pl.BlockSpec((1, 1, 1, d), lambda i: (i // per_b, 0, 0, 0)),
                  pl.BlockSpec((1, 1, 1, d), lambda i: (i // per_b, 1, 0, 0)),
                  pl.BlockSpec((1, d), lambda i: (0, 0)),
                  pl.BlockSpec((1, d, cols), lambda i: (layer, 0, 0))],
        out_specs=[pl.BlockSpec((tm, wd), lambda i: (i, 0)) for wd in widths],
        out_shape=[jax.ShapeDtypeStruct((n_tok, wd), F32) for wd in widths],
        scratch_shapes=[pltpu.VMEM((d, sum(widths)), BF16)],
        compiler_params=_cparams("arbitrary"),
        name="in_proj",
    )(x2d, mod_l, mod_l, g, w_in)


def _mlstm_kernel(qkvo_ref, gate_ref, cw_ref, cb_ref, wq_ref, wk_ref, gb_ref, nw_ref, out_ref,
                  xf_sc, q_sc, k_sc, gi_sc, gf_sc, cn_sc, m_sc, *, nb, tb, heads):
    dh, L = HEAD_DIM, CHUNK
    da = heads * dh
    i = pl.program_id(0)

    @pl.when(i == 0)
    def _():
        xf_sc[:, 0:SUBLANES, :] = jnp.zeros((nb, SUBLANES, da), F32)
        cn_sc[...] = jnp.zeros_like(cn_sc)
        m_sc[...] = jnp.zeros_like(m_sc)

    for b in range(nb):
        xqk = qkvo_ref[b, :, 0:da]
        xf_sc[b, pl.ds(SUBLANES, tb), :] = xqk
        acc = xqk * cw_ref[CONV_K - 1:CONV_K, :] + cb_ref[...]
        for j in range(1, CONV_K):
            acc = acc + xf_sc[b, pl.ds(SUBLANES - j, tb), :] * cw_ref[CONV_K - 1 - j:CONV_K - j, :]
        xf_sc[b, 0:SUBLANES, :] = xf_sc[b, pl.ds(tb, SUBLANES), :]
        cx = _silu(acc).astype(BF16)
        q_sc[b] = jnp.dot(cx, wq_ref[...], preferred_element_type=F32)
        k_sc[b] = jnp.dot(cx, wk_ref[...], preferred_element_type=F32) * (dh ** -0.5)

        g = gate_ref[b] + gb_ref[...]
        gi_sc[b] = g[:, :LANES]
        gf_sc[b] = _log_sigmoid(g[:, LANES:])

    tri = _tri_incl(L)
    rr = lax.broadcasted_iota(jnp.int32, (L, L), 0)
    cc = lax.broadcasted_iota(jnp.int32, (L, L), 1)
    causal = cc <= rr
    row_l = lax.broadcasted_iota(jnp.int32, (L, LANES), 0)
    sel = (lax.broadcasted_iota(jnp.int32, (LANES, heads * LANES), 0)
           == lax.broadcasted_iota(jnp.int32, (LANES, heads * LANES), 1) // LANES).astype(BF16)
    ones_v = jnp.ones((L, dh), F32)
    mean_m = jnp.full((dh, dh), 1.0 / dh, BF16)

    units = [(b, h) for b in range(nb) for h in range(heads)]
    ur = range(len(units))

    def hcol(h, base=0):
        return slice(base + h * dh, base + (h + 1) * dh)

    def chunk(c, carry):
        rows = pl.ds(pl.multiple_of(c * L, L), L)
        ex, g_t = [], []
        for b in range(nb):
            bc = _dot_xb(tri, gf_sc[b, rows, :])
            gtot = bc[L - 1:L, :]
            g = gi_sc[b, rows, :] - bc
            gmax = jnp.max(g, axis=0, keepdims=True)
            m_prev = m_sc[b]
            m_loc = gtot + gmax
            m_new = jnp.maximum(gtot + m_prev, m_loc)
            a_old = jnp.exp(gtot + m_prev - m_new)
            a_loc = jnp.exp(m_loc - m_new)
            m_sc[b] = m_new
            pm = g
            d = 1
            while d < L:
                pm = jnp.maximum(pm, jnp.where(row_l >= d, pltpu.roll(pm, d, 0), -jnp.inf))
                d *= 2
            mm = jnp.maximum(m_prev, pm)
            stack = jnp.concatenate([jnp.exp(g - gmax), mm, jnp.exp(m_prev - mm), jnp.exp(-(bc + mm)),
                                     a_old, a_loc, jnp.zeros((SUBLANES - 2, LANES), F32)], axis=0)
            ex.append(_dot_x3(stack, sel))
            g_t.append(g.T)

        def part(k, b, h, width=dh):
            return ex[b][k * L:(k + 1) * L, h * LANES:h * LANES + width]

        qc = [q_sc[b, rows, hcol(h)] for b, h in units]
        kc = [k_sc[b, rows, hcol(h)] for b, h in units]
        vo = [jnp.concatenate([qkvo_ref[b, rows, hcol(h, da)], ones_v], axis=1) for b, h in units]
        oc = [qkvo_ref[b, rows, hcol(h, 2 * da)] for b, h in units]
        cn_prev = [cn_sc[b, h] for b, h in units]
        s_raw = [_dot_nt(qc[u], kc[u]) for u in ur]
        q_cn = [_dot(qc[u], cn_prev[u]) for u in ur]
        cn_loc = [_dot_tn(kc[u] * part(0, b, h), vo[u]) for u, (b, h) in enumerate(units)]
        s_qk = [s_raw[u] * jnp.where(causal, jnp.exp(g_t[b][h:h + 1, :] - part(1, b, h)), 0.0)
                for u, (b, h) in enumerate(units)]
        s_vn = [_dot(s_qk[u], vo[u]) for u in ur]
        nd = [part(2, b, h, 2 * dh) * q_cn[u] + s_vn[u] for u, (b, h) in enumerate(units)]
        hh = [nd[u][:, :dh] / jnp.maximum(jnp.abs(nd[u][:, dh:]), part(3, b, h))
              for u, (b, h) in enumerate(units)]
        mu = [_dot_xa(hh[u], mean_m) for u in ur]
        dlt = [hh[u] - mu[u] for u in ur]
        var = [_dot_xa(dlt[u] * dlt[u], mean_m) for u in ur]
        outs = [dlt[u] * lax.rsqrt(var[u] + HEAD_NORM_EPS) * _sigmoid(oc[u]) for u in ur]
        for b in range(nb):
            mine = [u for u in ur if units[u][0] == b]
            out_ref[b, rows, :] = jnp.concatenate([outs[u] for u in mine], axis=1) * nw_ref[...]
        for u, (b, h) in enumerate(units):
            cn_sc[b, h] = (ex[b][4 * L:4 * L + 1, h * LANES:(h + 1) * LANES] * cn_prev[u]
                           + ex[b][4 * L + 1:4 * L + 2, h * LANES:(h + 1) * LANES] * cn_loc[u])
        return carry

    lax.fori_loop(0, tb // L, chunk, 0)


def _block_diag(w):
    heads, dh, _ = w.shape
    eye = jnp.eye(heads, dtype=w.dtype)
    return (eye[:, None, :, None] * w[:, :, None, :]).reshape(heads * dh, heads * dh)


def _mlstm(qkvo, gates, conv_w, conv_b, w_q, w_k, b_i, b_f, norm_w, bsz, seq):
    heads = w_q.shape[0]
    da = heads * HEAD_DIM
    tb = _row_tile(seq, 256)
    nblk = seq // tb
    gbias = jnp.zeros((1, 2 * LANES), F32).at[0, :heads].set(b_i).at[0, LANES:LANES + heads].set(b_f)
    kern = functools.partial(_mlstm_kernel, nb=bsz, tb=tb, heads=heads)
    row = lambda i: (0, i, 0)
    const = lambda i: (0, 0)
    out = pl.pallas_call(
        kern,
        grid=(nblk,),
        in_specs=[pl.BlockSpec((bsz, tb, 3 * da), row),
                  pl.BlockSpec((bsz, tb, 2 * LANES), row),
                  pl.BlockSpec((CONV_K, da), const),
                  pl.BlockSpec((1, da), const),
                  pl.BlockSpec((da, da), const),
                  pl.BlockSpec((da, da), const),
                  pl.BlockSpec((1, 2 * LANES), const),
                  pl.BlockSpec((1, da), const)],
        out_specs=pl.BlockSpec((bsz, tb, da), row),
        out_shape=jax.ShapeDtypeStruct((bsz, seq, da), F32),
        scratch_shapes=[pltpu.VMEM((bsz, tb + SUBLANES, da), F32),
                        pltpu.VMEM((bsz, tb, da), F32),
                        pltpu.VMEM((bsz, tb, da), F32),
                        pltpu.VMEM((bsz, tb, LANES), F32),
                        pltpu.VMEM((bsz, tb, LANES), F32),
                        pltpu.VMEM((bsz, heads, HEAD_DIM, 2 * HEAD_DIM), F32),
                        pltpu.VMEM((bsz, 1, LANES), F32)],
        compiler_params=_cparams("arbitrary"),
        name="mlstm",
    )(qkvo.reshape(bsz, seq, 3 * da), gates.reshape(bsz, seq, 2 * LANES), conv_w, conv_b.reshape(1, da),
      _block_diag(w_q).astype(BF16), _block_diag(w_k).astype(BF16), gbias, norm_w.reshape(1, da))
    return out.reshape(bsz * seq, da)


def _rwkv_kernel(*refs, nb, tb, heads, lw_dim, la_dim, lg_dim, has_vres):
    dh, L = HEAD_DIM, CHUNK
    db = heads * dh
    it = iter(refs)
    p_ref = next(it)
    vf_ref = next(it) if has_vres else None
    (mu_ref, w0_ref, wup_ref, a0_ref, aup_ref, gup_ref, kk_ref, ka_ref, rk_ref,
     lnw_ref, lnb_ref) = (next(it) for _ in range(11))
    if has_vres:
        v0_ref, vdn_ref, vup_ref = (next(it) for _ in range(3))
    y_ref = next(it)
    vout_ref = None if has_vres else next(it)
    (xf_sc, r_sc, k_sc, v_sc, a_sc, b_sc, lw_sc, y_sc, gate_sc, rp_sc, st_sc, q_sc, z_sc,
     gl_sc) = (next(it) for _ in range(14))

    i = pl.program_id(0)

    @pl.when(i == 0)
    def _():
        xf_sc[:, 0:SUBLANES, :] = jnp.zeros((nb, SUBLANES, xf_sc.shape[2]), F32)
        st_sc[...] = jnp.zeros_like(st_sc)

    ones_h = _head_ones(db)
    for b in range(nb):
        p = p_ref[b]
        xf_sc[b, pl.ds(SUBLANES, tb), :] = p
        prev = xf_sc[b, pl.ds(SUBLANES - 1, tb), :]
        xf_sc[b, 0:SUBLANES, :] = xf_sc[b, pl.ds(tb, SUBLANES), :]
        p = p + mu_ref[...] * (prev - p)

        o = 0
        r = p[:, o:o + db]; o += db
        k = p[:, o:o + db]; o += db
        v = p[:, o:o + db]; o += db
        wd = p[:, o:o + lw_dim]; o += lw_dim
        ad = p[:, o:o + la_dim]; o += la_dim
        gd = p[:, o:o + lg_dim]

        wlog = _log_sigmoid(w0_ref[...] + _dot(jnp.tanh(wd), wup_ref[...])) - 0.5
        lw_sc[b] = -jnp.exp(wlog)
        a = _sigmoid(a0_ref[...] + _dot(ad, aup_ref[...]))
        gate_sc[b] = _dot(_sigmoid(gd), gup_ref[...])
        if has_vres:
            v = v + (vf_ref[b] - v) * _sigmoid(v0_ref[...] + _dot(_dot(v, vdn_ref[...]), vup_ref[...]))
        else:
            vout_ref[b] = v
        kk = k * kk_ref[...]
        kk = kk / jnp.maximum(jnp.sqrt(_dot_xa(kk * kk, ones_h)), L2_EPS)
        r_sc[b] = r
        k_sc[b] = k * (1.0 + (a - 1.0) * ka_ref[...])
        v_sc[b] = v
        a_sc[b] = -kk
        b_sc[b] = kk * a

    tri = _tri_incl(L)
    rr = lax.broadcasted_iota(jnp.int32, (L, L), 0)
    cc = lax.broadcasted_iota(jnp.int32, (L, L), 1)
    strict = cc < rr
    incl = cc <= rr

    units = [(b, h) for b in range(nb) for h in range(heads)]
    ur = range(len(units))

    def hcol(h):
        return slice(h * dh, (h + 1) * dh)

    def chunk(c, carry):
        rows = pl.ds(pl.multiple_of(c * L, L), L)
        at, rt, bt, kt, bg, kg, vch = ([] for _ in range(7))
        for b in range(nb):
            lwc = lw_sc[b, rows, :]
            cum = _dot_xb(tri, lwc)
            cum_l = cum[L - 1:L, :]
            e_inv = jnp.exp(-cum)
            e_end = jnp.exp(cum_l - cum)
            bv = b_sc[b, rows, :]
            kv = k_sc[b, rows, :]
            at.append(a_sc[b, rows, :] * jnp.exp(cum - lwc))
            rt.append(r_sc[b, rows, :] * jnp.exp(cum))
            bt.append(bv * e_inv)
            kt.append(kv * e_inv)
            bg.append(bv * e_end)
            kg.append(kv * e_end)
            vch.append(v_sc[b, rows, :])
            gl_sc[c, b] = jnp.exp(cum_l)
        vh = [vch[b][:, hcol(h)] for b, h in units]
        g4 = [_dot_nt(jnp.concatenate([at[b][:, hcol(h)], rt[b][:, hcol(h)]], axis=0),
                      jnp.concatenate([bt[b][:, hcol(h)], kt[b][:, hcol(h)]], axis=0))
              for b, h in units]
        pw = [jnp.where(strict, g4[u][:L, :L], 0.0) for u in ur]
        n_ak = [jnp.where(strict, g4[u][:L, L:], 0.0) for u in ur]
        m_rb = [jnp.where(incl, g4[u][L:, :L], 0.0) for u in ur]
        m_rk = [jnp.where(incl, g4[u][L:, L:], 0.0) for u in ur]
        x = [jnp.concatenate([at[b][:, hcol(h)], _dot(n_ak[u], vh[u])], axis=1)
             for u, (b, h) in enumerate(units)]
        for step in range(6):
            x = [x[u] + _dot(pw[u], x[u]) for u in ur]
            if step < 5:
                pw = [_dot(pw[u], pw[u]) for u in ur]
        ry = [jnp.concatenate([rt[b][:, hcol(h)], _dot(m_rk[u], vh[u])], axis=1) + _dot(m_rb[u], x[u])
              for u, (b, h) in enumerate(units)]
        qz = [_dot_tn(x[u], bg[b][:, hcol(h)]) for u, (b, h) in enumerate(units)]
        z2 = [_dot_tn(vh[u], kg[b][:, hcol(h)]) for u, (b, h) in enumerate(units)]
        for b in range(nb):
            mine = [u for u in ur if units[u][0] == b]
            rp_sc[b, rows, :] = jnp.concatenate([ry[u][:, :dh] for u in mine], axis=1)
            y_sc[b, rows, :] = jnp.concatenate([ry[u][:, dh:] for u in mine], axis=1)
        for u, (b, h) in enumerate(units):
            q_sc[c, b, h] = qz[u][:dh]
            z_sc[c, b, h] = qz[u][dh:] + z2[u]
        return carry

    lax.fori_loop(0, tb // L, chunk, 0)

    def carry_state(c, carry):
        rows = pl.ds(pl.multiple_of(c * L, L), L)
        st = [st_sc[b, h] for b, h in units]
        ys = [_dot_nt(rp_sc[b, rows, hcol(h)], st[u]) for u, (b, h) in enumerate(units)]
        sq = [_dot_xa(st[u], q_sc[c, b, h].astype(BF16)) for u, (b, h) in enumerate(units)]
        for b in range(nb):
            mine = [u for u in ur if units[u][0] == b]
            y_sc[b, rows, :] = y_sc[b, rows, :] + jnp.concatenate([ys[u] for u in mine], axis=1)
        for u, (b, h) in enumerate(units):
            st_sc[b, h] = st[u] * gl_sc[c, b][:, hcol(h)] + sq[u] + z_sc[c, b, h]
        return carry

    lax.fori_loop(0, tb // L, carry_state, 0)

    for b in range(nb):
        y = y_sc[b]
        mean = _dot_xa(y, ones_h) * (1.0 / dh)
        dlt = y - mean
        var = _dot_xa(dlt * dlt, ones_h) * (1.0 / dh)
        yn = dlt * lax.rsqrt(var + RWKV_GN_EPS) * lnw_ref[...] + lnb_ref[...]
        bonus = _dot_xa(r_sc[b] * k_sc[b] * rk_ref[...], ones_h) * v_sc[b]
        y_ref[b] = (yn + bonus) * gate_sc[b]


def _rwkv(pcols, v_first, prm, bsz, seq):
    db = prm["w0"].shape[0]
    heads = db // HEAD_DIM
    cols = pcols.shape[1]
    lw_dim, la_dim, lg_dim = prm["w_up"].shape[0], prm["a_up"].shape[0], prm["g_up"].shape[0]
    has_vres = v_first is not None
    tb = _row_tile(seq, 256)
    nblk = seq // tb
    row = lambda i: (0, i, 0)
    const = lambda i: (0, 0)
    vec = lambda a: a.reshape(1, -1).astype(F32)

    args = [pcols.reshape(bsz, seq, cols)]
    specs = [pl.BlockSpec((bsz, tb, cols), row)]
    if has_vres:
        args.append(v_first.reshape(bsz, seq, db))
        specs.append(pl.BlockSpec((bsz, tb, db), row))
    small = [vec(prm["mu"]), vec(prm["w0"]), prm["w_up"].astype(BF16), vec(prm["a0"]),
             prm["a_up"].astype(BF16), prm["g_up"].astype(BF16), vec(prm["k_k"]), vec(prm["k_a"]),
             vec(prm["r_k"]), vec(prm["ln_w"]), vec(prm["ln_b"])]
    if has_vres:
        lv = prm["v_dn"].shape[1]
        lvp = -(-lv // LANES) * LANES
        v_dn = jnp.zeros((db, lvp), F32).at[:, :lv].set(prm["v_dn"]).astype(BF16)
        v_up = jnp.zeros((lvp, db), F32).at[:lv, :].set(prm["v_up"]).astype(BF16)
        small += [vec(prm["v0"]), v_dn, v_up]
    args += small
    specs += [pl.BlockSpec(a.shape, const) for a in small]

    out_shape = [jax.ShapeDtypeStruct((bsz, seq, db), F32)]
    out_specs = [pl.BlockSpec((bsz, tb, db), row)]
    if not has_vres:
        out_shape.append(jax.ShapeDtypeStruct((bsz, seq, db), F32))
        out_specs.append(pl.BlockSpec((bsz, tb, db), row))

    kern = functools.partial(_rwkv_kernel, nb=bsz, tb=tb, heads=heads, lw_dim=lw_dim, la_dim=la_dim,
                             lg_dim=lg_dim, has_vres=has_vres)
    nch = tb // CHUNK
    res = pl.pallas_call(
        kern,
        grid=(nblk,),
        in_specs=specs,
        out_specs=out_specs,
        out_shape=out_shape,
        scratch_shapes=[pltpu.VMEM((bsz, tb + SUBLANES, cols), F32)]
        + [pltpu.VMEM((bsz, tb, db), F32) for _ in range(9)]
        + [pltpu.VMEM((bsz, heads, HEAD_DIM, HEAD_DIM), F32),
           pltpu.VMEM((nch, bsz, heads, HEAD_DIM, HEAD_DIM), F32),
           pltpu.VMEM((nch, bsz, heads, HEAD_DIM, HEAD_DIM), F32),
           pltpu.VMEM((nch, bsz, 1, db), F32)],
        compiler_params=_cparams("arbitrary"),
        name="rwkv7",
    )(*args)
    res = [a.reshape(bsz * seq, db) for a in res]
    return (res[0], v_first) if has_vres else (res[0], res[1])


def _cmul(ar, ai, br, bi):
    return ar * br - ai * bi, ar * bi + ai * br


def _shift_rows(x, d):
    row = lax.broadcasted_iota(jnp.int32, x.shape, 0)
    return jnp.where(row >= d, pltpu.roll(x, d, 0), 0.0)


def _s5_kernel(u_ref, are_r, aim_r, dt_r, are_c, aim_c, dt_c, bre_ref, bim_ref, cre_ref, cim_ref,
               y_ref, m_sc, *, bsz, nchunk):
    L, gc, P = CHUNK, S5_GC, S5_P
    n = L * gc
    a_re, a_im, dt = are_r[0], aim_r[0], jnp.exp(dt_r[0])
    mag, ang = jnp.exp(a_re * dt), a_im * dt
    ab_re, ab_im = mag * jnp.cos(ang), mag * jnp.sin(ang)
    inv = 1.0 / (a_re * a_re + a_im * a_im)
    co_re = ((ab_re - 1.0) * a_re + ab_im * a_im) * inv
    co_im = (ab_im * a_re - (ab_re - 1.0) * a_im) * inv
    bb_re, bb_im = _cmul(co_re, co_im, bre_ref[0], bim_ref[0])
    c_re, c_im = cre_ref[0], cim_ref[0]

    def powers(tau):
        m = jnp.exp(tau * (a_re * dt))
        return m * jnp.cos(tau * ang), m * jnp.sin(tau * ang)

    a_re_c, a_im_c, dt_c_ = are_c[0], aim_c[0], jnp.exp(dt_c[0])
    tau_row = lax.broadcasted_iota(jnp.int32, (P, L), 1).astype(F32)
    pm = jnp.exp(tau_row * (a_re_c * dt_c_))
    pt_re = pm * jnp.cos(tau_row * (a_im_c * dt_c_))
    pt_im = pm * jnp.sin(tau_row * (a_im_c * dt_c_))
    pair = lax.broadcasted_iota(jnp.int32, (gc * gc, gc), 0)
    col = lax.broadcasted_iota(jnp.int32, (gc * gc, gc), 1)
    rep_c = (pair // gc == col).astype(BF16)
    rep_b = (pair % gc == col).astype(BF16)
    cb_re, cb_im = _cmul(_dot_xb(rep_c, c_re), _dot_xb(rep_c, c_im),
                         _dot_xb(rep_b, bb_re), _dot_xb(rep_b, bb_im))
    kap = _dot3(cb_re, pt_re) - _dot3(cb_im, pt_im)

    kap_pad = jnp.concatenate([kap, jnp.zeros_like(kap)], axis=1)
    srow = lax.broadcasted_iota(jnp.int32, (L, LANES), 0)
    tcol = lax.broadcasted_iota(jnp.int32, (L, LANES), 1)
    for cp in range(gc):
        for c2 in range(0, gc, 2):
            k0 = jnp.broadcast_to(kap_pad[c2 * gc + cp:c2 * gc + cp + 1, :], (L, LANES))
            k1 = jnp.broadcast_to(kap_pad[(c2 + 1) * gc + cp:(c2 + 1) * gc + cp + 1, :], (L, LANES))
            t0 = pltpu.roll(k0, 0, 1, stride=1, stride_axis=0)
            t1 = pltpu.roll(k1, L, 1, stride=1, stride_axis=0)
            blk = jnp.where(tcol < L, jnp.where(tcol >= srow, t0, 0.0),
                            jnp.where(tcol - L >= srow, t1, 0.0))
            m_sc[cp * L:(cp + 1) * L, c2 * L:(c2 + 2) * L] = blk.astype(BF16)

    s_col = lax.broadcasted_iota(jnp.int32, (L, P), 0).astype(F32)
    pw_re, pw_im = powers((L - 1.0) - s_col)
    pg_re, pg_im = powers(s_col + 1.0)
    w_re, w_im, g_re, g_im = [], [], [], []
    for c in range(gc):
        br = jnp.broadcast_to(bb_re[c:c + 1, :], (L, P))
        bi = jnp.broadcast_to(bb_im[c:c + 1, :], (L, P))
        wr, wi = _cmul(br, bi, pw_re, pw_im)
        w_re.append(wr); w_im.append(wi)
        cr = jnp.broadcast_to(c_re[c:c + 1, :], (L, P))
        ci = jnp.broadcast_to(c_im[c:c + 1, :], (L, P))
        gr, gi = _cmul(cr, ci, pg_re, pg_im)
        g_re.append(gr); g_im.append(gi)
    w_re, w_im = jnp.concatenate(w_re, axis=0), jnp.concatenate(w_im, axis=0)
    g_re, g_im = jnp.concatenate(g_re, axis=0), jnp.concatenate(g_im, axis=0)

    u = u_ref[0].astype(BF16)
    x_re = jnp.dot(u, w_re.astype(BF16), preferred_element_type=F32)
    x_im = jnp.dot(u, w_im.astype(BF16), preferred_element_type=F32)
    xs_re, xs_im = [], []
    for b in range(bsz):
        xr = x_re[b * nchunk:(b + 1) * nchunk]
        xi = x_im[b * nchunk:(b + 1) * nchunk]
        d = 1
        while d < nchunk:
            ar_, ai_ = powers(float(L * d))
            sr, si = _cmul(ar_, ai_, _shift_rows(xr, d), _shift_rows(xi, d))
            xr, xi = xr + sr, xi + si
            d *= 2
        xs_re.append(_shift_rows(xr, 1))
        xs_im.append(_shift_rows(xi, 1))
    xs_re, xs_im = jnp.concatenate(xs_re, axis=0), jnp.concatenate(xs_im, axis=0)
    y = jnp.dot(u, m_sc[...], preferred_element_type=F32)
    y = y + _dot_nt(xs_re, g_re) - _dot_nt(xs_im, g_im)
    y_ref[0] = y


def _s5_core(u2d, prm, bsz, seq):
    L, gc, P = CHUNK, S5_GC, S5_P
    groups = u2d.shape[1] // gc
    nchunk = seq // L
    rows = bsz * nchunk
    n = L * gc
    ug = u2d.reshape(bsz, nchunk, L, groups, gc).transpose(3, 0, 1, 4, 2).reshape(groups, rows, n)
    row3 = lambda a: a.reshape(groups, 1, P).astype(F32)
    col3 = lambda a: a.reshape(groups, P, 1).astype(F32)
    dt_b = jnp.broadcast_to(prm["log_dt"][:, None], (groups, P))
    args = [ug, row3(prm["a_re"]), row3(prm["a_im"]), row3(dt_b),
            col3(prm["a_re"]), col3(prm["a_im"]), col3(dt_b),
            prm["b_re"].transpose(0, 2, 1), prm["b_im"].transpose(0, 2, 1),
            prm["c_re"], prm["c_im"]]
    g3 = lambda g: (g, 0, 0)
    specs = [pl.BlockSpec((1, rows, n), g3)]
    specs += [pl.BlockSpec((1, 1, P), g3)] * 3 + [pl.BlockSpec((1, P, 1), g3)] * 3
    specs += [pl.BlockSpec((1, gc, P), g3)] * 4
    yg = pl.pallas_call(
        functools.partial(_s5_kernel, bsz=bsz, nchunk=nchunk),
        grid=(groups,),
        in_specs=specs,
        out_specs=pl.BlockSpec((1, rows, n), g3),
        out_shape=jax.ShapeDtypeStruct((groups, rows, n), F32),
        scratch_shapes=[pltpu.VMEM((n, n), BF16)],
        compiler_params=_cparams("parallel"),
        name="s5_core",
    )(*args)
    return yg.reshape(groups, bsz, nchunk, gc, L).transpose(1, 2, 4, 0, 3).reshape(bsz * seq, groups * gc)


def _gelu_tanh(x):
    return 0.5 * x * (1.0 + jnp.tanh(0.7978845608028654 * (x + 0.044715 * x * x * x)))


def _out_kernel(ya_ref, yb_ref, yc_ref, u_ref, x_ref, gt1_ref, sh2_ref, sc2_ref, g_ref, d_ref,
                gw_ref, gbias_ref, wo_ref, wrh_ref, wrl_ref, br_ref,
                x1_ref, h2_ref, ri_ref, rw_ref, hist_ref, *, da, db):
    yc = _gelu_tanh(yc_ref[...] + d_ref[...] * u_ref[...])
    yc = yc * _sigmoid(_dot(yc, gw_ref[...]) + gbias_ref[...])
    mixed = (_dot(ya_ref[...], wo_ref[0:da, :]) + _dot(yb_ref[...], wo_ref[da:da + db, :])
             + _dot(yc, wo_ref[da + db:, :]))
    x1 = x_ref[...] + (1.0 + gt1_ref[0, 0]) * mixed
    x1_ref[...] = x1
    h2 = _rmsnorm(x1, g_ref[...]) * (1.0 + sc2_ref[0, 0]) + sh2_ref[0, 0]
    h2_ref[...] = h2

    hh, hl = _split(h2)
    logits = (jnp.dot(hh, wrh_ref[...], preferred_element_type=F32)
              + jnp.dot(hh, wrl_ref[...], preferred_element_type=F32)
              + jnp.dot(hl, wrh_ref[...], preferred_element_type=F32)) + br_ref[...]
    lane_i = lax.broadcasted_iota(jnp.int32, logits.shape, 1)
    lane = lane_i.astype(F32)
    big = float(LANES)
    neg = -jnp.inf
    is_g = lane_i < N_GROUPS
    lg = jnp.where(is_g, logits, neg)
    gmax = jnp.max(lg, axis=1, keepdims=True)
    gi = jnp.min(jnp.where(is_g & (lg == gmax), lane, big), axis=1, keepdims=True)
    gp = 1.0 / jnp.sum(jnp.where(is_g, jnp.exp(lg - gmax), 0.0), axis=1, keepdims=True)
    e_lane = lane_i - N_GROUPS
    grp_of_lane = lax.shift_right_arithmetic(e_lane, 3).astype(F32)
    in_grp = (e_lane >= 0) & (e_lane < N_EXPERTS) & (grp_of_lane == gi)
    l1 = jnp.where(in_grp, logits, neg)
    m1 = jnp.max(l1, axis=1, keepdims=True)
    i1 = jnp.min(jnp.where(in_grp & (l1 == m1), lane, big), axis=1, keepdims=True)
    rest = in_grp & (lane != i1)
    l2 = jnp.where(rest, logits, neg)
    m2 = jnp.max(l2, axis=1, keepdims=True)
    i2 = jnp.min(jnp.where(rest & (l2 == m2), lane, big), axis=1, keepdims=True)
    e2 = jnp.exp(m2 - m1)
    w1 = gp / (1.0 + e2)
    w2 = gp * e2 / (1.0 + e2)
    hot1 = (lane == i1).astype(F32)
    hot2 = (lane == i2).astype(F32)
    both = (hot1 + hot2).astype(BF16)
    tm = logits.shape[0]
    rr = lax.broadcasted_iota(jnp.int32, (tm, tm), 0)
    cc = lax.broadcasted_iota(jnp.int32, (tm, tm), 1)
    before = jnp.dot((cc < rr).astype(BF16), both, preferred_element_type=F32)
    rank1 = jnp.sum(before * hot1, axis=1, keepdims=True)
    rank2 = jnp.sum(before * hot2, axis=1, keepdims=True)
    hist_ref[0] = jnp.sum(hot1 + hot2, axis=0, keepdims=True)
    ids = jnp.where(lane_i == 0, i1, jnp.where(lane_i == 1, i2, jnp.where(lane_i == 2, rank1, rank2)))
    ri_ref[...] = ids.astype(jnp.int32)
    rw_ref[...] = jnp.where(lane_i == 0, w1, jnp.where(lane_i == 1, w2, 0.0))


def _out_proj(ya, yb, yc, u, x2d, mod_l, g_ffn, s5_d, glu_w, glu_b, w_out, w_rg, b_rg, w_re, b_re, seq):
    n_tok, d = x2d.shape
    da, db, dc = ya.shape[1], yb.shape[1], yc.shape[1]
    tm = _row_tile(seq, 512)
    per_b = seq // tm
    wr = jnp.zeros((d, LANES), F32).at[:, :N_GROUPS].set(w_rg).at[:, N_GROUPS:N_GROUPS + N_EXPERTS].set(w_re)
    wr_hi = wr.astype(BF16)
    wr_lo = (wr - wr_hi.astype(F32)).astype(BF16)
    br = jnp.zeros((1, LANES), F32).at[0, :N_GROUPS].set(b_rg).at[0, N_GROUPS:N_GROUPS + N_EXPERTS].set(b_re)
    row = lambda i: (i, 0)
    const = lambda i: (0, 0)
    modspec = lambda j: pl.BlockSpec((1, 1, 1, d), lambda i: (i // per_b, j, 0, 0))
    return pl.pallas_call(
        functools.partial(_out_kernel, da=da, db=db),
        grid=(n_tok // tm,),
        in_specs=[pl.BlockSpec((tm, da), row), pl.BlockSpec((tm, db), row), pl.BlockSpec((tm, dc), row),
                  pl.BlockSpec((tm, dc), row), pl.BlockSpec((tm, d), row),
                  modspec(2), modspec(3), modspec(4),
                  pl.BlockSpec((1, d), const), pl.BlockSpec((1, dc), const),
                  pl.BlockSpec((dc, dc), const), pl.BlockSpec((1, dc), const),
                  pl.BlockSpec((d, d), const), pl.BlockSpec((d, LANES), const),
                  pl.BlockSpec((d, LANES), const), pl.BlockSpec((1, LANES), const)],
        out_specs=[pl.BlockSpec((tm, d), row), pl.BlockSpec((tm, d), row),
                   pl.BlockSpec((tm, LANES), row), pl.BlockSpec((tm, LANES), row),
                   pl.BlockSpec((1, 1, LANES), lambda i: (i, 0, 0))],
        out_shape=[jax.ShapeDtypeStruct((n_tok, d), F32), jax.ShapeDtypeStruct((n_tok, d), F32),
                   jax.ShapeDtypeStruct((n_tok, LANES), jnp.int32),
                   jax.ShapeDtypeStruct((n_tok, LANES), F32),
                   jax.ShapeDtypeStruct((n_tok // tm, 1, LANES), F32)],
        compiler_params=_cparams("parallel"),
        name="out_proj_router",
    )(ya, yb, yc, u, x2d, mod_l, mod_l, mod_l, g_ffn, s5_d.reshape(1, dc), glu_w.astype(BF16),
      glu_b.reshape(1, dc), w_out.astype(BF16), wr_hi, wr_lo, br)


ROW_UNROLL = 8


def _route(ri, hist, n_tok, tm):
    ntile = n_tok // tm
    h = hist.reshape(ntile, LANES)[:, N_GROUPS:N_GROUPS + N_EXPERTS].astype(jnp.int32)
    counts = jnp.sum(h, axis=0)
    pcounts = (counts + MOE_ROWS - 1) // MOE_ROWS * MOE_ROWS
    pends = jnp.cumsum(pcounts)
    base = (pends - pcounts)[None, :] + jnp.cumsum(h, axis=0) - h
    n_rows = 2 * n_tok + N_EXPERTS * MOE_ROWS
    nblk = n_rows // MOE_ROWS
    blk_start = jnp.arange(nblk, dtype=jnp.int32)[:, None] * MOE_ROWS
    blk_e = jnp.minimum(jnp.sum((pends[None, :] <= blk_start).astype(jnp.int32), axis=1), N_EXPERTS - 1)
    eid = (ri[:, 0:2] - N_GROUPS).reshape(ntile, tm, 2, 1)
    hot = eid == jnp.arange(N_EXPERTS, dtype=jnp.int32)
    dest = jnp.sum(jnp.where(hot, base[:, None, None, :], 0), axis=-1) + ri[:, 2:4].reshape(ntile, tm, 2)
    return (dest.reshape(n_tok, 2).astype(jnp.int32), blk_e.astype(jnp.int32), pends.astype(jnp.int32),
            pcounts.astype(jnp.int32), n_rows)


def _tile_rows(dest, tm):
    nblk = dest.shape[0] // tm
    return dest.reshape(nblk, tm, 2).transpose(0, 2, 1).reshape(nblk, 1, 2 * tm)


def _dispatch_kernel(pend_ref, pcnt_ref, d_ref, h_ref, xs_hbm, zbuf, sem, zsem, *, tm, n_rows):
    @pl.when(pl.program_id(0) == 0)
    def _():
        zbuf[...] = jnp.zeros_like(zbuf)

        def zcopy(start):
            return pltpu.make_async_copy(zbuf, xs_hbm.at[pl.ds(start, MOE_ROWS), :], zsem.at[0])

        first_tail = lax.shift_right_logical(pend_ref[N_EXPERTS - 1], MOE_ROWS.bit_length() - 1)
        nblk = n_rows // MOE_ROWS

        def tail_start(blk, carry):
            zcopy(pl.multiple_of(blk * MOE_ROWS, MOE_ROWS)).start()
            return carry

        def tail_wait(blk, carry):
            zcopy(0).wait()
            return carry

        for e in range(N_EXPERTS):
            @pl.when(pcnt_ref[e] > 0)
            def _():
                zcopy(pl.multiple_of(pend_ref[e] - MOE_ROWS, MOE_ROWS)).start()
        lax.fori_loop(first_tail, nblk, tail_start, 0)
        for e in range(N_EXPERTS):
            @pl.when(pcnt_ref[e] > 0)
            def _():
                zcopy(0).wait()
        lax.fori_loop(first_tail, nblk, tail_wait, 0)

    def body(g, carry):
        for u in range(ROW_UNROLL):
            r = g * ROW_UNROLL + u
            src = h_ref.at[pl.ds(r, 1), :]
            pltpu.make_async_copy(src, xs_hbm.at[pl.ds(d_ref[0, 0, r], 1), :], sem.at[0]).start(priority=0)
            pltpu.make_async_copy(src, xs_hbm.at[pl.ds(d_ref[0, 0, tm + r], 1), :], sem.at[0]).start(priority=1)
        return carry
    lax.fori_loop(0, tm // ROW_UNROLL, body, 0)
    for _ in range(2):
        pltpu.make_async_copy(h_ref, xs_hbm.at[pl.ds(0, tm), :], sem.at[0]).wait()


def _dispatch(h2, dest, pends, pcounts, n_rows, seq):
    n_tok, d = h2.shape
    tm = _row_tile(seq, 512)
    grid_spec = pltpu.PrefetchScalarGridSpec(
        num_scalar_prefetch=2,
        grid=(n_tok // tm,),
        in_specs=[pl.BlockSpec((1, 1, 2 * tm), lambda i, pe, pc: (i, 0, 0), memory_space=pltpu.SMEM),
                  pl.BlockSpec((tm, d), lambda i, pe, pc: (i, 0))],
        out_specs=pl.BlockSpec(memory_space=pl.ANY),
        scratch_shapes=[pltpu.VMEM((MOE_ROWS, d), F32), pltpu.SemaphoreType.DMA((1,)),
                        pltpu.SemaphoreType.DMA((1,))],
    )
    return pl.pallas_call(
        functools.partial(_dispatch_kernel, tm=tm, n_rows=n_rows),
        grid_spec=grid_spec,
        out_shape=jax.ShapeDtypeStruct((n_rows, d), F32),
        compiler_params=_cparams("arbitrary"),
        name="moe_dispatch",
    )(pends, pcounts, _tile_rows(dest, tm), h2)


def _moe_kernel(blk_e_ref, x_ref, w1_ref, w3_ref, w2_ref, y_ref, w1_sc, w3_sc, w2_sc):
    i = pl.program_id(0)

    @pl.when((i == 0) | (blk_e_ref[i] != blk_e_ref[jnp.maximum(i - 1, 0)]))
    def _():
        w1_sc[...] = w1_ref[0, 0].astype(BF16)
        w3_sc[...] = w3_ref[0, 0].astype(BF16)
        w2_sc[...] = w2_ref[0, 0].astype(BF16)

    xb = x_ref[...].astype(BF16)
    act = (_silu(jnp.dot(xb, w1_sc[...], preferred_element_type=F32))
           * jnp.dot(xb, w3_sc[...], preferred_element_type=F32))
    y_ref[...] = jnp.dot(act.astype(BF16), w2_sc[...], preferred_element_type=F32)


def _moe_experts(xs, blk_e, w1, w3, w2, layer):
    n_rows, d = xs.shape
    d_exp = w2.shape[2]
    grid_spec = pltpu.PrefetchScalarGridSpec(
        num_scalar_prefetch=1,
        grid=(n_rows // MOE_ROWS,),
        in_specs=[pl.BlockSpec((MOE_ROWS, d), lambda i, e: (i, 0)),
                  pl.BlockSpec((1, 1, d, d_exp), lambda i, e: (layer, e[i], 0, 0)),
                  pl.BlockSpec((1, 1, d, d_exp), lambda i, e: (layer, e[i], 0, 0)),
                  pl.BlockSpec((1, 1, d_exp, d), lambda i, e: (layer, e[i], 0, 0))],
        out_specs=pl.BlockSpec((MOE_ROWS, d), lambda i, e: (i, 0)),
        scratch_shapes=[pltpu.VMEM((d, d_exp), BF16), pltpu.VMEM((d, d_exp), BF16),
                        pltpu.VMEM((d_exp, d), BF16)],
    )
    return pl.pallas_call(
        _moe_kernel,
        grid_spec=grid_spec,
        out_shape=jax.ShapeDtypeStruct((n_rows, d), F32),
        compiler_params=_cparams("arbitrary"),
        name="moe_experts",
    )(blk_e, xs, w1, w3, w2)


def _comb_kernel(d_ref, d_next_ref, x_ref, rw_ref, gt_ref, y_hbm, g_ref, o_ref, ybuf, sem, *, tm, final):
    i = pl.program_id(0)
    n = pl.num_programs(0)
    slot = i % 2

    def start_all(idx_ref, s):
        def body(g, carry):
            for u in range(ROW_UNROLL):
                r = g * ROW_UNROLL + u
                pltpu.make_async_copy(y_hbm.at[pl.ds(idx_ref[0, 0, r], 1), :], ybuf.at[s, pl.ds(r, 1), :],
                                      sem.at[s]).start(priority=u % 2)
            return carry
        lax.fori_loop(0, 2 * tm // ROW_UNROLL, body, 0)

    @pl.when(i == 0)
    def _():
        start_all(d_ref, 0)

    @pl.when(i + 1 < n)
    def _():
        start_all(d_next_ref, 1 - slot)

    pltpu.make_async_copy(y_hbm.at[pl.ds(0, 2 * tm), :], ybuf.at[slot], sem.at[slot]).wait()

    w = rw_ref[...]
    moe = w[:, 0:1] * ybuf[slot, 0:tm, :] + w[:, 1:2] * ybuf[slot, tm:2 * tm, :]
    x2 = x_ref[...] + (1.0 + gt_ref[0, 0]) * moe
    o_ref[...] = _rmsnorm(x2, g_ref[...]) if final else x2


def _combine(x1, y_rows, dest, rw, mod_l, g_final, seq, final):
    n_tok, d = x1.shape
    tm = _row_tile(seq, 512)
    per_b = seq // tm
    nblk = n_tok // tm
    idx = _tile_rows(dest, tm)
    return pl.pallas_call(
        functools.partial(_comb_kernel, tm=tm, final=final),
        grid=(nblk,),
        in_specs=[pl.BlockSpec((1, 1, 2 * tm), lambda i: (i, 0, 0), memory_space=pltpu.SMEM),
                  pl.BlockSpec((1, 1, 2 * tm), lambda i: (jnp.minimum(i + 1, nblk - 1), 0, 0),
                               memory_space=pltpu.SMEM),
                  pl.BlockSpec((tm, d), lambda i: (i, 0)),
                  pl.BlockSpec((tm, LANES), lambda i: (i, 0)),
                  pl.BlockSpec((1, 1, 1, d), lambda i: (i // per_b, 5, 0, 0)),
                  pl.BlockSpec(memory_space=pl.ANY),
                  pl.BlockSpec((1, d), lambda i: (0, 0))],
        out_specs=pl.BlockSpec((tm, d), lambda i: (i, 0)),
        out_shape=jax.ShapeDtypeStruct((n_tok, d), F32),
        scratch_shapes=[pltpu.VMEM((2, 2 * tm, d), F32), pltpu.SemaphoreType.DMA((2,))],
        compiler_params=_cparams("arbitrary"),
        name="moe_combine",
    )(idx, idx, x1, rw, mod_l, y_rows, g_final)


def kernel(x, c, ada_w, ada_b, norm_mix, norm_ffn, norm_final, w_in, w_out, mlstm_conv_w, mlstm_conv_b, mlstm_w_q, mlstm_w_k, mlstm_b_i, mlstm_b_f, mlstm_norm_w, rwkv_mu, rwkv_w0, rwkv_w_up, rwkv_a0, rwkv_a_up, rwkv_g_up, rwkv_k_k, rwkv_k_a, rwkv_r_k, rwkv_ln_w, rwkv_ln_b, rwkv_v0, rwkv_v_dn, rwkv_v_up, s5_a_re, s5_a_im, s5_log_dt, s5_b_re, s5_b_im, s5_c_re, s5_c_im, s5_d, s5_glu_w, s5_glu_b, moe_w_rg, moe_b_rg, moe_w_re, moe_b_re, moe_w1, moe_w3, moe_w2):
    bsz, seq, d = x.shape
    depth = ada_w.shape[0]
    n_tok = bsz * seq
    heads_a = mlstm_w_q.shape[1]
    da = heads_a * HEAD_DIM
    db = rwkv_w0.shape[1]
    dc = s5_d.shape[1]
    rw_cols = rwkv_mu.shape[1]
    assert seq % CHUNK == 0 and w_in.shape[2] == 3 * da + 2 * heads_a + rw_cols + dc
    assert 2 * HEAD_DIM == LANES and CHUNK == HEAD_DIM and EXPERTS_PER_GROUP == 8
    widths = (3 * da, 2 * LANES, rw_cols, dc)

    mod = _modulation(c, ada_w, ada_b).reshape(depth, bsz, 6, 1, d)
    xc = x.reshape(n_tok, d)
    v_first = None
    for l in range(depth):
        mod_l = mod[l]
        qkvo, gates, pcols, u = _in_proj(xc, mod_l, norm_mix[l].reshape(1, d), w_in, l, widths, da,
                                         heads_a, seq)

        ya = _mlstm(qkvo, gates, mlstm_conv_w[l], mlstm_conv_b[l], mlstm_w_q[l], mlstm_w_k[l],
                    mlstm_b_i[l], mlstm_b_f[l], mlstm_norm_w[l], bsz, seq)
        rprm = dict(mu=rwkv_mu[l], w0=rwkv_w0[l], w_up=rwkv_w_up[l], a0=rwkv_a0[l], a_up=rwkv_a_up[l],
                    g_up=rwkv_g_up[l], k_k=rwkv_k_k[l], k_a=rwkv_k_a[l], r_k=rwkv_r_k[l],
                    ln_w=rwkv_ln_w[l], ln_b=rwkv_ln_b[l])
        if l > 0:
            rprm.update(v0=rwkv_v0[l - 1], v_dn=rwkv_v_dn[l - 1], v_up=rwkv_v_up[l - 1])
        yb, v_first = _rwkv(pcols, v_first if l > 0 else None, rprm, bsz, seq)
        sprm = dict(a_re=s5_a_re[l], a_im=s5_a_im[l], log_dt=s5_log_dt[l], b_re=s5_b_re[l],
                    b_im=s5_b_im[l], c_re=s5_c_re[l], c_im=s5_c_im[l])
        yc = _s5_core(u, sprm, bsz, seq)

        x1, h2, ri, rw, hist = _out_proj(ya, yb, yc, u, xc, mod_l, norm_ffn[l].reshape(1, d), s5_d[l],
                                         s5_glu_w[l], s5_glu_b[l], w_out[l], moe_w_rg[l], moe_b_rg[l],
                                         moe_w_re[l], moe_b_re[l], seq)
        dest, blk_e, pends, pcounts, n_rows = _route(ri, hist, n_tok, n_tok // hist.shape[0])
        y_rows = _moe_experts(_dispatch(h2, dest, pends, pcounts, n_rows, seq), blk_e, moe_w1, moe_w3,
                              moe_w2, l)
        xc = _combine(x1, y_rows, dest, rw, mod_l, norm_final.reshape(1, d), seq, final=(l == depth - 1))
    return xc.reshape(bsz, seq, d).astype(x.dtype)
```

```python
import functools

import jax
import jax.numpy as jnp
from jax import lax
from jax.experimental import pallas as pl
from jax.experimental.pallas import tpu as pltpu

F32 = jnp.float32
BF16 = jnp.bfloat16

HEAD_DIM = 64
CHUNK = 64
CONV_K = 4
S5_GC = 16
S5_P = 64
N_GROUPS = 4
EXPERTS_PER_GROUP = 8
N_EXPERTS = N_GROUPS * EXPERTS_PER_GROUP
NORM_EPS = 1e-6
HEAD_NORM_EPS = 1e-5
RWKV_GN_EPS = 64e-5
L2_EPS = 1e-12
LANES = 128
SUBLANES = 8
MOE_ROWS = 256
VMEM_LIMIT = 56 * 1024 * 1024


def _cparams(*sem):
    return pltpu.CompilerParams(dimension_semantics=sem, vmem_limit_bytes=VMEM_LIMIT)


def _row_tile(n, want):
    t = min(n, want)
    assert n % t == 0
    return t


def _dot(a, b):
    return jnp.dot(a.astype(BF16), b.astype(BF16), preferred_element_type=F32)


def _dot_nt(a, b):
    return lax.dot_general(a.astype(BF16), b.astype(BF16), (((1,), (1,)), ((), ())),
                           preferred_element_type=F32)


def _dot_tn(a, b):
    return lax.dot_general(a.astype(BF16), b.astype(BF16), (((0,), (0,)), ((), ())),
                           preferred_element_type=F32)


def _split(a):
    hi = a.astype(BF16)
    lo = (a - hi.astype(F32)).astype(BF16)
    return hi, lo


def _dot_xa(a, b_exact):
    hi, lo = _split(a)
    return (jnp.dot(hi, b_exact, preferred_element_type=F32)
            + jnp.dot(lo, b_exact, preferred_element_type=F32))


def _dot_xb(a_exact, b):
    hi, lo = _split(b)
    return (jnp.dot(a_exact, hi, preferred_element_type=F32)
            + jnp.dot(a_exact, lo, preferred_element_type=F32))


def _dot_x3(a, b_exact):
    hi = a.astype(BF16)
    r1 = a - hi.astype(F32)
    mid = r1.astype(BF16)
    lo = (r1 - mid.astype(F32)).astype(BF16)
    return (jnp.dot(hi, b_exact, preferred_element_type=F32)
            + jnp.dot(mid, b_exact, preferred_element_type=F32)
            + jnp.dot(lo, b_exact, preferred_element_type=F32))


def _dot3(a, b):
    ah, al = _split(a)
    bh, bl = _split(b)
    return (jnp.dot(ah, bh, preferred_element_type=F32)
            + jnp.dot(ah, bl, preferred_element_type=F32)
            + jnp.dot(al, bh, preferred_element_type=F32))


def _dot3_nt(a, b):
    ah, al = _split(a)
    bh, bl = _split(b)
    dn = (((1,), (1,)), ((), ()))
    return (lax.dot_general(ah, bh, dn, preferred_element_type=F32)
            + lax.dot_general(ah, bl, dn, preferred_element_type=F32)
            + lax.dot_general(al, bh, dn, preferred_element_type=F32))


def _sigmoid(x):
    return 1.0 / (1.0 + jnp.exp(-x))


def _silu(x):
    return x * _sigmoid(x)


def _log_sigmoid(x):
    return jnp.minimum(x, 0.0) - jnp.log1p(jnp.exp(-jnp.abs(x)))


def _rmsnorm(x, g):
    ms = jnp.mean(x * x, axis=-1, keepdims=True)
    return x * lax.rsqrt(ms + NORM_EPS) * g


def _tri_incl(n):
    r = lax.broadcasted_iota(jnp.int32, (n, n), 0)
    c = lax.broadcasted_iota(jnp.int32, (n, n), 1)
    return (c <= r).astype(BF16)


def _head_ones(width):
    r = lax.broadcasted_iota(jnp.int32, (width, width), 0) // HEAD_DIM
    c = lax.broadcasted_iota(jnp.int32, (width, width), 1) // HEAD_DIM
    return (r == c).astype(BF16)


def _mod_kernel(c_ref, w_ref, b_ref, o_ref):
    o_ref[0] = _dot(_silu(c_ref[...]), w_ref[0]) + b_ref[0]


def _modulation(c, ada_w, ada_b):
    depth, d, d6 = ada_w.shape
    bsz = c.shape[0]
    tn = _row_tile(d6, 1024)
    return pl.pallas_call(
        _mod_kernel,
        grid=(depth, d6 // tn),
        in_specs=[pl.BlockSpec((bsz, d), lambda l, j: (0, 0)),
                  pl.BlockSpec((1, d, tn), lambda l, j: (l, 0, j)),
                  pl.BlockSpec((1, 1, tn), lambda l, j: (l, 0, j))],
        out_specs=pl.BlockSpec((1, bsz, tn), lambda l, j: (l, 0, j)),
        out_shape=jax.ShapeDtypeStruct((depth, bsz, d6), F32),
        compiler_params=_cparams("parallel", "parallel"),
        name="adaln_mod",
    )(c, ada_w, ada_b.reshape(depth, 1, d6))


def _in_kernel(x_ref, sh_ref, sc_ref, g_ref, w_ref, *refs, widths, da, heads):
    out_refs, w_sc = refs[:-1], refs[-1]
    d = w_ref.shape[1]
    rw_cols, dc = widths[2], widths[3]

    @pl.when(pl.program_id(0) == 0)
    def _():
        rb = 256
        g0 = 3 * da
        r_src = g0 + 2 * heads
        u_src = r_src + rw_cols
        r_win = -(-(2 * heads + rw_cols) // LANES) * LANES
        u_al = u_src // LANES * LANES
        lane = lax.broadcasted_iota(jnp.int32, (rb, LANES), 1)
        for r0 in range(0, d, rb):
            rs = slice(r0, r0 + rb)
            w_sc[rs, 0:g0] = w_ref[0, rs, 0:g0].astype(BF16)
            gblk = w_ref[0, rs, g0:g0 + LANES]
            w_sc[rs, g0:g0 + LANES] = jnp.where(lane < heads, gblk, 0.0).astype(BF16)
            w_sc[rs, g0 + LANES:g0 + 2 * LANES] = jnp.where(
                lane < heads, pltpu.roll(gblk, LANES - heads, 1), 0.0).astype(BF16)
            win = w_ref[0, rs, g0:g0 + r_win]
            w_sc[rs, g0 + 2 * LANES:g0 + 2 * LANES + rw_cols] = win[:, 2 * heads:2 * heads + rw_cols].astype(BF16)
            win2 = w_ref[0, rs, u_al:u_src + dc]
            w_sc[rs, g0 + 2 * LANES + rw_cols:] = win2[:, u_src - u_al:u_src - u_al + dc].astype(BF16)

    h = _rmsnorm(x_ref[...], g_ref[...]) * (1.0 + sc_ref[0, 0]) + sh_ref[0, 0]
    hb = h.astype(BF16)
    off = 0
    for o_ref, wd in zip(out_refs, widths):
        o_ref[...] = jnp.dot(hb, w_sc[:, off:off + wd], preferred_element_type=F32)
        off += wd


def _in_proj(x2d, mod_l, g, w_in, layer, widths, da, heads, seq):
    n_tok, d = x2d.shape
    cols = w_in.shape[2]
    tm = _row_tile(seq, 512)
    per_b = seq // tm
    return pl.pallas_call(
        functools.partial(_in_kernel, widths=widths, da=da, heads=heads),
        grid=(n_tok // tm,),
        in_specs=[pl.BlockSpec((tm, d), lambda i: (i, 0)),
                  pl.BlockSpec((1, 1, 1, d), lambda i: (i // per_b, 0, 0, 0)),
                  pl.BlockSpec((1, 1, 1, d), lambda i: (i // per_b, 1, 0, 0)),
                  pl.BlockSpec((1, d), lambda i: (0, 0)),
                  pl.BlockSpec((1, d, cols), lambda i: (layer, 0, 0))],
        out_specs=[pl.BlockSpec((tm, wd), lambda i: (i, 0)) for wd in widths],
        out_shape=[jax.ShapeDtypeStruct((n_tok, wd), F32) for wd in widths],
        scratch_shapes=[pltpu.VMEM((d, sum(widths)), BF16)],
        compiler_params=_cparams("arbitrary"),
        name="in_proj",
    )(x2d, mod_l, mod_l, g, w_in)


def _mlstm_kernel(qkvo_ref, gate_ref, cw_ref, cb_ref, wq_ref, wk_ref, gb_ref, nw_ref, out_ref,
                  xf_sc, q_sc, k_sc, gi_sc, gf_sc, cn_sc, m_sc, *, nb, tb, heads):
    dh, L = HEAD_DIM, CHUNK
    da = heads * dh
    i = pl.program_id(0)

    @pl.when(i == 0)
    def _():
        xf_sc[:, 0:SUBLANES, :] = jnp.zeros((nb, SUBLANES, da), F32)
        cn_sc[...] = jnp.zeros_like(cn_sc)
        m_sc[...] = jnp.zeros_like(m_sc)

    for b in range(nb):
        xqk = qkvo_ref[b, :, 0:da]
        xf_sc[b, pl.ds(SUBLANES, tb), :] = xqk
        acc = xqk * cw_ref[CONV_K - 1:CONV_K, :] + cb_ref[...]
        for j in range(1, CONV_K):
            acc = acc + xf_sc[b, pl.ds(SUBLANES - j, tb), :] * cw_ref[CONV_K - 1 - j:CONV_K - j, :]
        xf_sc[b, 0:SUBLANES, :] = xf_sc[b, pl.ds(tb, SUBLANES), :]
        cx = _silu(acc).astype(BF16)
        q_sc[b] = jnp.dot(cx, wq_ref[...], preferred_element_type=F32)
        k_sc[b] = jnp.dot(cx, wk_ref[...], preferred_element_type=F32) * (dh ** -0.5)

        g = gate_ref[b] + gb_ref[...]
        gi_sc[b] = g[:, :LANES]
        gf_sc[b] = _log_sigmoid(g[:, LANES:])

    tri = _tri_incl(L)
    rr = lax.broadcasted_iota(jnp.int32, (L, L), 0)
    cc = lax.broadcasted_iota(jnp.int32, (L, L), 1)
    causal = cc <= rr
    row_l = lax.broadcasted_iota(jnp.int32, (L, LANES), 0)
    sel = (lax.broadcasted_iota(jnp.int32, (LANES, heads * LANES), 0)
           == lax.broadcasted_iota(jnp.int32, (LANES, heads * LANES), 1) // LANES).astype(BF16)
    ones_v = jnp.ones((L, dh), F32)
    mean_m = jnp.full((dh, dh), 1.0 / dh, BF16)

    units = [(b, h) for b in range(nb) for h in range(heads)]
    ur = range(len(units))

    def hcol(h, base=0):
        return slice(base + h * dh, base + (h + 1) * dh)

    def chunk(c, carry):
        rows = pl.ds(pl.multiple_of(c * L, L), L)
        ex, g_t = [], []
        for b in range(nb):
            bc = _dot_xb(tri, gf_sc[b, rows, :])
            gtot = bc[L - 1:L, :]
            g = gi_sc[b, rows, :] - bc
            gmax = jnp.max(g, axis=0, keepdims=True)
            m_prev = m_sc[b]
            m_loc = gtot + gmax
            m_new = jnp.maximum(gtot + m_prev, m_loc)
            a_old = jnp.exp(gtot + m_prev - m_new)
            a_loc = jnp.exp(m_loc - m_new)
            m_sc[b] = m_new
            pm = g
            d = 1
            while d < L:
                pm = jnp.maximum(pm, jnp.where(row_l >= d, pltpu.roll(pm, d, 0), -jnp.inf))
                d *= 2
            mm = jnp.maximum(m_prev, pm)
            stack = jnp.concatenate([jnp.exp(g - gmax), mm, jnp.exp(m_prev - mm), jnp.exp(-(bc + mm)),
                                     a_old, a_loc, jnp.zeros((SUBLANES - 2, LANES), F32)], axis=0)
            ex.append(_dot_x3(stack, sel))
            g_t.append(g.T)

        def part(k, b, h, width=dh):
            return ex[b][k * L:(k + 1) * L, h * LANES:h * LANES + width]

        qc = [q_sc[b, rows, hcol(h)] for b, h in units]
        kc = [k_sc[b, rows, hcol(h)] for b, h in units]
        vo = [jnp.concatenate([qkvo_ref[b, rows, hcol(h, da)], ones_v], axis=1) for b, h in units]
        oc = [qkvo_ref[b, rows, hcol(h, 2 * da)] for b, h in units]
        cn_prev = [cn_sc[b, h] for b, h in units]
        s_raw = [_dot_nt(qc[u], kc[u]) for u in ur]
        q_cn = [_dot(qc[u], cn_prev[u]) for u in ur]
        cn_loc = [_dot_tn(kc[u] * part(0, b, h), vo[u]) for u, (b, h) in enumerate(units)]
        s_qk = [s_raw[u] * jnp.where(causal, jnp.exp(g_t[b][h:h + 1, :] - part(1, b, h)), 0.0)
                for u, (b, h) in enumerate(units)]
        s_vn = [_dot(s_qk[u], vo[u]) for u in ur]
        nd = [part(2, b, h, 2 * dh) * q_cn[u] + s_vn[u] for u, (b, h) in enumerate(units)]
        hh = [nd[u][:, :dh] / jnp.maximum(jnp.abs(nd[u][:, dh:]), part(3, b, h))
              for u, (b, h) in enumerate(units)]
        mu = [_dot_xa(hh[u], mean_m) for u in ur]
        dlt = [hh[u] - mu[u] for u in ur]
        var = [_dot_xa(dlt[u] * dlt[u], mean_m) for u in ur]
        outs = [dlt[u] * lax.rsqrt(var[u] + HEAD_NORM_EPS) * _sigmoid(oc[u]) for u in ur]
        for b in range(nb):
            mine = [u for u in ur if units[u][0] == b]
            out_ref[b, rows, :] = jnp.concatenate([outs[u] for u in mine], axis=1) * nw_ref[...]
        for u, (b, h) in enumerate(units):
            cn_sc[b, h] = (ex[b][4 * L:4 * L + 1, h * LANES:(h + 1) * LANES] * cn_prev[u]
                           + ex[b][4 * L + 1:4 * L + 2, h * LANES:(h + 1) * LANES] * cn_loc[u])
        return carry

    lax.fori_loop(0, tb // L, chunk, 0)


def _block_diag(w):
    heads, dh, _ = w.shape
    eye = jnp.eye(heads, dtype=w.dtype)
    return (eye[:, None, :, None] * w[:, :, None, :]).reshape(heads * dh, heads * dh)


def _mlstm(qkvo, gates, conv_w, conv_b, w_q, w_k, b_i, b_f, norm_w, bsz, seq):
    heads = w_q.shape[0]
    da = heads * HEAD_DIM
    tb = _row_tile(seq, 256)
    nblk = seq // tb
    gbias = jnp.zeros((1, 2 * LANES), F32).at[0, :heads].set(b_i).at[0, LANES:LANES + heads].set(b_f)
    kern = functools.partial(_mlstm_kernel, nb=bsz, tb=tb, heads=heads)
    row = lambda i: (0, i, 0)
    const = lambda i: (0, 0)
    out = pl.pallas_call(
        kern,
        grid=(nblk,),
        in_specs=[pl.BlockSpec((bsz, tb, 3 * da), row),
                  pl.BlockSpec((bsz, tb, 2 * LANES), row),
                  pl.BlockSpec((CONV_K, da), const),
                  pl.BlockSpec((1, da), const),
                  pl.BlockSpec((da, da), const),
                  pl.BlockSpec((da, da), const),
                  pl.BlockSpec((1, 2 * LANES), const),
                  pl.BlockSpec((1, da), const)],
        out_specs=pl.BlockSpec((bsz, tb, da), row),
        out_shape=jax.ShapeDtypeStruct((bsz, seq, da), F32),
        scratch_shapes=[pltpu.VMEM((bsz, tb + SUBLANES, da), F32),
                        pltpu.VMEM((bsz, tb, da), F32),
                        pltpu.VMEM((bsz, tb, da), F32),
                        pltpu.VMEM((bsz, tb, LANES), F32),
                        pltpu.VMEM((bsz, tb, LANES), F32),
                        pltpu.VMEM((bsz, heads, HEAD_DIM, 2 * HEAD_DIM), F32),
                        pltpu.VMEM((bsz, 1, LANES), F32)],
        compiler_params=_cparams("arbitrary"),
        name="mlstm",
    )(qkvo.reshape(bsz, seq, 3 * da), gates.reshape(bsz, seq, 2 * LANES), conv_w, conv_b.reshape(1, da),
      _block_diag(w_q).astype(BF16), _block_diag(w_k).astype(BF16), gbias, norm_w.reshape(1, da))
    return out.reshape(bsz * seq, da)


def _rwkv_kernel(*refs, nb, tb, heads, lw_dim, la_dim, lg_dim, has_vres):
    dh, L = HEAD_DIM, CHUNK
    db = heads * dh
    it = iter(refs)
    p_ref = next(it)
    vf_ref = next(it) if has_vres else None
    (mu_ref, w0_ref, wup_ref, a0_ref, aup_ref, gup_ref, kk_ref, ka_ref, rk_ref,
     lnw_ref, lnb_ref) = (next(it) for _ in range(11))
    if has_vres:
        v0_ref, vdn_ref, vup_ref = (next(it) for _ in range(3))
    y_ref = next(it)
    vout_ref = None if has_vres else next(it)
    (xf_sc, r_sc, k_sc, v_sc, a_sc, b_sc, lw_sc, y_sc, gate_sc, rp_sc, st_sc, q_sc, z_sc,
     gl_sc) = (next(it) for _ in range(14))

    i = pl.program_id(0)

    @pl.when(i == 0)
    def _():
        xf_sc[:, 0:SUBLANES, :] = jnp.zeros((nb, SUBLANES, xf_sc.shape[2]), F32)
        st_sc[...] = jnp.zeros_like(st_sc)

    ones_h = _head_ones(db)
    for b in range(nb):
        p = p_ref[b]
        xf_sc[b, pl.ds(SUBLANES, tb), :] = p
        prev = xf_sc[b, pl.ds(SUBLANES - 1, tb), :]
        xf_sc[b, 0:SUBLANES, :] = xf_sc[b, pl.ds(tb, SUBLANES), :]
        p = p + mu_ref[...] * (prev - p)

        o = 0
        r = p[:, o:o + db]; o += db
        k = p[:, o:o + db]; o += db
        v = p[:, o:o + db]; o += db
        wd = p[:, o:o + lw_dim]; o += lw_dim
        ad = p[:, o:o + la_dim]; o += la_dim
        gd = p[:, o:o + lg_dim]

        wlog = _log_sigmoid(w0_ref[...] + _dot(jnp.tanh(wd), wup_ref[...])) - 0.5
        lw_sc[b] = -jnp.exp(wlog)
        a = _sigmoid(a0_ref[...] + _dot(ad, aup_ref[...]))
        gate_sc[b] = _dot(_sigmoid(gd), gup_ref[...])
        if has_vres:
            v = v + (vf_ref[b] - v) * _sigmoid(v0_ref[...] + _dot(_dot(v, vdn_ref[...]), vup_ref[...]))
        else:
            vout_ref[b] = v
        kk = k * kk_ref[...]
        kk = kk / jnp.maximum(jnp.sqrt(_dot_xa(kk * kk, ones_h)), L2_EPS)
        r_sc[b] = r
        k_sc[b] = k * (1.0 + (a - 1.0) * ka_ref[...])
        v_sc[b] = v
        a_sc[b] = -kk
        b_sc[b] = kk * a

    tri = _tri_incl(L)
    rr = lax.broadcasted_iota(jnp.int32, (L, L), 0)
    cc = lax.broadcasted_iota(jnp.int32, (L, L), 1)
    strict = cc < rr
    incl = cc <= rr

    units = [(b, h) for b in range(nb) for h in range(heads)]
    ur = range(len(units))

    def hcol(h):
        return slice(h * dh, (h + 1) * dh)

    def chunk(c, carry):
        rows = pl.ds(pl.multiple_of(c * L, L), L)
        at, rt, bt, kt, bg, kg, vch = ([] for _ in range(7))
        for b in range(nb):
            lwc = lw_sc[b, rows, :]
            cum = _dot_xb(tri, lwc)
            cum_l = cum[L - 1:L, :]
            e_inv = jnp.exp(-cum)
            e_end = jnp.exp(cum_l - cum)
            bv = b_sc[b, rows, :]
            kv = k_sc[b, rows, :]
            at.append(a_sc[b, rows, :] * jnp.exp(cum - lwc))
            rt.append(r_sc[b, rows, :] * jnp.exp(cum))
            bt.append(bv * e_inv)
            kt.append(kv * e_inv)
            bg.append(bv * e_end)
            kg.append(kv * e_end)
            vch.append(v_sc[b, rows, :])
            gl_sc[c, b] = jnp.exp(cum_l)
        vh = [vch[b][:, hcol(h)] for b, h in units]
        g4 = [_dot_nt(jnp.concatenate([at[b][:, hcol(h)], rt[b][:, hcol(h)]], axis=0),
                      jnp.concatenate([bt[b][:, hcol(h)], kt[b][:, hcol(h)]], axis=0))
              for b, h in units]
        pw = [jnp.where(strict, g4[u][:L, :L], 0.0) for u in ur]
        n_ak = [jnp.where(strict, g4[u][:L, L:], 0.0) for u in ur]
        m_rb = [jnp.where(incl, g4[u][L:, :L], 0.0) for u in ur]
        m_rk = [jnp.where(incl, g4[u][L:, L:], 0.0) for u in ur]
        x = [jnp.concatenate([at[b][:, hcol(h)], _dot(n_ak[u], vh[u])], axis=1)
             for u, (b, h) in enumerate(units)]
        for step in range(6):
            x = [x[u] + _dot(pw[u], x[u]) for u in ur]
            if step < 5:
                pw = [_dot(pw[u], pw[u]) for u in ur]
        ry = [jnp.concatenate([rt[b][:, hcol(h)], _dot(m_rk[u], vh[u])], axis=1) + _dot(m_rb[u], x[u])
              for u, (b, h) in enumerate(units)]
        qz = [_dot_tn(x[u], bg[b][:, hcol(h)]) for u, (b, h) in enumerate(units)]
        z2 = [_dot_tn(vh[u], kg[b][:, hcol(h)]) for u, (b, h) in enumerate(units)]
        for b in range(nb):
            mine = [u for u in ur if units[u][0] == b]
            rp_sc[b, rows, :] = jnp.concatenate([ry[u][:, :dh] for u in mine], axis=1)
            y_sc[b, rows, :] = jnp.concatenate([ry[u][:, dh:] for u in mine], axis=1)
        for u, (b, h) in enumerate(units):
            q_sc[c, b, h] = qz[u][:dh]
            z_sc[c, b, h] = qz[u][dh:] + z2[u]
        return carry

    lax.fori_loop(0, tb // L, chunk, 0)

    def carry_state(c, carry):
        rows = pl.ds(pl.multiple_of(c * L, L), L)
        st = [st_sc[b, h] for b, h in units]
        ys = [_dot_nt(rp_sc[b, rows, hcol(h)], st[u]) for u, (b, h) in enumerate(units)]
        sq = [_dot_xa(st[u], q_sc[c, b, h].astype(BF16)) for u, (b, h) in enumerate(units)]
        for b in range(nb):
            mine = [u for u in ur if units[u][0] == b]
            y_sc[b, rows, :] = y_sc[b, rows, :] + jnp.concatenate([ys[u] for u in mine], axis=1)
        for u, (b, h) in enumerate(units):
            st_sc[b, h] = st[u] * gl_sc[c, b][:, hcol(h)] + sq[u] + z_sc[c, b, h]
        return carry

    lax.fori_loop(0, tb // L, carry_state, 0)

    for b in range(nb):
        y = y_sc[b]
        mean = _dot_xa(y, ones_h) * (1.0 / dh)
        dlt = y - mean
        var = _dot_xa(dlt * dlt, ones_h) * (1.0 / dh)
        yn = dlt * lax.rsqrt(var + RWKV_GN_EPS) * lnw_ref[...] + lnb_ref[...]
        bonus = _dot_xa(r_sc[b] * k_sc[b] * rk_ref[...], ones_h) * v_sc[b]
        y_ref[b] = (yn + bonus) * gate_sc[b]


def _rwkv(pcols, v_first, prm, bsz, seq):
    db = prm["w0"].shape[0]
    heads = db // HEAD_DIM
    cols = pcols.shape[1]
    lw_dim, la_dim, lg_dim = prm["w_up"].shape[0], prm["a_up"].shape[0], prm["g_up"].shape[0]
    has_vres = v_first is not None
    tb = _row_tile(seq, 256)
    nblk = seq // tb
    row = lambda i: (0, i, 0)
    const = lambda i: (0, 0)
    vec = lambda a: a.reshape(1, -1).astype(F32)

    args = [pcols.reshape(bsz, seq, cols)]
    specs = [pl.BlockSpec((bsz, tb, cols), row)]
    if has_vres:
        args.append(v_first.reshape(bsz, seq, db))
        specs.append(pl.BlockSpec((bsz, tb, db), row))
    small = [vec(prm["mu"]), vec(prm["w0"]), prm["w_up"].astype(BF16), vec(prm["a0"]),
             prm["a_up"].astype(BF16), prm["g_up"].astype(BF16), vec(prm["k_k"]), vec(prm["k_a"]),
             vec(prm["r_k"]), vec(prm["ln_w"]), vec(prm["ln_b"])]
    if has_vres:
        lv = prm["v_dn"].shape[1]
        lvp = -(-lv // LANES) * LANES
        v_dn = jnp.zeros((db, lvp), F32).at[:, :lv].set(prm["v_dn"]).astype(BF16)
        v_up = jnp.zeros((lvp, db), F32).at[:lv, :].set(prm["v_up"]).astype(BF16)
        small += [vec(prm["v0"]), v_dn, v_up]
    args += small
    specs += [pl.BlockSpec(a.shape, const) for a in small]

    out_shape = [jax.ShapeDtypeStruct((bsz, seq, db), F32)]
    out_specs = [pl.BlockSpec((bsz, tb, db), row)]
    if not has_vres:
        out_shape.append(jax.ShapeDtypeStruct((bsz, seq, db), F32))
        out_specs.append(pl.BlockSpec((bsz, tb, db), row))

    kern = functools.partial(_rwkv_kernel, nb=bsz, tb=tb, heads=heads, lw_dim=lw_dim, la_dim=la_dim,
                             lg_dim=lg_dim, has_vres=has_vres)
    nch = tb // CHUNK
    res = pl.pallas_call(
        kern,
        grid=(nblk,),
        in_specs=specs,
        out_specs=out_specs,
        out_shape=out_shape,
        scratch_shapes=[pltpu.VMEM((bsz, tb + SUBLANES, cols), F32)]
        + [pltpu.VMEM((bsz, tb, db), F32) for _ in range(9)]
        + [pltpu.VMEM((bsz, heads, HEAD_DIM, HEAD_DIM), F32),
           pltpu.VMEM((nch, bsz, heads, HEAD_DIM, HEAD_DIM), F32),
           pltpu.VMEM((nch, bsz, heads, HEAD_DIM, HEAD_DIM), F32),
           pltpu.VMEM((nch, bsz, 1, db), F32)],
        compiler_params=_cparams("arbitrary"),
        name="rwkv7",
    )(*args)
    res = [a.reshape(bsz * seq, db) for a in res]
    return (res[0], v_first) if has_vres else (res[0], res[1])


def _cmul(ar, ai, br, bi):
    return ar * br - ai * bi, ar * bi + ai * br


def _shift_rows(x, d):
    row = lax.broadcasted_iota(jnp.int32, x.shape, 0)
    return jnp.where(row >= d, pltpu.roll(x, d, 0), 0.0)


def _s5_kernel(u_ref, are_r, aim_r, dt_r, are_c, aim_c, dt_c, bre_ref, bim_ref, cre_ref, cim_ref,
               y_ref, m_sc, *, bsz, nchunk):
    L, gc, P = CHUNK, S5_GC, S5_P
    n = L * gc
    a_re, a_im, dt = are_r[0], aim_r[0], jnp.exp(dt_r[0])
    mag, ang = jnp.exp(a_re * dt), a_im * dt
    ab_re, ab_im = mag * jnp.cos(ang), mag * jnp.sin(ang)
    inv = 1.0 / (a_re * a_re + a_im * a_im)
    co_re = ((ab_re - 1.0) * a_re + ab_im * a_im) * inv
    co_im = (ab_im * a_re - (ab_re - 1.0) * a_im) * inv
    bb_re, bb_im = _cmul(co_re, co_im, bre_ref[0], bim_ref[0])
    c_re, c_im = cre_ref[0], cim_ref[0]

    def powers(tau):
        m = jnp.exp(tau * (a_re * dt))
        return m * jnp.cos(tau * ang), m * jnp.sin(tau * ang)

    a_re_c, a_im_c, dt_c_ = are_c[0], aim_c[0], jnp.exp(dt_c[0])
    tau_row = lax.broadcasted_iota(jnp.int32, (P, L), 1).astype(F32)
    pm = jnp.exp(tau_row * (a_re_c * dt_c_))
    pt_re = pm * jnp.cos(tau_row * (a_im_c * dt_c_))
    pt_im = pm * jnp.sin(tau_row * (a_im_c * dt_c_))
    pair = lax.broadcasted_iota(jnp.int32, (gc * gc, gc), 0)
    col = lax.broadcasted_iota(jnp.int32, (gc * gc, gc), 1)
    rep_c = (pair // gc == col).astype(BF16)
    rep_b = (pair % gc == col).astype(BF16)
    cb_re, cb_im = _cmul(_dot_xb(rep_c, c_re), _dot_xb(rep_c, c_im),
                         _dot_xb(rep_b, bb_re), _dot_xb(rep_b, bb_im))
    kap = _dot3(cb_re, pt_re) - _dot3(cb_im, pt_im)

    kap_pad = jnp.concatenate([kap, jnp.zeros_like(kap)], axis=1)
    srow = lax.broadcasted_iota(jnp.int32, (L, LANES), 0)
    tcol = lax.broadcasted_iota(jnp.int32, (L, LANES), 1)
    for cp in range(gc):
        for c2 in range(0, gc, 2):
            k0 = jnp.broadcast_to(kap_pad[c2 * gc + cp:c2 * gc + cp + 1, :], (L, LANES))
            k1 = jnp.broadcast_to(kap_pad[(c2 + 1) * gc + cp:(c2 + 1) * gc + cp + 1, :], (L, LANES))
            t0 = pltpu.roll(k0, 0, 1, stride=1, stride_axis=0)
            t1 = pltpu.roll(k1, L, 1, stride=1, stride_axis=0)
            blk = jnp.where(tcol < L, jnp.where(tcol >= srow, t0, 0.0),
                            jnp.where(tcol - L >= srow, t1, 0.0))
            m_sc[cp * L:(cp + 1) * L, c2 * L:(c2 + 2) * L] = blk.astype(BF16)

    s_col = lax.broadcasted_iota(jnp.int32, (L, P), 0).astype(F32)
    pw_re, pw_im = powers((L - 1.0) - s_col)
    pg_re, pg_im = powers(s_col + 1.0)
    w_re, w_im, g_re, g_im = [], [], [], []
    for c in range(gc):
        br = jnp.broadcast_to(bb_re[c:c + 1, :], (L, P))
        bi = jnp.broadcast_to(bb_im[c:c + 1, :], (L, P))
        wr, wi = _cmul(br, bi, pw_re, pw_im)
        w_re.append(wr); w_im.append(wi)
        cr = jnp.broadcast_to(c_re[c:c + 1, :], (L, P))
        ci = jnp.broadcast_to(c_im[c:c + 1, :], (L, P))
        gr, gi = _cmul(cr, ci, pg_re, pg_im)
        g_re.append(gr); g_im.append(gi)
    w_re, w_im = jnp.concatenate(w_re, axis=0), jnp.concatenate(w_im, axis=0)
    g_re, g_im = jnp.concatenate(g_re, axis=0), jnp.concatenate(g_im, axis=0)

    u = u_ref[0].astype(BF16)
    x_re = jnp.dot(u, w_re.astype(BF16), preferred_element_type=F32)
    x_im = jnp.dot(u, w_im.astype(BF16), preferred_element_type=F32)
    xs_re, xs_im = [], []
    for b in range(bsz):
        xr = x_re[b * nchunk:(b + 1) * nchunk]
        xi = x_im[b * nchunk:(b + 1) * nchunk]
        d = 1
        while d < nchunk:
            ar_, ai_ = powers(float(L * d))
            sr, si = _cmul(ar_, ai_, _shift_rows(xr, d), _shift_rows(xi, d))
            xr, xi = xr + sr, xi + si
            d *= 2
        xs_re.append(_shift_rows(xr, 1))
        xs_im.append(_shift_rows(xi, 1))
    xs_re, xs_im = jnp.concatenate(xs_re, axis=0), jnp.concatenate(xs_im, axis=0)
    y = jnp.dot(u, m_sc[...], preferred_element_type=F32)
    y = y + _dot_nt(xs_re, g_re) - _dot_nt(xs_im, g_im)
    y_ref[0] = y


def _s5_core(u2d, prm, bsz, seq):
    L, gc, P = CHUNK, S5_GC, S5_P
    groups = u2d.shape[1] // gc
    nchunk = seq // L
    rows = bsz * nchunk
    n = L * gc
    ug = u2d.reshape(bsz, nchunk, L, groups, gc).transpose(3, 0, 1, 4, 2).reshape(groups, rows, n)
    row3 = lambda a: a.reshape(groups, 1, P).astype(F32)
    col3 = lambda a: a.reshape(groups, P, 1).astype(F32)
    dt_b = jnp.broadcast_to(prm["log_dt"][:, None], (groups, P))
    args = [ug, row3(prm["a_re"]), row3(prm["a_im"]), row3(dt_b),
            col3(prm["a_re"]), col3(prm["a_im"]), col3(dt_b),
            prm["b_re"].transpose(0, 2, 1), prm["b_im"].transpose(0, 2, 1),
            prm["c_re"], prm["c_im"]]
    g3 = lambda g: (g, 0, 0)
    specs = [pl.BlockSpec((1, rows, n), g3)]
    specs += [pl.BlockSpec((1, 1, P), g3)] * 3 + [pl.BlockSpec((1, P, 1), g3)] * 3
    specs += [pl.BlockSpec((1, gc, P), g3)] * 4
    yg = pl.pallas_call(
        functools.partial(_s5_kernel, bsz=bsz, nchunk=nchunk),
        grid=(groups,),
        in_specs=specs,
        out_specs=pl.BlockSpec((1, rows, n), g3),
        out_shape=jax.ShapeDtypeStruct((groups, rows, n), F32),
        scratch_shapes=[pltpu.VMEM((n, n), BF16)],
        compiler_params=_cparams("parallel"),
        name="s5_core",
    )(*args)
    return yg.reshape(groups, bsz, nchunk, gc, L).transpose(1, 2, 4, 0, 3).reshape(bsz * seq, groups * gc)


def _gelu_tanh(x):
    return 0.5 * x * (1.0 + jnp.tanh(0.7978845608028654 * (x + 0.044715 * x * x * x)))


def _out_kernel(ya_ref, yb_ref, yc_ref, u_ref, x_ref, gt1_ref, sh2_ref, sc2_ref, g_ref, d_ref,
                gw_ref, gbias_ref, wo_ref, wrh_ref, wrl_ref, br_ref,
                x1_ref, h2_ref, ri_ref, rw_ref, hist_ref, *, da, db):
    yc = _gelu_tanh(yc_ref[...] + d_ref[...] * u_ref[...])
    yc = yc * _sigmoid(_dot(yc, gw_ref[...]) + gbias_ref[...])
    mixed = (_dot(ya_ref[...], wo_ref[0:da, :]) + _dot(yb_ref[...], wo_ref[da:da + db, :])
             + _dot(yc, wo_ref[da + db:, :]))
    x1 = x_ref[...] + (1.0 + gt1_ref[0, 0]) * mixed
    x1_ref[...] = x1
    h2 = _rmsnorm(x1, g_ref[...]) * (1.0 + sc2_ref[0, 0]) + sh2_ref[0, 0]
    h2_ref[...] = h2

    hh, hl = _split(h2)
    logits = (jnp.dot(hh, wrh_ref[...], preferred_element_type=F32)
              + jnp.dot(hh, wrl_ref[...], preferred_element_type=F32)
              + jnp.dot(hl, wrh_ref[...], preferred_element_type=F32)) + br_ref[...]
    lane_i = lax.broadcasted_iota(jnp.int32, logits.shape, 1)
    lane = lane_i.astype(F32)
    big = float(LANES)
    neg = -jnp.inf
    is_g = lane_i < N_GROUPS
    lg = jnp.where(is_g, logits, neg)
    gmax = jnp.max(lg, axis=1, keepdims=True)
    gi = jnp.min(jnp.where(is_g & (lg == gmax), lane, big), axis=1, keepdims=True)
    gp = 1.0 / jnp.sum(jnp.where(is_g, jnp.exp(lg - gmax), 0.0), axis=1, keepdims=True)
    e_lane = lane_i - N_GROUPS
    grp_of_lane = lax.shift_right_arithmetic(e_lane, 3).astype(F32)
    in_grp = (e_lane >= 0) & (e_lane < N_EXPERTS) & (grp_of_lane == gi)
    l1 = jnp.where(in_grp, logits, neg)
    m1 = jnp.max(l1, axis=1, keepdims=True)
    i1 = jnp.min(jnp.where(in_grp & (l1 == m1), lane, big), axis=1, keepdims=True)
    rest = in_grp & (lane != i1)
    l2 = jnp.where(rest, logits, neg)
    m2 = jnp.max(l2, axis=1, keepdims=True)
    i2 = jnp.min(jnp.where(rest & (l2 == m2), lane, big), axis=1, keepdims=True)
    e2 = jnp.exp(m2 - m1)
    w1 = gp / (1.0 + e2)
    w2 = gp * e2 / (1.0 + e2)
    hot1 = (lane == i1).astype(F32)
    hot2 = (lane == i2).astype(F32)
    both = (hot1 + hot2).astype(BF16)
    tm = logits.shape[0]
    rr = lax.broadcasted_iota(jnp.int32, (tm, tm), 0)
    cc = lax.broadcasted_iota(jnp.int32, (tm, tm), 1)
    before = jnp.dot((cc < rr).astype(BF16), both, preferred_element_type=F32)
    rank1 = jnp.sum(before * hot1, axis=1, keepdims=True)
    rank2 = jnp.sum(before * hot2, axis=1, keepdims=True)
    hist_ref[0] = jnp.sum(hot1 + hot2, axis=0, keepdims=True)
    ids = jnp.where(lane_i == 0, i1, jnp.where(lane_i == 1, i2, jnp.where(lane_i == 2, rank1, rank2)))
    ri_ref[...] = ids.astype(jnp.int32)
    rw_ref[...] = jnp.where(lane_i == 0, w1, jnp.where(lane_i == 1, w2, 0.0))


def _out_proj(ya, yb, yc, u, x2d, mod_l, g_ffn, s5_d, glu_w, glu_b, w_out, w_rg, b_rg, w_re, b_re, seq):
    n_tok, d = x2d.shape
    da, db, dc = ya.shape[1], yb.shape[1], yc.shape[1]
    tm = _row_tile(seq, 512)
    per_b = seq // tm
    wr = jnp.zeros((d, LANES), F32).at[:, :N_GROUPS].set(w_rg).at[:, N_GROUPS:N_GROUPS + N_EXPERTS].set(w_re)
    wr_hi = wr.astype(BF16)
    wr_lo = (wr - wr_hi.astype(F32)).astype(BF16)
    br = jnp.zeros((1, LANES), F32).at[0, :N_GROUPS].set(b_rg).at[0, N_GROUPS:N_GROUPS + N_EXPERTS].set(b_re)
    row = lambda i: (i, 0)
    const = lambda i: (0, 0)
    modspec = lambda j: pl.BlockSpec((1, 1, 1, d), lambda i: (i // per_b, j, 0, 0))
    return pl.pallas_call(
        functools.partial(_out_kernel, da=da, db=db),
        grid=(n_tok // tm,),
        in_specs=[pl.BlockSpec((tm, da), row), pl.BlockSpec((tm, db), row), pl.BlockSpec((tm, dc), row),
                  pl.BlockSpec((tm, dc), row), pl.BlockSpec((tm, d), row),
                  modspec(2), modspec(3), modspec(4),
                  pl.BlockSpec((1, d), const), pl.BlockSpec((1, dc), const),
                  pl.BlockSpec((dc, dc), const), pl.BlockSpec((1, dc), const),
                  pl.BlockSpec((d, d), const), pl.BlockSpec((d, LANES), const),
                  pl.BlockSpec((d, LANES), const), pl.BlockSpec((1, LANES), const)],
        out_specs=[pl.BlockSpec((tm, d), row), pl.BlockSpec((tm, d), row),
                   pl.BlockSpec((tm, LANES), row), pl.BlockSpec((tm, LANES), row),
                   pl.BlockSpec((1, 1, LANES), lambda i: (i, 0, 0))],
        out_shape=[jax.ShapeDtypeStruct((n_tok, d), F32), jax.ShapeDtypeStruct((n_tok, d), F32),
                   jax.ShapeDtypeStruct((n_tok, LANES), jnp.int32),
                   jax.ShapeDtypeStruct((n_tok, LANES), F32),
                   jax.ShapeDtypeStruct((n_tok // tm, 1, LANES), F32)],
        compiler_params=_cparams("parallel"),
        name="out_proj_router",
    )(ya, yb, yc, u, x2d, mod_l, mod_l, mod_l, g_ffn, s5_d.reshape(1, dc), glu_w.astype(BF16),
      glu_b.reshape(1, dc), w_out.astype(BF16), wr_hi, wr_lo, br)


ROW_UNROLL = 8


def _route(ri, hist, n_tok, tm):
    ntile = n_tok // tm
    h = hist.reshape(ntile, LANES)[:, N_GROUPS:N_GROUPS + N_EXPERTS].astype(jnp.int32)
    counts = jnp.sum(h, axis=0)
    pcounts = (counts + MOE_ROWS - 1) // MOE_ROWS * MOE_ROWS
    pends = jnp.cumsum(pcounts)
    base = (pends - pcounts)[None, :] + jnp.cumsum(h, axis=0) - h
    n_rows = 2 * n_tok + N_EXPERTS * MOE_ROWS
    nblk = n_rows // MOE_ROWS
    blk_start = jnp.arange(nblk, dtype=jnp.int32)[:, None] * MOE_ROWS
    blk_e = jnp.minimum(jnp.sum((pends[None, :] <= blk_start).astype(jnp.int32), axis=1), N_EXPERTS - 1)
    eid = (ri[:, 0:2] - N_GROUPS).reshape(ntile, tm, 2, 1)
    hot = eid == jnp.arange(N_EXPERTS, dtype=jnp.int32)
    dest = jnp.sum(jnp.where(hot, base[:, None, None, :], 0), axis=-1) + ri[:, 2:4].reshape(ntile, tm, 2)
    return (dest.reshape(n_tok, 2).astype(jnp.int32), blk_e.astype(jnp.int32), pends.astype(jnp.int32),
            pcounts.astype(jnp.int32), n_rows)


def _tile_rows(dest, tm):
    nblk = dest.shape[0] // tm
    return dest.reshape(nblk, tm, 2).transpose(0, 2, 1).reshape(nblk, 1, 2 * tm)


def _dispatch_kernel(pend_ref, pcnt_ref, d_ref, h_ref, xs_hbm, zbuf, hbuf, sem, zsem, *, tm, n_rows):
    i = pl.program_id(0)
    n = pl.num_programs(0)
    slot = i % 2

    @pl.when(i == 0)
    def _():
        zbuf[...] = jnp.zeros_like(zbuf)

        def zcopy(start):
            return pltpu.make_async_copy(zbuf, xs_hbm.at[pl.ds(start, MOE_ROWS)], zsem.at[0])

        first_tail = lax.shift_right_logical(pend_ref[N_EXPERTS - 1], MOE_ROWS.bit_length() - 1)
        nblk = n_rows // MOE_ROWS

        def tail_start(blk, carry):
            zcopy(pl.multiple_of(blk * MOE_ROWS, MOE_ROWS)).start()
            return carry

        def tail_wait(blk, carry):
            zcopy(0).wait()
            return carry

        for e in range(N_EXPERTS):
            @pl.when(pcnt_ref[e] > 0)
            def _():
                zcopy(pl.multiple_of(pend_ref[e] - MOE_ROWS, MOE_ROWS)).start()
        lax.fori_loop(first_tail, nblk, tail_start, 0)
        for e in range(N_EXPERTS):
            @pl.when(pcnt_ref[e] > 0)
            def _():
                zcopy(0).wait()
        lax.fori_loop(first_tail, nblk, tail_wait, 0)

    def body(g, carry):
        for u in range(ROW_UNROLL):
            r = g * ROW_UNROLL + u
            src = hbuf.at[slot, pl.ds(r, 1)]
            pltpu.make_async_copy(src, xs_hbm.at[pl.ds(d_ref[0, 0, r], 1)], sem.at[slot]).start(priority=0)
            pltpu.make_async_copy(src, xs_hbm.at[pl.ds(d_ref[0, 0, tm + r], 1)], sem.at[slot]).start(priority=1)
        return carry

    def drain(s):
        for _ in range(2):
            pltpu.make_async_copy(hbuf.at[s], xs_hbm.at[pl.ds(0, tm)], sem.at[s]).wait()

    hbuf[slot] = h_ref[...].reshape(tm, SUBLANES, LANES)
    lax.fori_loop(0, tm // ROW_UNROLL, body, 0)

    @pl.when(i > 0)
    def _():
        drain(1 - slot)

    @pl.when(i == n - 1)
    def _():
        drain(slot)


def _dispatch(h2, dest, pends, pcounts, n_rows, seq):
    n_tok, d = h2.shape
    tm = _row_tile(seq, 512)
    grid_spec = pltpu.PrefetchScalarGridSpec(
        num_scalar_prefetch=2,
        grid=(n_tok // tm,),
        in_specs=[pl.BlockSpec((1, 1, 2 * tm), lambda i, pe, pc: (i, 0, 0), memory_space=pltpu.SMEM),
                  pl.BlockSpec((tm, d), lambda i, pe, pc: (i, 0))],
        out_specs=pl.BlockSpec(memory_space=pl.ANY),
        scratch_shapes=[pltpu.VMEM((MOE_ROWS, SUBLANES, LANES), F32),
                        pltpu.VMEM((2, tm, SUBLANES, LANES), F32),
                        pltpu.SemaphoreType.DMA((2,)), pltpu.SemaphoreType.DMA((1,))],
    )
    assert d == SUBLANES * LANES
    return pl.pallas_call(
        functools.partial(_dispatch_kernel, tm=tm, n_rows=n_rows),
        grid_spec=grid_spec,
        out_shape=jax.ShapeDtypeStruct((n_rows, SUBLANES, LANES), F32),
        compiler_params=_cparams("arbitrary"),
        name="moe_dispatch",
    )(pends, pcounts, _tile_rows(dest, tm), h2)


def _moe_kernel(blk_e_ref, x_ref, w1_ref, w3_ref, w2_ref, y_ref, w1_sc, w3_sc, w2_sc):
    i = pl.program_id(0)

    @pl.when((i == 0) | (blk_e_ref[i] != blk_e_ref[jnp.maximum(i - 1, 0)]))
    def _():
        w1_sc[...] = w1_ref[0, 0].astype(BF16)
        w3_sc[...] = w3_ref[0, 0].astype(BF16)
        w2_sc[...] = w2_ref[0, 0].astype(BF16)

    rows = x_ref.shape[0]
    xb = x_ref[...].reshape(rows, SUBLANES * LANES).astype(BF16)
    act = (_silu(jnp.dot(xb, w1_sc[...], preferred_element_type=F32))
           * jnp.dot(xb, w3_sc[...], preferred_element_type=F32))
    y = jnp.dot(act.astype(BF16), w2_sc[...], preferred_element_type=F32)
    y_ref[...] = y.reshape(rows, SUBLANES, LANES)


def _moe_experts(xs, blk_e, w1, w3, w2, layer):
    n_rows = xs.shape[0]
    d, d_exp = w1.shape[2], w1.shape[3]
    row_blk = pl.BlockSpec((MOE_ROWS, SUBLANES, LANES), lambda i, e: (i, 0, 0))
    grid_spec = pltpu.PrefetchScalarGridSpec(
        num_scalar_prefetch=1,
        grid=(n_rows // MOE_ROWS,),
        in_specs=[row_blk,
                  pl.BlockSpec((1, 1, d, d_exp), lambda i, e: (layer, e[i], 0, 0)),
                  pl.BlockSpec((1, 1, d, d_exp), lambda i, e: (layer, e[i], 0, 0)),
                  pl.BlockSpec((1, 1, d_exp, d), lambda i, e: (layer, e[i], 0, 0))],
        out_specs=row_blk,
        scratch_shapes=[pltpu.VMEM((d, d_exp), BF16), pltpu.VMEM((d, d_exp), BF16),
                        pltpu.VMEM((d_exp, d), BF16)],
    )
    return pl.pallas_call(
        _moe_kernel,
        grid_spec=grid_spec,
        out_shape=jax.ShapeDtypeStruct((n_rows, SUBLANES, LANES), F32),
        compiler_params=_cparams("arbitrary"),
        name="moe_experts",
    )(blk_e, xs, w1, w3, w2)


def _comb_kernel(d_ref, d_next_ref, x_ref, rw_ref, gt_ref, y_hbm, g_ref, o_ref, ybuf, sem, *, tm, final):
    i = pl.program_id(0)
    n = pl.num_programs(0)
    slot = i % 2

    def start_all(idx_ref, s):
        def body(g, carry):
            for u in range(ROW_UNROLL):
                r = g * ROW_UNROLL + u
                pltpu.make_async_copy(y_hbm.at[pl.ds(idx_ref[0, 0, r], 1)], ybuf.at[s, pl.ds(r, 1)],
                                      sem.at[s]).start(priority=u % 2)
            return carry
        lax.fori_loop(0, 2 * tm // ROW_UNROLL, body, 0)

    @pl.when(i == 0)
    def _():
        start_all(d_ref, 0)

    @pl.when(i + 1 < n)
    def _():
        start_all(d_next_ref, 1 - slot)

    pltpu.make_async_copy(y_hbm.at[pl.ds(0, 2 * tm)], ybuf.at[slot], sem.at[slot]).wait()

    w = rw_ref[...]
    d = x_ref.shape[1]
    moe = (w[:, 0:1] * ybuf[slot, 0:tm].reshape(tm, d)
           + w[:, 1:2] * ybuf[slot, tm:2 * tm].reshape(tm, d))
    x2 = x_ref[...] + (1.0 + gt_ref[0, 0]) * moe
    o_ref[...] = _rmsnorm(x2, g_ref[...]) if final else x2


def _combine(x1, y_rows, dest, rw, mod_l, g_final, seq, final):
    n_tok, d = x1.shape
    tm = _row_tile(seq, 512)
    per_b = seq // tm
    nblk = n_tok // tm
    idx = _tile_rows(dest, tm)
    return pl.pallas_call(
        functools.partial(_comb_kernel, tm=tm, final=final),
        grid=(nblk,),
        in_specs=[pl.BlockSpec((1, 1, 2 * tm), lambda i: (i, 0, 0), memory_space=pltpu.SMEM),
                  pl.BlockSpec((1, 1, 2 * tm), lambda i: (jnp.minimum(i + 1, nblk - 1), 0, 0),
                               memory_space=pltpu.SMEM),
                  pl.BlockSpec((tm, d), lambda i: (i, 0)),
                  pl.BlockSpec((tm, LANES), lambda i: (i, 0)),
                  pl.BlockSpec((1, 1, 1, d), lambda i: (i // per_b, 5, 0, 0)),
                  pl.BlockSpec(memory_space=pl.ANY),
                  pl.BlockSpec((1, d), lambda i: (0, 0))],
        out_specs=pl.BlockSpec((tm, d), lambda i: (i, 0)),
        out_shape=jax.ShapeDtypeStruct((n_tok, d), F32),
        scratch_shapes=[pltpu.VMEM((2, 2 * tm, SUBLANES, LANES), F32), pltpu.SemaphoreType.DMA((2,))],
        compiler_params=_cparams("arbitrary"),
        name="moe_combine",
    )(idx, idx, x1, rw, mod_l, y_rows, g_final)


def kernel(x, c, ada_w, ada_b, norm_mix, norm_ffn, norm_final, w_in, w_out, mlstm_conv_w, mlstm_conv_b, mlstm_w_q, mlstm_w_k, mlstm_b_i, mlstm_b_f, mlstm_norm_w, rwkv_mu, rwkv_w0, rwkv_w_up, rwkv_a0, rwkv_a_up, rwkv_g_up, rwkv_k_k, rwkv_k_a, rwkv_r_k, rwkv_ln_w, rwkv_ln_b, rwkv_v0, rwkv_v_dn, rwkv_v_up, s5_a_re, s5_a_im, s5_log_dt, s5_b_re, s5_b_im, s5_c_re, s5_c_im, s5_d, s5_glu_w, s5_glu_b, moe_w_rg, moe_b_rg, moe_w_re, moe_b_re, moe_w1, moe_w3, moe_w2):
    bsz, seq, d = x.shape
    depth = ada_w.shape[0]
    n_tok = bsz * seq
    heads_a = mlstm_w_q.shape[1]
    da = heads_a * HEAD_DIM
    db = rwkv_w0.shape[1]
    dc = s5_d.shape[1]
    rw_cols = rwkv_mu.shape[1]
    assert seq % CHUNK == 0 and w_in.shape[2] == 3 * da + 2 * heads_a + rw_cols + dc
    assert 2 * HEAD_DIM == LANES and CHUNK == HEAD_DIM and EXPERTS_PER_GROUP == 8
    widths = (3 * da, 2 * LANES, rw_cols, dc)

    mod = _modulation(c, ada_w, ada_b).reshape(depth, bsz, 6, 1, d)
    xc = x.reshape(n_tok, d)
    v_first = None
    for l in range(depth):
        mod_l = mod[l]
        qkvo, gates, pcols, u = _in_proj(xc, mod_l, norm_mix[l].reshape(1, d), w_in, l, widths, da,
                                         heads_a, seq)

        ya = _mlstm(qkvo, gates, mlstm_conv_w[l], mlstm_conv_b[l], mlstm_w_q[l], mlstm_w_k[l],
                    mlstm_b_i[l], mlstm_b_f[l], mlstm_norm_w[l], bsz, seq)
        rprm = dict(mu=rwkv_mu[l], w0=rwkv_w0[l], w_up=rwkv_w_up[l], a0=rwkv_a0[l], a_up=rwkv_a_up[l],
                    g_up=rwkv_g_up[l], k_k=rwkv_k_k[l], k_a=rwkv_k_a[l], r_k=rwkv_r_k[l],
                    ln_w=rwkv_ln_w[l], ln_b=rwkv_ln_b[l])
        if l > 0:
            rprm.update(v0=rwkv_v0[l - 1], v_dn=rwkv_v_dn[l - 1], v_up=rwkv_v_up[l - 1])
        yb, v_first = _rwkv(pcols, v_first if l > 0 else None, rprm, bsz, seq)
        sprm = dict(a_re=s5_a_re[l], a_im=s5_a_im[l], log_dt=s5_log_dt[l], b_re=s5_b_re[l],
                    b_im=s5_b_im[l], c_re=s5_c_re[l], c_im=s5_c_im[l])
        yc = _s5_core(u, sprm, bsz, seq)

        x1, h2, ri, rw, hist = _out_proj(ya, yb, yc, u, xc, mod_l, norm_ffn[l].reshape(1, d), s5_d[l],
                                         s5_glu_w[l], s5_glu_b[l], w_out[l], moe_w_rg[l], moe_b_rg[l],
                                         moe_w_re[l], moe_b_re[l], seq)
        dest, blk_e, pends, pcounts, n_rows = _route(ri, hist, n_tok, n_tok // hist.shape[0])
        y_rows = _moe_experts(_dispatch(h2, dest, pends, pcounts, n_rows, seq), blk_e, moe_w1, moe_w3,
                              moe_w2, l)
        xc = _combine(x1, y_rows, dest, rw, mod_l, norm_final.reshape(1, d), seq, final=(l == depth - 1))
    return xc.reshape(bsz, seq, d).astype(x.dtype)
```

```python
import functools

import jax
import jax.numpy as jnp
from jax import lax
from jax.experimental import pallas as pl
from jax.experimental.pallas import tpu as pltpu

F32 = jnp.float32
BF16 = jnp.bfloat16

HEAD_DIM = 64
CHUNK = 64
CONV_K = 4
S5_GC = 16
S5_P = 64
N_GROUPS = 4
EXPERTS_PER_GROUP = 8
N_EXPERTS = N_GROUPS * EXPERTS_PER_GROUP
NORM_EPS = 1e-6
HEAD_NORM_EPS = 1e-5
RWKV_GN_EPS = 64e-5
L2_EPS = 1e-12
LANES = 128
SUBLANES = 8
MOE_ROWS = 512
VMEM_LIMIT = 56 * 1024 * 1024


def _cparams(*sem):
    return pltpu.CompilerParams(dimension_semantics=sem, vmem_limit_bytes=VMEM_LIMIT)


def _row_tile(n, want):
    t = min(n, want)
    assert n % t == 0
    return t


def _dot(a, b):
    return jnp.dot(a.astype(BF16), b.astype(BF16), preferred_element_type=F32)


def _dot_nt(a, b):
    return lax.dot_general(a.astype(BF16), b.astype(BF16), (((1,), (1,)), ((), ())),
                           preferred_element_type=F32)


def _dot_tn(a, b):
    return lax.dot_general(a.astype(BF16), b.astype(BF16), (((0,), (0,)), ((), ())),
                           preferred_element_type=F32)


def _split(a):
    hi = a.astype(BF16)
    lo = (a - hi.astype(F32)).astype(BF16)
    return hi, lo


def _dot_xa(a, b_exact):
    hi, lo = _split(a)
    return (jnp.dot(hi, b_exact, preferred_element_type=F32)
            + jnp.dot(lo, b_exact, preferred_element_type=F32))


def _dot_xb(a_exact, b):
    hi, lo = _split(b)
    return (jnp.dot(a_exact, hi, preferred_element_type=F32)
            + jnp.dot(a_exact, lo, preferred_element_type=F32))


def _dot_x3(a, b_exact):
    hi = a.astype(BF16)
    r1 = a - hi.astype(F32)
    mid = r1.astype(BF16)
    lo = (r1 - mid.astype(F32)).astype(BF16)
    return (jnp.dot(hi, b_exact, preferred_element_type=F32)
            + jnp.dot(mid, b_exact, preferred_element_type=F32)
            + jnp.dot(lo, b_exact, preferred_element_type=F32))


def _dot3(a, b):
    ah, al = _split(a)
    bh, bl = _split(b)
    return (jnp.dot(ah, bh, preferred_element_type=F32)
            + jnp.dot(ah, bl, preferred_element_type=F32)
            + jnp.dot(al, bh, preferred_element_type=F32))


def _dot3_nt(a, b):
    ah, al = _split(a)
    bh, bl = _split(b)
    dn = (((1,), (1,)), ((), ()))
    return (lax.dot_general(ah, bh, dn, preferred_element_type=F32)
            + lax.dot_general(ah, bl, dn, preferred_element_type=F32)
            + lax.dot_general(al, bh, dn, preferred_element_type=F32))


def _sigmoid(x):
    return 1.0 / (1.0 + jnp.exp(-x))


def _silu(x):
    return x * _sigmoid(x)


def _log_sigmoid(x):
    return jnp.minimum(x, 0.0) - jnp.log1p(jnp.exp(-jnp.abs(x)))


def _rmsnorm(x, g):
    ms = jnp.mean(x * x, axis=-1, keepdims=True)
    return x * lax.rsqrt(ms + NORM_EPS) * g


def _tri_incl(n):
    r = lax.broadcasted_iota(jnp.int32, (n, n), 0)
    c = lax.broadcasted_iota(jnp.int32, (n, n), 1)
    return (c <= r).astype(BF16)


def _head_ones(width):
    r = lax.broadcasted_iota(jnp.int32, (width, width), 0) // HEAD_DIM
    c = lax.broadcasted_iota(jnp.int32, (width, width), 1) // HEAD_DIM
    return (r == c).astype(BF16)


def _mod_kernel(c_ref, w_ref, b_ref, o_ref):
    o_ref[0] = _dot(_silu(c_ref[...]), w_ref[0]) + b_ref[0]


def _modulation(c, ada_w, ada_b):
    depth, d, d6 = ada_w.shape
    bsz = c.shape[0]
    tn = _row_tile(d6, 1024)
    return pl.pallas_call(
        _mod_kernel,
        grid=(depth, d6 // tn),
        in_specs=[pl.BlockSpec((bsz, d), lambda l, j: (0, 0)),
                  pl.BlockSpec((1, d, tn), lambda l, j: (l, 0, j)),
                  pl.BlockSpec((1, 1, tn), lambda l, j: (l, 0, j))],
        out_specs=pl.BlockSpec((1, bsz, tn), lambda l, j: (l, 0, j)),
        out_shape=jax.ShapeDtypeStruct((depth, bsz, d6), F32),
        compiler_params=_cparams("parallel", "parallel"),
        name="adaln_mod",
    )(c, ada_w, ada_b.reshape(depth, 1, d6))


def _in_kernel(x_ref, sh_ref, sc_ref, g_ref, w_ref, *refs, widths, da, heads):
    out_refs, w_sc = refs[:-1], refs[-1]
    d = w_ref.shape[1]
    rw_cols, dc = widths[2], widths[3]

    @pl.when(pl.program_id(0) == 0)
    def _():
        rb = 256
        g0 = 3 * da
        r_src = g0 + 2 * heads
        u_src = r_src + rw_cols
        r_win = -(-(2 * heads + rw_cols) // LANES) * LANES
        u_al = u_src // LANES * LANES
        lane = lax.broadcasted_iota(jnp.int32, (rb, LANES), 1)
        for r0 in range(0, d, rb):
            rs = slice(r0, r0 + rb)
            w_sc[rs, 0:g0] = w_ref[0, rs, 0:g0].astype(BF16)
            gblk = w_ref[0, rs, g0:g0 + LANES]
            w_sc[rs, g0:g0 + LANES] = jnp.where(lane < heads, gblk, 0.0).astype(BF16)
            w_sc[rs, g0 + LANES:g0 + 2 * LANES] = jnp.where(
                lane < heads, pltpu.roll(gblk, LANES - heads, 1), 0.0).astype(BF16)
            win = w_ref[0, rs, g0:g0 + r_win]
            w_sc[rs, g0 + 2 * LANES:g0 + 2 * LANES + rw_cols] = win[:, 2 * heads:2 * heads + rw_cols].astype(BF16)
            win2 = w_ref[0, rs, u_al:u_src + dc]
            w_sc[rs, g0 + 2 * LANES + rw_cols:] = win2[:, u_src - u_al:u_src - u_al + dc].astype(BF16)

    h = _rmsnorm(x_ref[...], g_ref[...]) * (1.0 + sc_ref[0, 0]) + sh_ref[0, 0]
    hb = h.astype(BF16)
    off = 0
    for k, (o_ref, wd) in enumerate(zip(out_refs, widths)):
        res = jnp.dot(hb, w_sc[:, off:off + wd], preferred_element_type=F32)
        o_ref[...] = res.T if k == len(widths) - 1 else res
        off += wd


def _in_proj(x2d, mod_l, g, w_in, layer, widths, da, heads, seq):
    n_tok, d = x2d.shape
    cols = w_in.shape[2]
    tm = _row_tile(seq, 512)
    per_b = seq // tm
    return pl.pallas_call(
        functools.partial(_in_kernel, widths=widths, da=da, heads=heads),
        grid=(n_tok // tm,),
        in_specs=[pl.BlockSpec((tm, d), lambda i: (i, 0)),
                  pl.BlockSpec((1, 1, 1, d), lambda i: (i // per_b, 0, 0, 0)),
                  pl.BlockSpec((1, 1, 1, d), lambda i: (i // per_b, 1, 0, 0)),
                  pl.BlockSpec((1, d), lambda i: (0, 0)),
                  pl.BlockSpec((1, d, cols), lambda i: (layer, 0, 0))],
        out_specs=[pl.BlockSpec((tm, wd), lambda i: (i, 0)) for wd in widths[:-1]]
        + [pl.BlockSpec((widths[-1], tm), lambda i: (0, i))],
        out_shape=[jax.ShapeDtypeStruct((n_tok, wd), F32) for wd in widths[:-1]]
        + [jax.ShapeDtypeStruct((widths[-1], n_tok), F32)],
        scratch_shapes=[pltpu.VMEM((d, sum(widths)), BF16)],
        compiler_params=_cparams("arbitrary"),
        name="in_proj",
    )(x2d, mod_l, mod_l, g, w_in)


def _mlstm_kernel(qkvo_ref, gate_ref, cw_ref, cb_ref, wq_ref, wk_ref, gb_ref, nw_ref, out_ref,
                  xf_sc, q_sc, k_sc, gi_sc, gf_sc, cn_sc, m_sc, *, nb, tb, heads):
    dh, L = HEAD_DIM, CHUNK
    da = heads * dh
    i = pl.program_id(0)

    @pl.when(i == 0)
    def _():
        xf_sc[:, 0:SUBLANES, :] = jnp.zeros((nb, SUBLANES, da), F32)
        cn_sc[...] = jnp.zeros_like(cn_sc)
        m_sc[...] = jnp.zeros_like(m_sc)

    for b in range(nb):
        xqk = qkvo_ref[b, :, 0:da]
        xf_sc[b, pl.ds(SUBLANES, tb), :] = xqk
        acc = xqk * cw_ref[CONV_K - 1:CONV_K, :] + cb_ref[...]
        for j in range(1, CONV_K):
            acc = acc + xf_sc[b, pl.ds(SUBLANES - j, tb), :] * cw_ref[CONV_K - 1 - j:CONV_K - j, :]
        xf_sc[b, 0:SUBLANES, :] = xf_sc[b, pl.ds(tb, SUBLANES), :]
        cx = _silu(acc).astype(BF16)
        q_sc[b] = jnp.dot(cx, wq_ref[...], preferred_element_type=F32)
        k_sc[b] = jnp.dot(cx, wk_ref[...], preferred_element_type=F32) * (dh ** -0.5)

        g = gate_ref[b] + gb_ref[...]
        gi_sc[b] = g[:, :LANES]
        gf_sc[b] = _log_sigmoid(g[:, LANES:])

    tri = _tri_incl(L)
    rr = lax.broadcasted_iota(jnp.int32, (L, L), 0)
    cc = lax.broadcasted_iota(jnp.int32, (L, L), 1)
    causal = cc <= rr
    row_l = lax.broadcasted_iota(jnp.int32, (L, LANES), 0)
    sel = (lax.broadcasted_iota(jnp.int32, (LANES, heads * LANES), 0)
           == lax.broadcasted_iota(jnp.int32, (LANES, heads * LANES), 1) // LANES).astype(BF16)
    ones_v = jnp.ones((L, dh), F32)
    mean_m = jnp.full((dh, dh), 1.0 / dh, BF16)

    units = [(b, h) for b in range(nb) for h in range(heads)]
    ur = range(len(units))

    def hcol(h, base=0):
        return slice(base + h * dh, base + (h + 1) * dh)

    def chunk(c, carry):
        rows = pl.ds(pl.multiple_of(c * L, L), L)
        ex, g_t = [], []
        for b in range(nb):
            bc = _dot_xb(tri, gf_sc[b, rows, :])
            gtot = bc[L - 1:L, :]
            g = gi_sc[b, rows, :] - bc
            gmax = jnp.max(g, axis=0, keepdims=True)
            m_prev = m_sc[b]
            m_loc = gtot + gmax
            m_new = jnp.maximum(gtot + m_prev, m_loc)
            a_old = jnp.exp(gtot + m_prev - m_new)
            a_loc = jnp.exp(m_loc - m_new)
            m_sc[b] = m_new
            pm = g
            d = 1
            while d < L:
                pm = jnp.maximum(pm, jnp.where(row_l >= d, pltpu.roll(pm, d, 0), -jnp.inf))
                d *= 2
            mm = jnp.maximum(m_prev, pm)
            stack = jnp.concatenate([jnp.exp(g - gmax), mm, jnp.exp(m_prev - mm), jnp.exp(-(bc + mm)),
                                     a_old, a_loc, jnp.zeros((SUBLANES - 2, LANES), F32)], axis=0)
            ex.append(_dot_x3(stack, sel))
            g_t.append(g.T)

        def part(k, b, h, width=dh):
            return ex[b][k * L:(k + 1) * L, h * LANES:h * LANES + width]

        qc = [q_sc[b, rows, hcol(h)] for b, h in units]
        kc = [k_sc[b, rows, hcol(h)] for b, h in units]
        vo = [jnp.concatenate([qkvo_ref[b, rows, hcol(h, da)], ones_v], axis=1) for b, h in units]
        oc = [qkvo_ref[b, rows, hcol(h, 2 * da)] for b, h in units]
        cn_prev = [cn_sc[b, h] for b, h in units]
        s_raw = [_dot_nt(qc[u], kc[u]) for u in ur]
        q_cn = [_dot(qc[u], cn_prev[u]) for u in ur]
        cn_loc = [_dot_tn(kc[u] * part(0, b, h), vo[u]) for u, (b, h) in enumerate(units)]
        s_qk = [s_raw[u] * jnp.where(causal, jnp.exp(g_t[b][h:h + 1, :] - part(1, b, h)), 0.0)
                for u, (b, h) in enumerate(units)]
        s_vn = [_dot(s_qk[u], vo[u]) for u in ur]
        nd = [part(2, b, h, 2 * dh) * q_cn[u] + s_vn[u] for u, (b, h) in enumerate(units)]
        hh = [nd[u][:, :dh] / jnp.maximum(jnp.abs(nd[u][:, dh:]), part(3, b, h))
              for u, (b, h) in enumerate(units)]
        mu = [_dot_xa(hh[u], mean_m) for u in ur]
        dlt = [hh[u] - mu[u] for u in ur]
        var = [_dot_xa(dlt[u] * dlt[u], mean_m) for u in ur]
        outs = [dlt[u] * lax.rsqrt(var[u] + HEAD_NORM_EPS) * _sigmoid(oc[u]) for u in ur]
        for b in range(nb):
            mine = [u for u in ur if units[u][0] == b]
            out_ref[b, rows, :] = jnp.concatenate([outs[u] for u in mine], axis=1) * nw_ref[...]
        for u, (b, h) in enumerate(units):
            cn_sc[b, h] = (ex[b][4 * L:4 * L + 1, h * LANES:(h + 1) * LANES] * cn_prev[u]
                           + ex[b][4 * L + 1:4 * L + 2, h * LANES:(h + 1) * LANES] * cn_loc[u])
        return carry

    lax.fori_loop(0, tb // L, chunk, 0)


def _block_diag(w):
    heads, dh, _ = w.shape
    eye = jnp.eye(heads, dtype=w.dtype)
    return (eye[:, None, :, None] * w[:, :, None, :]).reshape(heads * dh, heads * dh)


def _mlstm(qkvo, gates, conv_w, conv_b, w_q, w_k, b_i, b_f, norm_w, bsz, seq):
    heads = w_q.shape[0]
    da = heads * HEAD_DIM
    tb = _row_tile(seq, 256)
    nblk = seq // tb
    gbias = jnp.zeros((1, 2 * LANES), F32).at[0, :heads].set(b_i).at[0, LANES:LANES + heads].set(b_f)
    kern = functools.partial(_mlstm_kernel, nb=bsz, tb=tb, heads=heads)
    row = lambda i: (0, i, 0)
    const = lambda i: (0, 0)
    out = pl.pallas_call(
        kern,
        grid=(nblk,),
        in_specs=[pl.BlockSpec((bsz, tb, 3 * da), row),
                  pl.BlockSpec((bsz, tb, 2 * LANES), row),
                  pl.BlockSpec((CONV_K, da), const),
                  pl.BlockSpec((1, da), const),
                  pl.BlockSpec((da, da), const),
                  pl.BlockSpec((da, da), const),
                  pl.BlockSpec((1, 2 * LANES), const),
                  pl.BlockSpec((1, da), const)],
        out_specs=pl.BlockSpec((bsz, tb, da), row),
        out_shape=jax.ShapeDtypeStruct((bsz, seq, da), F32),
        scratch_shapes=[pltpu.VMEM((bsz, tb + SUBLANES, da), F32),
                        pltpu.VMEM((bsz, tb, da), F32),
                        pltpu.VMEM((bsz, tb, da), F32),
                        pltpu.VMEM((bsz, tb, LANES), F32),
                        pltpu.VMEM((bsz, tb, LANES), F32),
                        pltpu.VMEM((bsz, heads, HEAD_DIM, 2 * HEAD_DIM), F32),
                        pltpu.VMEM((bsz, 1, LANES), F32)],
        compiler_params=_cparams("arbitrary"),
        name="mlstm",
    )(qkvo.reshape(bsz, seq, 3 * da), gates.reshape(bsz, seq, 2 * LANES), conv_w, conv_b.reshape(1, da),
      _block_diag(w_q).astype(BF16), _block_diag(w_k).astype(BF16), gbias, norm_w.reshape(1, da))
    return out.reshape(bsz * seq, da)


def _rwkv_kernel(*refs, nb, tb, heads, lw_dim, la_dim, lg_dim, has_vres):
    dh, L = HEAD_DIM, CHUNK
    db = heads * dh
    it = iter(refs)
    p_ref = next(it)
    vf_ref = next(it) if has_vres else None
    (mu_ref, w0_ref, wup_ref, a0_ref, aup_ref, gup_ref, kk_ref, ka_ref, rk_ref,
     lnw_ref, lnb_ref) = (next(it) for _ in range(11))
    if has_vres:
        v0_ref, vdn_ref, vup_ref = (next(it) for _ in range(3))
    y_ref = next(it)
    vout_ref = None if has_vres else next(it)
    (xf_sc, r_sc, k_sc, v_sc, a_sc, b_sc, lw_sc, y_sc, gate_sc, rp_sc, st_sc, q_sc, z_sc,
     gl_sc) = (next(it) for _ in range(14))

    i = pl.program_id(0)

    @pl.when(i == 0)
    def _():
        xf_sc[:, 0:SUBLANES, :] = jnp.zeros((nb, SUBLANES, xf_sc.shape[2]), F32)
        st_sc[...] = jnp.zeros_like(st_sc)

    ones_h = _head_ones(db)
    for b in range(nb):
        p = p_ref[b]
        xf_sc[b, pl.ds(SUBLANES, tb), :] = p
        prev = xf_sc[b, pl.ds(SUBLANES - 1, tb), :]
        xf_sc[b, 0:SUBLANES, :] = xf_sc[b, pl.ds(tb, SUBLANES), :]
        p = p + mu_ref[...] * (prev - p)

        o = 0
        r = p[:, o:o + db]; o += db
        k = p[:, o:o + db]; o += db
        v = p[:, o:o + db]; o += db
        wd = p[:, o:o + lw_dim]; o += lw_dim
        ad = p[:, o:o + la_dim]; o += la_dim
        gd = p[:, o:o + lg_dim]

        wlog = _log_sigmoid(w0_ref[...] + _dot(jnp.tanh(wd), wup_ref[...])) - 0.5
        lw_sc[b] = -jnp.exp(wlog)
        a = _sigmoid(a0_ref[...] + _dot(ad, aup_ref[...]))
        gate_sc[b] = _dot(_sigmoid(gd), gup_ref[...])
        if has_vres:
            v = v + (vf_ref[b] - v) * _sigmoid(v0_ref[...] + _dot(_dot(v, vdn_ref[...]), vup_ref[...]))
        else:
            vout_ref[b] = v
        kk = k * kk_ref[...]
        kk = kk / jnp.maximum(jnp.sqrt(_dot_xa(kk * kk, ones_h)), L2_EPS)
        r_sc[b] = r
        k_sc[b] = k * (1.0 + (a - 1.0) * ka_ref[...])
        v_sc[b] = v
        a_sc[b] = -kk
        b_sc[b] = kk * a

    tri = _tri_incl(L)
    rr = lax.broadcasted_iota(jnp.int32, (L, L), 0)
    cc = lax.broadcasted_iota(jnp.int32, (L, L), 1)
    strict = cc < rr
    incl = cc <= rr

    units = [(b, h) for b in range(nb) for h in range(heads)]
    ur = range(len(units))

    def hcol(h):
        return slice(h * dh, (h + 1) * dh)

    def chunk(c, carry):
        rows = pl.ds(pl.multiple_of(c * L, L), L)
        at, rt, bt, kt, bg, kg, vch = ([] for _ in range(7))
        for b in range(nb):
            lwc = lw_sc[b, rows, :]
            cum = _dot_xb(tri, lwc)
            cum_l = cum[L - 1:L, :]
            e_inv = jnp.exp(-cum)
            e_end = jnp.exp(cum_l - cum)
            bv = b_sc[b, rows, :]
            kv = k_sc[b, rows, :]
            at.append(a_sc[b, rows, :] * jnp.exp(cum - lwc))
            rt.append(r_sc[b, rows, :] * jnp.exp(cum))
            bt.append(bv * e_inv)
            kt.append(kv * e_inv)
            bg.append(bv * e_end)
            kg.append(kv * e_end)
            vch.append(v_sc[b, rows, :])
            gl_sc[c, b] = jnp.exp(cum_l)
        vh = [vch[b][:, hcol(h)] for b, h in units]
        g4 = [_dot_nt(jnp.concatenate([at[b][:, hcol(h)], rt[b][:, hcol(h)]], axis=0),
                      jnp.concatenate([bt[b][:, hcol(h)], kt[b][:, hcol(h)]], axis=0))
              for b, h in units]
        pw = [jnp.where(strict, g4[u][:L, :L], 0.0) for u in ur]
        n_ak = [jnp.where(strict, g4[u][:L, L:], 0.0) for u in ur]
        m_rb = [jnp.where(incl, g4[u][L:, :L], 0.0) for u in ur]
        m_rk = [jnp.where(incl, g4[u][L:, L:], 0.0) for u in ur]
        nv = [_dot(jnp.concatenate([n_ak[u], m_rk[u]], axis=0), vh[u]) for u in ur]
        x = [jnp.concatenate([at[b][:, hcol(h)], nv[u][:L]], axis=1)
             for u, (b, h) in enumerate(units)]
        for step in range(6):
            if step < 5:
                px = [_dot(pw[u], jnp.concatenate([x[u], pw[u]], axis=1)) for u in ur]
                x = [x[u] + px[u][:, :2 * dh] for u in ur]
                pw = [px[u][:, 2 * dh:] for u in ur]
            else:
                x = [x[u] + _dot(pw[u], x[u]) for u in ur]
        ry = [jnp.concatenate([rt[b][:, hcol(h)], nv[u][L:]], axis=1) + _dot(m_rb[u], x[u])
              for u, (b, h) in enumerate(units)]
        qz = [_dot_tn(x[u], bg[b][:, hcol(h)]) for u, (b, h) in enumerate(units)]
        z2 = [_dot_tn(vh[u], kg[b][:, hcol(h)]) for u, (b, h) in enumerate(units)]
        for b in range(nb):
            mine = [u for u in ur if units[u][0] == b]
            rp_sc[b, rows, :] = jnp.concatenate([ry[u][:, :dh] for u in mine], axis=1)
            y_sc[b, rows, :] = jnp.concatenate([ry[u][:, dh:] for u in mine], axis=1)
        for u, (b, h) in enumerate(units):
            q_sc[c, b, h] = qz[u][:dh]
            z_sc[c, b, h] = qz[u][dh:] + z2[u]
        return carry

    lax.fori_loop(0, tb // L, chunk, 0)

    def carry_state(c, carry):
        rows = pl.ds(pl.multiple_of(c * L, L), L)
        st = [st_sc[b, h] for b, h in units]
        ys = [_dot_nt(rp_sc[b, rows, hcol(h)], st[u]) for u, (b, h) in enumerate(units)]
        sq = [_dot_xa(st[u], q_sc[c, b, h].astype(BF16)) for u, (b, h) in enumerate(units)]
        for b in range(nb):
            mine = [u for u in ur if units[u][0] == b]
            y_sc[b, rows, :] = y_sc[b, rows, :] + jnp.concatenate([ys[u] for u in mine], axis=1)
        for u, (b, h) in enumerate(units):
            st_sc[b, h] = st[u] * gl_sc[c, b][:, hcol(h)] + sq[u] + z_sc[c, b, h]
        return carry

    lax.fori_loop(0, tb // L, carry_state, 0)

    for b in range(nb):
        y = y_sc[b]
        mean = _dot_xa(y, ones_h) * (1.0 / dh)
        dlt = y - mean
        var = _dot_xa(dlt * dlt, ones_h) * (1.0 / dh)
        yn = dlt * lax.rsqrt(var + RWKV_GN_EPS) * lnw_ref[...] + lnb_ref[...]
        bonus = _dot_xa(r_sc[b] * k_sc[b] * rk_ref[...], ones_h) * v_sc[b]
        y_ref[b] = (yn + bonus) * gate_sc[b]


def _rwkv(pcols, v_first, prm, bsz, seq):
    db = prm["w0"].shape[0]
    heads = db // HEAD_DIM
    cols = pcols.shape[1]
    lw_dim, la_dim, lg_dim = prm["w_up"].shape[0], prm["a_up"].shape[0], prm["g_up"].shape[0]
    has_vres = v_first is not None
    tb = _row_tile(seq, 256)
    nblk = seq // tb
    row = lambda i: (0, i, 0)
    const = lambda i: (0, 0)
    vec = lambda a: a.reshape(1, -1).astype(F32)

    args = [pcols.reshape(bsz, seq, cols)]
    specs = [pl.BlockSpec((bsz, tb, cols), row)]
    if has_vres:
        args.append(v_first.reshape(bsz, seq, db))
        specs.append(pl.BlockSpec((bsz, tb, db), row))
    small = [vec(prm["mu"]), vec(prm["w0"]), prm["w_up"].astype(BF16), vec(prm["a0"]),
             prm["a_up"].astype(BF16), prm["g_up"].astype(BF16), vec(prm["k_k"]), vec(prm["k_a"]),
             vec(prm["r_k"]), vec(prm["ln_w"]), vec(prm["ln_b"])]
    if has_vres:
        lv = prm["v_dn"].shape[1]
        lvp = -(-lv // LANES) * LANES
        v_dn = jnp.zeros((db, lvp), F32).at[:, :lv].set(prm["v_dn"]).astype(BF16)
        v_up = jnp.zeros((lvp, db), F32).at[:lv, :].set(prm["v_up"]).astype(BF16)
        small += [vec(prm["v0"]), v_dn, v_up]
    args += small
    specs += [pl.BlockSpec(a.shape, const) for a in small]

    out_shape = [jax.ShapeDtypeStruct((bsz, seq, db), F32)]
    out_specs = [pl.BlockSpec((bsz, tb, db), row)]
    if not has_vres:
        out_shape.append(jax.ShapeDtypeStruct((bsz, seq, db), F32))
        out_specs.append(pl.BlockSpec((bsz, tb, db), row))

    kern = functools.partial(_rwkv_kernel, nb=bsz, tb=tb, heads=heads, lw_dim=lw_dim, la_dim=la_dim,
                             lg_dim=lg_dim, has_vres=has_vres)
    nch = tb // CHUNK
    res = pl.pallas_call(
        kern,
        grid=(nblk,),
        in_specs=specs,
        out_specs=out_specs,
        out_shape=out_shape,
        scratch_shapes=[pltpu.VMEM((bsz, tb + SUBLANES, cols), F32)]
        + [pltpu.VMEM((bsz, tb, db), F32) for _ in range(9)]
        + [pltpu.VMEM((bsz, heads, HEAD_DIM, HEAD_DIM), F32),
           pltpu.VMEM((nch, bsz, heads, HEAD_DIM, HEAD_DIM), F32),
           pltpu.VMEM((nch, bsz, heads, HEAD_DIM, HEAD_DIM), F32),
           pltpu.VMEM((nch, bsz, 1, db), F32)],
        compiler_params=_cparams("arbitrary"),
        name="rwkv7",
    )(*args)
    res = [a.reshape(bsz * seq, db) for a in res]
    return (res[0], v_first) if has_vres else (res[0], res[1])


def _cmul(ar, ai, br, bi):
    return ar * br - ai * bi, ar * bi + ai * br


def _shift_rows(x, d):
    row = lax.broadcasted_iota(jnp.int32, x.shape, 0)
    return jnp.where(row >= d, pltpu.roll(x, d, 0), 0.0)


def _s5_kernel(u_ref, are_r, aim_r, dt_r, are_c, aim_c, dt_c, bre_ref, bim_ref, cre_ref, cim_ref,
               y_ref, m_sc, *, bsz, nchunk):
    L, gc, P = CHUNK, S5_GC, S5_P
    n = L * gc
    a_re, a_im, dt = are_r[0], aim_r[0], jnp.exp(dt_r[0])
    mag, ang = jnp.exp(a_re * dt), a_im * dt
    ab_re, ab_im = mag * jnp.cos(ang), mag * jnp.sin(ang)
    inv = 1.0 / (a_re * a_re + a_im * a_im)
    co_re = ((ab_re - 1.0) * a_re + ab_im * a_im) * inv
    co_im = (ab_im * a_re - (ab_re - 1.0) * a_im) * inv
    bb_re, bb_im = _cmul(co_re, co_im, bre_ref[0], bim_ref[0])
    c_re, c_im = cre_ref[0], cim_ref[0]

    def powers(tau):
        m = jnp.exp(tau * (a_re * dt))
        return m * jnp.cos(tau * ang), m * jnp.sin(tau * ang)

    a_re_c, a_im_c, dt_c_ = are_c[0], aim_c[0], jnp.exp(dt_c[0])
    tau_row = lax.broadcasted_iota(jnp.int32, (P, L), 1).astype(F32)
    pm = jnp.exp(tau_row * (a_re_c * dt_c_))
    pt_re = pm * jnp.cos(tau_row * (a_im_c * dt_c_))
    pt_im = pm * jnp.sin(tau_row * (a_im_c * dt_c_))
    pair = lax.broadcasted_iota(jnp.int32, (gc * gc, gc), 0)
    col = lax.broadcasted_iota(jnp.int32, (gc * gc, gc), 1)
    rep_c = (pair // gc == col).astype(BF16)
    rep_b = (pair % gc == col).astype(BF16)
    cb_re, cb_im = _cmul(_dot_xb(rep_c, c_re), _dot_xb(rep_c, c_im),
                         _dot_xb(rep_b, bb_re), _dot_xb(rep_b, bb_im))
    kap = _dot3(cb_re, pt_re) - _dot3(cb_im, pt_im)

    kap_pad = jnp.concatenate([kap, jnp.zeros_like(kap)], axis=1)
    srow = lax.broadcasted_iota(jnp.int32, (L, LANES), 0)
    tcol = lax.broadcasted_iota(jnp.int32, (L, LANES), 1)
    for cp in range(gc):
        for c2 in range(0, gc, 2):
            k0 = jnp.broadcast_to(kap_pad[c2 * gc + cp:c2 * gc + cp + 1, :], (L, LANES))
            k1 = jnp.broadcast_to(kap_pad[(c2 + 1) * gc + cp:(c2 + 1) * gc + cp + 1, :], (L, LANES))
            t0 = pltpu.roll(k0, 0, 1, stride=1, stride_axis=0)
            t1 = pltpu.roll(k1, L, 1, stride=1, stride_axis=0)
            blk = jnp.where(tcol < L, jnp.where(tcol >= srow, t0, 0.0),
                            jnp.where(tcol - L >= srow, t1, 0.0))
            m_sc[cp * L:(cp + 1) * L, c2 * L:(c2 + 2) * L] = blk.astype(BF16)

    s_col = lax.broadcasted_iota(jnp.int32, (L, P), 0).astype(F32)
    pw_re, pw_im = powers((L - 1.0) - s_col)
    pg_re, pg_im = powers(s_col + 1.0)
    w_re, w_im, g_re, g_im = [], [], [], []
    for c in range(gc):
        br = jnp.broadcast_to(bb_re[c:c + 1, :], (L, P))
        bi = jnp.broadcast_to(bb_im[c:c + 1, :], (L, P))
        wr, wi = _cmul(br, bi, pw_re, pw_im)
        w_re.append(wr); w_im.append(wi)
        cr = jnp.broadcast_to(c_re[c:c + 1, :], (L, P))
        ci = jnp.broadcast_to(c_im[c:c + 1, :], (L, P))
        gr, gi = _cmul(cr, ci, pg_re, pg_im)
        g_re.append(gr); g_im.append(gi)
    w_re, w_im = jnp.concatenate(w_re, axis=0), jnp.concatenate(w_im, axis=0)
    g_re, g_im = jnp.concatenate(g_re, axis=0), jnp.concatenate(g_im, axis=0)

    u = pltpu.einshape("c(js)->j(cs)", u_ref[...], s=L).astype(BF16)
    x_re = jnp.dot(u, w_re.astype(BF16), preferred_element_type=F32)
    x_im = jnp.dot(u, w_im.astype(BF16), preferred_element_type=F32)
    xs_re, xs_im = [], []
    for b in range(bsz):
        xr = x_re[b * nchunk:(b + 1) * nchunk]
        xi = x_im[b * nchunk:(b + 1) * nchunk]
        d = 1
        while d < nchunk:
            ar_, ai_ = powers(float(L * d))
            sr, si = _cmul(ar_, ai_, _shift_rows(xr, d), _shift_rows(xi, d))
            xr, xi = xr + sr, xi + si
            d *= 2
        xs_re.append(_shift_rows(xr, 1))
        xs_im.append(_shift_rows(xi, 1))
    xs_re, xs_im = jnp.concatenate(xs_re, axis=0), jnp.concatenate(xs_im, axis=0)
    y = jnp.dot(u, m_sc[...], preferred_element_type=F32)
    y = y + _dot_nt(xs_re, g_re) - _dot_nt(xs_im, g_im)
    y_ref[...] = pltpu.einshape("j(cs)->c(js)", y, s=L)


def _s5_core(ut, prm, bsz, seq):
    L, gc, P = CHUNK, S5_GC, S5_P
    n_tok = ut.shape[1]
    groups = ut.shape[0] // gc
    nchunk = seq // L
    n = L * gc
    row3 = lambda a: a.reshape(groups, 1, P).astype(F32)
    col3 = lambda a: a.reshape(groups, P, 1).astype(F32)
    dt_b = jnp.broadcast_to(prm["log_dt"][:, None], (groups, P))
    args = [ut, row3(prm["a_re"]), row3(prm["a_im"]), row3(dt_b),
            col3(prm["a_re"]), col3(prm["a_im"]), col3(dt_b),
            prm["b_re"].transpose(0, 2, 1), prm["b_im"].transpose(0, 2, 1),
            prm["c_re"], prm["c_im"]]
    g3 = lambda g: (g, 0, 0)
    specs = [pl.BlockSpec((gc, n_tok), lambda g: (g, 0))]
    specs += [pl.BlockSpec((1, 1, P), g3)] * 3 + [pl.BlockSpec((1, P, 1), g3)] * 3
    specs += [pl.BlockSpec((1, gc, P), g3)] * 4
    return pl.pallas_call(
        functools.partial(_s5_kernel, bsz=bsz, nchunk=nchunk),
        grid=(groups,),
        in_specs=specs,
        out_specs=pl.BlockSpec((gc, n_tok), lambda g: (g, 0)),
        out_shape=jax.ShapeDtypeStruct(ut.shape, F32),
        scratch_shapes=[pltpu.VMEM((n, n), BF16)],
        compiler_params=_cparams("parallel"),
        name="s5_core",
    )(*args)


def _gelu_tanh(x):
    return 0.5 * x * (1.0 + jnp.tanh(0.7978845608028654 * (x + 0.044715 * x * x * x)))


def _out_kernel(ya_ref, yb_ref, yc_ref, u_ref, x_ref, gt1_ref, sh2_ref, sc2_ref, g_ref, d_ref,
                gw_ref, gbias_ref, wo_ref, wrh_ref, wrl_ref, br_ref,
                x1_ref, h2_ref, ri_ref, rw_ref, hist_ref, *, da, db):
    yc = _gelu_tanh(yc_ref[...].T + d_ref[...] * u_ref[...].T)
    yc = yc * _sigmoid(_dot(yc, gw_ref[...]) + gbias_ref[...])
    mixed = (_dot(ya_ref[...], wo_ref[0:da, :]) + _dot(yb_ref[...], wo_ref[da:da + db, :])
             + _dot(yc, wo_ref[da + db:, :]))
    x1 = x_ref[...] + (1.0 + gt1_ref[0, 0]) * mixed
    x1_ref[...] = x1
    h2 = _rmsnorm(x1, g_ref[...]) * (1.0 + sc2_ref[0, 0]) + sh2_ref[0, 0]
    h2_ref[...] = h2

    hh, hl = _split(h2)
    logits = (jnp.dot(hh, wrh_ref[...], preferred_element_type=F32)
              + jnp.dot(hh, wrl_ref[...], preferred_element_type=F32)
              + jnp.dot(hl, wrh_ref[...], preferred_element_type=F32)) + br_ref[...]
    lane_i = lax.broadcasted_iota(jnp.int32, logits.shape, 1)
    lane = lane_i.astype(F32)
    big = float(LANES)
    neg = -jnp.inf
    is_g = lane_i < N_GROUPS
    lg = jnp.where(is_g, logits, neg)
    gmax = jnp.max(lg, axis=1, keepdims=True)
    gi = jnp.min(jnp.where(is_g & (lg == gmax), lane, big), axis=1, keepdims=True)
    gp = 1.0 / jnp.sum(jnp.where(is_g, jnp.exp(lg - gmax), 0.0), axis=1, keepdims=True)
    e_lane = lane_i - N_GROUPS
    grp_of_lane = lax.shift_right_arithmetic(e_lane, 3).astype(F32)
    in_grp = (e_lane >= 0) & (e_lane < N_EXPERTS) & (grp_of_lane == gi)
    l1 = jnp.where(in_grp, logits, neg)
    m1 = jnp.max(l1, axis=1, keepdims=True)
    i1 = jnp.min(jnp.where(in_grp & (l1 == m1), lane, big), axis=1, keepdims=True)
    rest = in_grp & (lane != i1)
    l2 = jnp.where(rest, logits, neg)
    m2 = jnp.max(l2, axis=1, keepdims=True)
    i2 = jnp.min(jnp.where(rest & (l2 == m2), lane, big), axis=1, keepdims=True)
    e2 = jnp.exp(m2 - m1)
    w1 = gp / (1.0 + e2)
    w2 = gp * e2 / (1.0 + e2)
    hot1 = (lane == i1).astype(F32)
    hot2 = (lane == i2).astype(F32)
    both = (hot1 + hot2).astype(BF16)
    tm = logits.shape[0]
    rr = lax.broadcasted_iota(jnp.int32, (tm, tm), 0)
    cc = lax.broadcasted_iota(jnp.int32, (tm, tm), 1)
    before = jnp.dot((cc < rr).astype(BF16), both, preferred_element_type=F32)
    rank1 = jnp.sum(before * hot1, axis=1, keepdims=True)
    rank2 = jnp.sum(before * hot2, axis=1, keepdims=True)
    hist_ref[0] = jnp.sum(hot1 + hot2, axis=0, keepdims=True)
    ids = jnp.where(lane_i == 0, i1, jnp.where(lane_i == 1, i2, jnp.where(lane_i == 2, rank1, rank2)))
    ri_ref[...] = ids.astype(jnp.int32)
    rw_ref[...] = jnp.where(lane_i == 0, w1, jnp.where(lane_i == 1, w2, 0.0))


def _out_proj(ya, yb, yc, u, x2d, mod_l, g_ffn, s5_d, glu_w, glu_b, w_out, w_rg, b_rg, w_re, b_re, seq):
    n_tok, d = x2d.shape
    da, db, dc = ya.shape[1], yb.shape[1], yc.shape[0]
    tm = _row_tile(seq, 512)
    per_b = seq // tm
    wr = jnp.zeros((d, LANES), F32).at[:, :N_GROUPS].set(w_rg).at[:, N_GROUPS:N_GROUPS + N_EXPERTS].set(w_re)
    wr_hi = wr.astype(BF16)
    wr_lo = (wr - wr_hi.astype(F32)).astype(BF16)
    br = jnp.zeros((1, LANES), F32).at[0, :N_GROUPS].set(b_rg).at[0, N_GROUPS:N_GROUPS + N_EXPERTS].set(b_re)
    row = lambda i: (i, 0)
    const = lambda i: (0, 0)
    modspec = lambda j: pl.BlockSpec((1, 1, 1, d), lambda i: (i // per_b, j, 0, 0))
    return pl.pallas_call(
        functools.partial(_out_kernel, da=da, db=db),
        grid=(n_tok // tm,),
        in_specs=[pl.BlockSpec((tm, da), row), pl.BlockSpec((tm, db), row),
                  pl.BlockSpec((dc, tm), lambda i: (0, i)), pl.BlockSpec((dc, tm), lambda i: (0, i)),
                  pl.BlockSpec((tm, d), row),
                  modspec(2), modspec(3), modspec(4),
                  pl.BlockSpec((1, d), const), pl.BlockSpec((1, dc), const),
                  pl.BlockSpec((dc, dc), const), pl.BlockSpec((1, dc), const),
                  pl.BlockSpec((d, d), const), pl.BlockSpec((d, LANES), const),
                  pl.BlockSpec((d, LANES), const), pl.BlockSpec((1, LANES), const)],
        out_specs=[pl.BlockSpec((tm, d), row), pl.BlockSpec((tm, d), row),
                   pl.BlockSpec((tm, LANES), row), pl.BlockSpec((tm, LANES), row),
                   pl.BlockSpec((1, 1, LANES), lambda i: (i, 0, 0))],
        out_shape=[jax.ShapeDtypeStruct((n_tok, d), F32), jax.ShapeDtypeStruct((n_tok, d), F32),
                   jax.ShapeDtypeStruct((n_tok, LANES), jnp.int32),
                   jax.ShapeDtypeStruct((n_tok, LANES), F32),
                   jax.ShapeDtypeStruct((n_tok // tm, 1, LANES), F32)],
        compiler_params=_cparams("parallel"),
        name="out_proj_router",
    )(ya, yb, yc, u, x2d, mod_l, mod_l, mod_l, g_ffn, s5_d.reshape(1, dc), glu_w.astype(BF16),
      glu_b.reshape(1, dc), w_out.astype(BF16), wr_hi, wr_lo, br)


ROW_UNROLL = 8


def _route(ri, hist, n_tok, tm):
    ntile = n_tok // tm
    h = hist.reshape(ntile, LANES)[:, N_GROUPS:N_GROUPS + N_EXPERTS].astype(jnp.int32)
    counts = jnp.sum(h, axis=0)
    pcounts = (counts + MOE_ROWS - 1) // MOE_ROWS * MOE_ROWS
    pends = jnp.cumsum(pcounts)
    base = (pends - pcounts)[None, :] + jnp.cumsum(h, axis=0) - h
    n_rows = 2 * n_tok + N_EXPERTS * MOE_ROWS
    nblk = n_rows // MOE_ROWS
    blk_start = jnp.arange(nblk, dtype=jnp.int32)[:, None] * MOE_ROWS
    blk_e = jnp.sum((pends[None, :] <= blk_start).astype(jnp.int32), axis=1)
    eid = (ri[:, 0:2] - N_GROUPS).reshape(ntile, tm, 2, 1)
    hot = eid == jnp.arange(N_EXPERTS, dtype=jnp.int32)
    dest = jnp.sum(jnp.where(hot, base[:, None, None, :], 0), axis=-1) + ri[:, 2:4].reshape(ntile, tm, 2)
    return (dest.reshape(n_tok, 2).astype(jnp.int32), blk_e.astype(jnp.int32), pends.astype(jnp.int32),
            pcounts.astype(jnp.int32), n_rows)


def _tile_rows(dest, tm):
    nblk = dest.shape[0] // tm
    return dest.reshape(nblk, tm, 2).transpose(0, 2, 1).reshape(nblk, 1, 2 * tm)


def _dispatch_kernel(pend_ref, pcnt_ref, d_ref, h_ref, xs_hbm, zbuf, hbuf, sem, zsem, *, tm, n_rows):
    i = pl.program_id(0)
    n = pl.num_programs(0)
    slot = i % 2

    @pl.when(i == 0)
    def _():
        zbuf[...] = jnp.zeros_like(zbuf)

        def zcopy(start):
            return pltpu.make_async_copy(zbuf, xs_hbm.at[pl.ds(start, MOE_ROWS)], zsem.at[0])

        first_tail = lax.shift_right_logical(pend_ref[N_EXPERTS - 1], MOE_ROWS.bit_length() - 1)
        nblk = n_rows // MOE_ROWS

        def tail_start(blk, carry):
            zcopy(pl.multiple_of(blk * MOE_ROWS, MOE_ROWS)).start()
            return carry

        def tail_wait(blk, carry):
            zcopy(0).wait()
            return carry

        for e in range(N_EXPERTS):
            @pl.when(pcnt_ref[e] > 0)
            def _():
                zcopy(pl.multiple_of(pend_ref[e] - MOE_ROWS, MOE_ROWS)).start()
        lax.fori_loop(first_tail, nblk, tail_start, 0)
        for e in range(N_EXPERTS):
            @pl.when(pcnt_ref[e] > 0)
            def _():
                zcopy(0).wait()
        lax.fori_loop(first_tail, nblk, tail_wait, 0)

    def body(g, carry):
        for u in range(ROW_UNROLL):
            r = g * ROW_UNROLL + u
            src = hbuf.at[slot, pl.ds(r, 1)]
            pltpu.make_async_copy(src, xs_hbm.at[pl.ds(d_ref[0, 0, r], 1)], sem.at[slot]).start(priority=0)
            pltpu.make_async_copy(src, xs_hbm.at[pl.ds(d_ref[0, 0, tm + r], 1)], sem.at[slot]).start(priority=1)
        return carry

    def drain(s):
        for _ in range(2):
            pltpu.make_async_copy(hbuf.at[s], xs_hbm.at[pl.ds(0, tm)], sem.at[s]).wait()

    hbuf[slot] = h_ref[...].reshape(tm, SUBLANES, LANES)
    lax.fori_loop(0, tm // ROW_UNROLL, body, 0)

    @pl.when(i > 0)
    def _():
        drain(1 - slot)

    @pl.when(i == n - 1)
    def _():
        drain(slot)


def _dispatch(h2, dest, pends, pcounts, n_rows, seq):
    n_tok, d = h2.shape
    tm = _row_tile(seq, 512)
    grid_spec = pltpu.PrefetchScalarGridSpec(
        num_scalar_prefetch=2,
        grid=(n_tok // tm,),
        in_specs=[pl.BlockSpec((1, 1, 2 * tm), lambda i, pe, pc: (i, 0, 0), memory_space=pltpu.SMEM),
                  pl.BlockSpec((tm, d), lambda i, pe, pc: (i, 0))],
        out_specs=pl.BlockSpec(memory_space=pl.ANY),
        scratch_shapes=[pltpu.VMEM((MOE_ROWS, SUBLANES, LANES), F32),
                        pltpu.VMEM((2, tm, SUBLANES, LANES), F32),
                        pltpu.SemaphoreType.DMA((2,)), pltpu.SemaphoreType.DMA((1,))],
    )
    assert d == SUBLANES * LANES
    return pl.pallas_call(
        functools.partial(_dispatch_kernel, tm=tm, n_rows=n_rows),
        grid_spec=grid_spec,
        out_shape=jax.ShapeDtypeStruct((n_rows, SUBLANES, LANES), F32),
        compiler_params=_cparams("arbitrary"),
        name="moe_dispatch",
    )(pends, pcounts, _tile_rows(dest, tm), h2)


def _moe_kernel(blk_e_ref, x_ref, w1_ref, w3_ref, w2_ref, y_ref, w1_sc, w3_sc, w2_sc):
    i = pl.program_id(0)
    e = blk_e_ref[i]
    used = e < N_EXPERTS

    @pl.when(used & ((i == 0) | (e != blk_e_ref[jnp.maximum(i - 1, 0)])))
    def _():
        w1_sc[...] = w1_ref[0, 0].astype(BF16)
        w3_sc[...] = w3_ref[0, 0].astype(BF16)
        w2_sc[...] = w2_ref[0, 0].astype(BF16)

    @pl.when(used)
    def _():
        rows = x_ref.shape[0]
        xb = x_ref[...].reshape(rows, SUBLANES * LANES).astype(BF16)
        act = (_silu(jnp.dot(xb, w1_sc[...], preferred_element_type=F32))
               * jnp.dot(xb, w3_sc[...], preferred_element_type=F32))
        y = jnp.dot(act.astype(BF16), w2_sc[...], preferred_element_type=F32)
        y_ref[...] = y.reshape(rows, SUBLANES, LANES)

    @pl.when(jnp.logical_not(used))
    def _():
        y_ref[...] = jnp.zeros_like(y_ref)


def _moe_experts(xs, blk_e, w1, w3, w2, layer):
    n_rows = xs.shape[0]
    d, d_exp = w1.shape[2], w1.shape[3]
    row_blk = pl.BlockSpec((MOE_ROWS, SUBLANES, LANES), lambda i, e: (i, 0, 0))
    wmap = lambda i, e: (layer, jnp.minimum(e[i], N_EXPERTS - 1), 0, 0)
    grid_spec = pltpu.PrefetchScalarGridSpec(
        num_scalar_prefetch=1,
        grid=(n_rows // MOE_ROWS,),
        in_specs=[row_blk,
                  pl.BlockSpec((1, 1, d, d_exp), wmap), pl.BlockSpec((1, 1, d, d_exp), wmap),
                  pl.BlockSpec((1, 1, d_exp, d), wmap)],
        out_specs=row_blk,
        scratch_shapes=[pltpu.VMEM((d, d_exp), BF16), pltpu.VMEM((d, d_exp), BF16),
                        pltpu.VMEM((d_exp, d), BF16)],
    )
    return pl.pallas_call(
        _moe_kernel,
        grid_spec=grid_spec,
        out_shape=jax.ShapeDtypeStruct((n_rows, SUBLANES, LANES), F32),
        compiler_params=_cparams("arbitrary"),
        name="moe_experts",
    )(blk_e, xs, w1, w3, w2)


def _comb_kernel(d_ref, d_next_ref, x_ref, rw_ref, gt_ref, y_hbm, g_ref, o_ref, ybuf, sem, *, tm, final):
    i = pl.program_id(0)
    n = pl.num_programs(0)
    slot = i % 2

    def start_all(idx_ref, s):
        def body(g, carry):
            for u in range(ROW_UNROLL):
                r = g * ROW_UNROLL + u
                pltpu.make_async_copy(y_hbm.at[pl.ds(idx_ref[0, 0, r], 1)], ybuf.at[s, pl.ds(r, 1)],
                                      sem.at[s]).start(priority=u % 2)
            return carry
        lax.fori_loop(0, 2 * tm // ROW_UNROLL, body, 0)

    @pl.when(i == 0)
    def _():
        start_all(d_ref, 0)

    @pl.when(i + 1 < n)
    def _():
        start_all(d_next_ref, 1 - slot)

    pltpu.make_async_copy(y_hbm.at[pl.ds(0, 2 * tm)], ybuf.at[slot], sem.at[slot]).wait()

    w = rw_ref[...]
    d = x_ref.shape[1]
    moe = (w[:, 0:1] * ybuf[slot, 0:tm].reshape(tm, d)
           + w[:, 1:2] * ybuf[slot, tm:2 * tm].reshape(tm, d))
    x2 = x_ref[...] + (1.0 + gt_ref[0, 0]) * moe
    o_ref[...] = _rmsnorm(x2, g_ref[...]) if final else x2


def _combine(x1, y_rows, dest, rw, mod_l, g_final, seq, final):
    n_tok, d = x1.shape
    tm = _row_tile(seq, 512)
    per_b = seq // tm
    nblk = n_tok // tm
    idx = _tile_rows(dest, tm)
    return pl.pallas_call(
        functools.partial(_comb_kernel, tm=tm, final=final),
        grid=(nblk,),
        in_specs=[pl.BlockSpec((1, 1, 2 * tm), lambda i: (i, 0, 0), memory_space=pltpu.SMEM),
                  pl.BlockSpec((1, 1, 2 * tm), lambda i: (jnp.minimum(i + 1, nblk - 1), 0, 0),
                               memory_space=pltpu.SMEM),
                  pl.BlockSpec((tm, d), lambda i: (i, 0)),
                  pl.BlockSpec((tm, LANES), lambda i: (i, 0)),
                  pl.BlockSpec((1, 1, 1, d), lambda i: (i // per_b, 5, 0, 0)),
                  pl.BlockSpec(memory_space=pl.ANY),
                  pl.BlockSpec((1, d), lambda i: (0, 0))],
        out_specs=pl.BlockSpec((tm, d), lambda i: (i, 0)),
        out_shape=jax.ShapeDtypeStruct((n_tok, d), F32),
        scratch_shapes=[pltpu.VMEM((2, 2 * tm, SUBLANES, LANES), F32), pltpu.SemaphoreType.DMA((2,))],
        compiler_params=_cparams("arbitrary"),
        name="moe_combine",
    )(idx, idx, x1, rw, mod_l, y_rows, g_final)


def kernel(x, c, ada_w, ada_b, norm_mix, norm_ffn, norm_final, w_in, w_out, mlstm_conv_w, mlstm_conv_b, mlstm_w_q, mlstm_w_k, mlstm_b_i, mlstm_b_f, mlstm_norm_w, rwkv_mu, rwkv_w0, rwkv_w_up, rwkv_a0, rwkv_a_up, rwkv_g_up, rwkv_k_k, rwkv_k_a, rwkv_r_k, rwkv_ln_w, rwkv_ln_b, rwkv_v0, rwkv_v_dn, rwkv_v_up, s5_a_re, s5_a_im, s5_log_dt, s5_b_re, s5_b_im, s5_c_re, s5_c_im, s5_d, s5_glu_w, s5_glu_b, moe_w_rg, moe_b_rg, moe_w_re, moe_b_re, moe_w1, moe_w3, moe_w2):
    bsz, seq, d = x.shape
    depth = ada_w.shape[0]
    n_tok = bsz * seq
    heads_a = mlstm_w_q.shape[1]
    da = heads_a * HEAD_DIM
    db = rwkv_w0.shape[1]
    dc = s5_d.shape[1]
    rw_cols = rwkv_mu.shape[1]
    assert seq % CHUNK == 0 and w_in.shape[2] == 3 * da + 2 * heads_a + rw_cols + dc
    assert 2 * HEAD_DIM == LANES and CHUNK == HEAD_DIM and EXPERTS_PER_GROUP == 8
    widths = (3 * da, 2 * LANES, rw_cols, dc)

    mod = _modulation(c, ada_w, ada_b).reshape(depth, bsz, 6, 1, d)
    xc = x.reshape(n_tok, d)
    v_first = None
    for l in range(depth):
        mod_l = mod[l]
        qkvo, gates, pcols, u = _in_proj(xc, mod_l, norm_mix[l].reshape(1, d), w_in, l, widths, da,
                                         heads_a, seq)

        ya = _mlstm(qkvo, gates, mlstm_conv_w[l], mlstm_conv_b[l], mlstm_w_q[l], mlstm_w_k[l],
                    mlstm_b_i[l], mlstm_b_f[l], mlstm_norm_w[l], bsz, seq)
        rprm = dict(mu=rwkv_mu[l], w0=rwkv_w0[l], w_up=rwkv_w_up[l], a0=rwkv_a0[l], a_up=rwkv_a_up[l],
                    g_up=rwkv_g_up[l], k_k=rwkv_k_k[l], k_a=rwkv_k_a[l], r_k=rwkv_r_k[l],
                    ln_w=rwkv_ln_w[l], ln_b=rwkv_ln_b[l])
        if l > 0:
            rprm.update(v0=rwkv_v0[l - 1], v_dn=rwkv_v_dn[l - 1], v_up=rwkv_v_up[l - 1])
        yb, v_first = _rwkv(pcols, v_first if l > 0 else None, rprm, bsz, seq)
        sprm = dict(a_re=s5_a_re[l], a_im=s5_a_im[l], log_dt=s5_log_dt[l], b_re=s5_b_re[l],
                    b_im=s5_b_im[l], c_re=s5_c_re[l], c_im=s5_c_im[l])
        yc = _s5_core(u, sprm, bsz, seq)

        x1, h2, ri, rw, hist = _out_proj(ya, yb, yc, u, xc, mod_l, norm_ffn[l].reshape(1, d), s5_d[l],
                                         s5_glu_w[l], s5_glu_b[l], w_out[l], moe_w_rg[l], moe_b_rg[l],
                                         moe_w_re[l], moe_b_re[l], seq)
        dest, blk_e, pends, pcounts, n_rows = _route(ri, hist, n_tok, n_tok // hist.shape[0])
        y_rows = _moe_experts(_dispatch(h2, dest, pends, pcounts, n_rows, seq), blk_e, moe_w1, moe_w3,
                              moe_w2, l)
        xc = _combine(x1, y_rows, dest, rw, mod_l, norm_final.reshape(1, d), seq, final=(l == depth - 1))
    return xc.reshape(bsz, seq, d).astype(x.dtype)
```

```python
import functools

import jax
import jax.numpy as jnp
from jax import lax
from jax.experimental import pallas as pl
from jax.experimental.pallas import tpu as pltpu

F32 = jnp.float32
BF16 = jnp.bfloat16

HEAD_DIM = 64
CHUNK = 64
CONV_K = 4
S5_GC = 16
S5_P = 64
N_GROUPS = 4
EXPERTS_PER_GROUP = 8
N_EXPERTS = N_GROUPS * EXPERTS_PER_GROUP
NORM_EPS = 1e-6
HEAD_NORM_EPS = 1e-5
RWKV_GN_EPS = 64e-5
L2_EPS = 1e-12
LANES = 128
SUBLANES = 8
MOE_ROWS = 512
VMEM_LIMIT = 56 * 1024 * 1024


def _cparams(*sem):
    return pltpu.CompilerParams(dimension_semantics=sem, vmem_limit_bytes=VMEM_LIMIT)


def _row_tile(n, want):
    t = min(n, want)
    assert n % t == 0
    return t


def _dot(a, b):
    return jnp.dot(a.astype(BF16), b.astype(BF16), preferred_element_type=F32)


def _dot_nt(a, b):
    return lax.dot_general(a.astype(BF16), b.astype(BF16), (((1,), (1,)), ((), ())),
                           preferred_element_type=F32)


def _dot_tn(a, b):
    return lax.dot_general(a.astype(BF16), b.astype(BF16), (((0,), (0,)), ((), ())),
                           preferred_element_type=F32)


def _split(a):
    hi = a.astype(BF16)
    lo = (a - hi.astype(F32)).astype(BF16)
    return hi, lo


def _dot_xa(a, b_exact):
    hi, lo = _split(a)
    return (jnp.dot(hi, b_exact, preferred_element_type=F32)
            + jnp.dot(lo, b_exact, preferred_element_type=F32))


def _dot_xb(a_exact, b):
    hi, lo = _split(b)
    return (jnp.dot(a_exact, hi, preferred_element_type=F32)
            + jnp.dot(a_exact, lo, preferred_element_type=F32))


def _dot_x3(a, b_exact):
    hi = a.astype(BF16)
    r1 = a - hi.astype(F32)
    mid = r1.astype(BF16)
    lo = (r1 - mid.astype(F32)).astype(BF16)
    return (jnp.dot(hi, b_exact, preferred_element_type=F32)
            + jnp.dot(mid, b_exact, preferred_element_type=F32)
            + jnp.dot(lo, b_exact, preferred_element_type=F32))


def _dot3(a, b):
    ah, al = _split(a)
    bh, bl = _split(b)
    return (jnp.dot(ah, bh, preferred_element_type=F32)
            + jnp.dot(ah, bl, preferred_element_type=F32)
            + jnp.dot(al, bh, preferred_element_type=F32))


def _sigmoid(x):
    return 1.0 / (1.0 + jnp.exp(-x))


def _silu(x):
    return x * _sigmoid(x)


def _log_sigmoid(x):
    return jnp.minimum(x, 0.0) - jnp.log1p(jnp.exp(-jnp.abs(x)))


def _rmsnorm(x, g):
    ms = jnp.mean(x * x, axis=-1, keepdims=True)
    return x * lax.rsqrt(ms + NORM_EPS) * g


def _tri_incl(n):
    r = lax.broadcasted_iota(jnp.int32, (n, n), 0)
    c = lax.broadcasted_iota(jnp.int32, (n, n), 1)
    return (c <= r).astype(BF16)


def _head_ones(width):
    r = lax.broadcasted_iota(jnp.int32, (width, width), 0) // HEAD_DIM
    c = lax.broadcasted_iota(jnp.int32, (width, width), 1) // HEAD_DIM
    return (r == c).astype(BF16)


def _mod_kernel(c_ref, w_ref, b_ref, o_ref):
    o_ref[0] = _dot(_silu(c_ref[...]), w_ref[0]) + b_ref[0]


def _modulation(c, ada_w, ada_b):
    depth, d, d6 = ada_w.shape
    bsz = c.shape[0]
    tn = _row_tile(d6, 1024)
    return pl.pallas_call(
        _mod_kernel,
        grid=(depth, d6 // tn),
        in_specs=[pl.BlockSpec((bsz, d), lambda l, j: (0, 0)),
                  pl.BlockSpec((1, d, tn), lambda l, j: (l, 0, j)),
                  pl.BlockSpec((1, 1, tn), lambda l, j: (l, 0, j))],
        out_specs=pl.BlockSpec((1, bsz, tn), lambda l, j: (l, 0, j)),
        out_shape=jax.ShapeDtypeStruct((depth, bsz, d6), F32),
        compiler_params=_cparams("parallel", "parallel"),
        name="adaln_mod",
    )(c, ada_w, ada_b.reshape(depth, 1, d6))


def _in_kernel(x_ref, sh_ref, sc_ref, g_ref, w_ref, *refs, widths, da, heads):
    out_refs, w_sc = refs[:-1], refs[-1]
    d = w_ref.shape[1]
    rw_cols, dc = widths[2], widths[3]

    @pl.when(pl.program_id(0) == 0)
    def _():
        rb = 256
        g0 = 3 * da
        r_src = g0 + 2 * heads
        u_src = r_src + rw_cols
        r_win = -(-(2 * heads + rw_cols) // LANES) * LANES
        u_al = u_src // LANES * LANES
        lane = lax.broadcasted_iota(jnp.int32, (rb, LANES), 1)
        for r0 in range(0, d, rb):
            rs = slice(r0, r0 + rb)
            w_sc[rs, 0:g0] = w_ref[0, rs, 0:g0].astype(BF16)
            gblk = w_ref[0, rs, g0:g0 + LANES]
            w_sc[rs, g0:g0 + LANES] = jnp.where(lane < heads, gblk, 0.0).astype(BF16)
            w_sc[rs, g0 + LANES:g0 + 2 * LANES] = jnp.where(
                lane < heads, pltpu.roll(gblk, LANES - heads, 1), 0.0).astype(BF16)
            win = w_ref[0, rs, g0:g0 + r_win]
            w_sc[rs, g0 + 2 * LANES:g0 + 2 * LANES + rw_cols] = win[:, 2 * heads:2 * heads + rw_cols].astype(BF16)
            win2 = w_ref[0, rs, u_al:u_src + dc]
            w_sc[rs, g0 + 2 * LANES + rw_cols:] = win2[:, u_src - u_al:u_src - u_al + dc].astype(BF16)

    h = _rmsnorm(x_ref[...], g_ref[...]) * (1.0 + sc_ref[0, 0]) + sh_ref[0, 0]
    hb = h.astype(BF16)
    off = 0
    for k, (o_ref, wd) in enumerate(zip(out_refs, widths)):
        res = jnp.dot(hb, w_sc[:, off:off + wd], preferred_element_type=F32)
        o_ref[...] = res.T if k == len(widths) - 1 else res
        off += wd


def _in_proj(x2d, mod_l, g, w_in, layer, widths, da, heads, seq):
    n_tok, d = x2d.shape
    cols = w_in.shape[2]
    tm = _row_tile(seq, 1024)
    per_b = seq // tm
    return pl.pallas_call(
        functools.partial(_in_kernel, widths=widths, da=da, heads=heads),
        grid=(n_tok // tm,),
        in_specs=[pl.BlockSpec((tm, d), lambda i: (i, 0)),
                  pl.BlockSpec((1, 1, 1, d), lambda i: (i // per_b, 0, 0, 0)),
                  pl.BlockSpec((1, 1, 1, d), lambda i: (i // per_b, 1, 0, 0)),
                  pl.BlockSpec((1, d), lambda i: (0, 0)),
                  pl.BlockSpec((1, d, cols), lambda i: (layer, 0, 0), pipeline_mode=pl.Buffered(1))],
        out_specs=[pl.BlockSpec((tm, wd), lambda i: (i, 0)) for wd in widths[:-1]]
        + [pl.BlockSpec((widths[-1], tm), lambda i: (0, i))],
        out_shape=[jax.ShapeDtypeStruct((n_tok, wd), F32) for wd in widths[:-1]]
        + [jax.ShapeDtypeStruct((widths[-1], n_tok), F32)],
        scratch_shapes=[pltpu.VMEM((d, sum(widths)), BF16)],
        compiler_params=_cparams("arbitrary"),
        name="in_proj",
    )(x2d, mod_l, mod_l, g, w_in)


def _mlstm_kernel(qkvo_ref, gate_ref, cw_ref, cb_ref, wq_ref, wk_ref, gb_ref, nw_ref, out_ref,
                  xf_sc, q_sc, k_sc, gi_sc, gf_sc, cn_sc, m_sc, *, nb, tb, heads):
    dh, L = HEAD_DIM, CHUNK
    da = heads * dh
    i = pl.program_id(0)

    @pl.when(i == 0)
    def _():
        xf_sc[:, 0:SUBLANES, :] = jnp.zeros((nb, SUBLANES, da), F32)
        cn_sc[...] = jnp.zeros_like(cn_sc)
        m_sc[...] = jnp.zeros_like(m_sc)

    for b in range(nb):
        xqk = qkvo_ref[b, :, 0:da]
        xf_sc[b, pl.ds(SUBLANES, tb), :] = xqk
        acc = xqk * cw_ref[CONV_K - 1:CONV_K, :] + cb_ref[...]
        for j in range(1, CONV_K):
            acc = acc + xf_sc[b, pl.ds(SUBLANES - j, tb), :] * cw_ref[CONV_K - 1 - j:CONV_K - j, :]
        xf_sc[b, 0:SUBLANES, :] = xf_sc[b, pl.ds(tb, SUBLANES), :]
        cx = _silu(acc).astype(BF16)
        q_sc[b] = jnp.dot(cx, wq_ref[...], preferred_element_type=F32)
        k_sc[b] = jnp.dot(cx, wk_ref[...], preferred_element_type=F32) * (dh ** -0.5)

        g = gate_ref[b] + gb_ref[...]
        gi_sc[b] = g[:, :LANES]
        gf_sc[b] = _log_sigmoid(g[:, LANES:])

    tri = _tri_incl(L)
    rr = lax.broadcasted_iota(jnp.int32, (L, L), 0)
    cc = lax.broadcasted_iota(jnp.int32, (L, L), 1)
    causal = cc <= rr
    row_l = lax.broadcasted_iota(jnp.int32, (L, LANES), 0)
    sel = (lax.broadcasted_iota(jnp.int32, (LANES, heads * LANES), 0)
           == lax.broadcasted_iota(jnp.int32, (LANES, heads * LANES), 1) // LANES).astype(BF16)
    ones_v = jnp.ones((L, dh), F32)
    mean_m = jnp.full((dh, dh), 1.0 / dh, BF16)

    units = [(b, h) for b in range(nb) for h in range(heads)]
    ur = range(len(units))

    def hcol(h, base=0):
        return slice(base + h * dh, base + (h + 1) * dh)

    def chunk(c, carry):
        rows = pl.ds(pl.multiple_of(c * L, L), L)
        ex, g_t = [], []
        for b in range(nb):
            bc = _dot_xb(tri, gf_sc[b, rows, :])
            gtot = bc[L - 1:L, :]
            g = gi_sc[b, rows, :] - bc
            gmax = jnp.max(g, axis=0, keepdims=True)
            m_prev = m_sc[b]
            m_loc = gtot + gmax
            m_new = jnp.maximum(gtot + m_prev, m_loc)
            a_old = jnp.exp(gtot + m_prev - m_new)
            a_loc = jnp.exp(m_loc - m_new)
            m_sc[b] = m_new
            pm = g
            d = 1
            while d < L:
                pm = jnp.maximum(pm, jnp.where(row_l >= d, pltpu.roll(pm, d, 0), -jnp.inf))
                d *= 2
            mm = jnp.maximum(m_prev, pm)
            stack = jnp.concatenate([jnp.exp(g - gmax), mm, jnp.exp(m_prev - mm), jnp.exp(-(bc + mm)),
                                     a_old, a_loc, jnp.zeros((SUBLANES - 2, LANES), F32)], axis=0)
            ex.append(_dot_x3(stack, sel))
            g_t.append(g.T)

        def part(k, b, h, width=dh):
            return ex[b][k * L:(k + 1) * L, h * LANES:h * LANES + width]

        qc = [q_sc[b, rows, hcol(h)] for b, h in units]
        kc = [k_sc[b, rows, hcol(h)] for b, h in units]
        vo = [jnp.concatenate([qkvo_ref[b, rows, hcol(h, da)], ones_v], axis=1) for b, h in units]
        oc = [qkvo_ref[b, rows, hcol(h, 2 * da)] for b, h in units]
        cn_prev = [cn_sc[b, h] for b, h in units]
        s_raw = [_dot_nt(qc[u], kc[u]) for u in ur]
        q_cn = [_dot(qc[u], cn_prev[u]) for u in ur]
        cn_loc = [_dot_tn(kc[u] * part(0, b, h), vo[u]) for u, (b, h) in enumerate(units)]
        s_qk = [s_raw[u] * jnp.where(causal, jnp.exp(g_t[b][h:h + 1, :] - part(1, b, h)), 0.0)
                for u, (b, h) in enumerate(units)]
        s_vn = [_dot(s_qk[u], vo[u]) for u in ur]
        nd = [part(2, b, h, 2 * dh) * q_cn[u] + s_vn[u] for u, (b, h) in enumerate(units)]
        hh = [nd[u][:, :dh] / jnp.maximum(jnp.abs(nd[u][:, dh:]), part(3, b, h))
              for u, (b, h) in enumerate(units)]
        mu = [_dot_xa(hh[u], mean_m) for u in ur]
        dlt = [hh[u] - mu[u] for u in ur]
        var = [_dot_xa(dlt[u] * dlt[u], mean_m) for u in ur]
        outs = [dlt[u] * lax.rsqrt(var[u] + HEAD_NORM_EPS) * _sigmoid(oc[u]) for u in ur]
        for b in range(nb):
            mine = [u for u in ur if units[u][0] == b]
            out_ref[b, rows, :] = jnp.concatenate([outs[u] for u in mine], axis=1) * nw_ref[...]
        for u, (b, h) in enumerate(units):
            cn_sc[b, h] = (ex[b][4 * L:4 * L + 1, h * LANES:(h + 1) * LANES] * cn_prev[u]
                           + ex[b][4 * L + 1:4 * L + 2, h * LANES:(h + 1) * LANES] * cn_loc[u])
        return carry

    lax.fori_loop(0, tb // L, chunk, 0)


def _block_diag(w):
    heads, dh, _ = w.shape
    eye = jnp.eye(heads, dtype=w.dtype)
    return (eye[:, None, :, None] * w[:, :, None, :]).reshape(heads * dh, heads * dh)


def _mlstm(qkvo, gates, conv_w, conv_b, w_q, w_k, b_i, b_f, norm_w, bsz, seq):
    heads = w_q.shape[0]
    da = heads * HEAD_DIM
    tb = _row_tile(seq, 256)
    nblk = seq // tb
    gbias = jnp.zeros((1, 2 * LANES), F32).at[0, :heads].set(b_i).at[0, LANES:LANES + heads].set(b_f)
    kern = functools.partial(_mlstm_kernel, nb=bsz, tb=tb, heads=heads)
    row = lambda i: (0, i, 0)
    const = lambda i: (0, 0)
    out = pl.pallas_call(
        kern,
        grid=(nblk,),
        in_specs=[pl.BlockSpec((bsz, tb, 3 * da), row),
                  pl.BlockSpec((bsz, tb, 2 * LANES), row),
                  pl.BlockSpec((CONV_K, da), const),
                  pl.BlockSpec((1, da), const),
                  pl.BlockSpec((da, da), const),
                  pl.BlockSpec((da, da), const),
                  pl.BlockSpec((1, 2 * LANES), const),
                  pl.BlockSpec((1, da), const)],
        out_specs=pl.BlockSpec((bsz, tb, da), row),
        out_shape=jax.ShapeDtypeStruct((bsz, seq, da), F32),
        scratch_shapes=[pltpu.VMEM((bsz, tb + SUBLANES, da), F32),
                        pltpu.VMEM((bsz, tb, da), F32),
                        pltpu.VMEM((bsz, tb, da), F32),
                        pltpu.VMEM((bsz, tb, LANES), F32),
                        pltpu.VMEM((bsz, tb, LANES), F32),
                        pltpu.VMEM((bsz, heads, HEAD_DIM, 2 * HEAD_DIM), F32),
                        pltpu.VMEM((bsz, 1, LANES), F32)],
        compiler_params=_cparams("arbitrary"),
        name="mlstm",
    )(qkvo.reshape(bsz, seq, 3 * da), gates.reshape(bsz, seq, 2 * LANES), conv_w, conv_b.reshape(1, da),
      _block_diag(w_q).astype(BF16), _block_diag(w_k).astype(BF16), gbias, norm_w.reshape(1, da))
    return out.reshape(bsz * seq, da)


def _rwkv_kernel(*refs, nb, tb, heads, lw_dim, la_dim, lg_dim, has_vres):
    dh, L = HEAD_DIM, CHUNK
    db = heads * dh
    it = iter(refs)
    p_ref = next(it)
    vf_ref = next(it) if has_vres else None
    (mu_ref, w0_ref, wup_ref, a0_ref, aup_ref, gup_ref, kk_ref, ka_ref, rk_ref,
     lnw_ref, lnb_ref) = (next(it) for _ in range(11))
    if has_vres:
        v0_ref, vdn_ref, vup_ref = (next(it) for _ in range(3))
    y_ref = next(it)
    vout_ref = None if has_vres else next(it)
    (xf_sc, r_sc, k_sc, v_sc, a_sc, b_sc, lw_sc, y_sc, gate_sc, rp_sc, st_sc, q_sc, z_sc,
     gl_sc) = (next(it) for _ in range(14))

    i = pl.program_id(0)

    @pl.when(i == 0)
    def _():
        xf_sc[:, 0:SUBLANES, :] = jnp.zeros((nb, SUBLANES, xf_sc.shape[2]), F32)
        st_sc[...] = jnp.zeros_like(st_sc)

    ones_h = _head_ones(db)
    for b in range(nb):
        p = p_ref[b]
        xf_sc[b, pl.ds(SUBLANES, tb), :] = p
        prev = xf_sc[b, pl.ds(SUBLANES - 1, tb), :]
        xf_sc[b, 0:SUBLANES, :] = xf_sc[b, pl.ds(tb, SUBLANES), :]
        p = p + mu_ref[...] * (prev - p)

        o = 0
        r = p[:, o:o + db]; o += db
        k = p[:, o:o + db]; o += db
        v = p[:, o:o + db]; o += db
        wd = p[:, o:o + lw_dim]; o += lw_dim
        ad = p[:, o:o + la_dim]; o += la_dim
        gd = p[:, o:o + lg_dim]

        wlog = _log_sigmoid(w0_ref[...] + _dot(jnp.tanh(wd), wup_ref[...])) - 0.5
        lw_sc[b] = -jnp.exp(wlog)
        a = _sigmoid(a0_ref[...] + _dot(ad, aup_ref[...]))
        gate_sc[b] = _dot(_sigmoid(gd), gup_ref[...])
        if has_vres:
            v = v + (vf_ref[b] - v) * _sigmoid(v0_ref[...] + _dot(_dot(v, vdn_ref[...]), vup_ref[...]))
        else:
            vout_ref[b] = v
        kk = k * kk_ref[...]
        kk = kk / jnp.maximum(jnp.sqrt(_dot_xa(kk * kk, ones_h)), L2_EPS)
        r_sc[b] = r
        k_sc[b] = k * (1.0 + (a - 1.0) * ka_ref[...])
        v_sc[b] = v
        a_sc[b] = -kk
        b_sc[b] = kk * a

    tri = _tri_incl(L)
    rr = lax.broadcasted_iota(jnp.int32, (L, L), 0)
    cc = lax.broadcasted_iota(jnp.int32, (L, L), 1)
    strict = cc < rr
    incl = cc <= rr

    units = [(b, h) for b in range(nb) for h in range(heads)]
    ur = range(len(units))

    def hcol(h):
        return slice(h * dh, (h + 1) * dh)

    def chunk(c, carry):
        rows = pl.ds(pl.multiple_of(c * L, L), L)
        at, rt, bt, kt, bg, kg, vch = ([] for _ in range(7))
        for b in range(nb):
            lwc = lw_sc[b, rows, :]
            cum = _dot_xb(tri, lwc)
            cum_l = cum[L - 1:L, :]
            e_inv = jnp.exp(-cum)
            e_end = jnp.exp(cum_l - cum)
            bv = b_sc[b, rows, :]
            kv = k_sc[b, rows, :]
            at.append(a_sc[b, rows, :] * jnp.exp(cum - lwc))
            rt.append(r_sc[b, rows, :] * jnp.exp(cum))
            bt.append(bv * e_inv)
            kt.append(kv * e_inv)
            bg.append(bv * e_end)
            kg.append(kv * e_end)
            vch.append(v_sc[b, rows, :])
            gl_sc[c, b] = jnp.exp(cum_l)
        vh = [vch[b][:, hcol(h)] for b, h in units]
        g4 = [_dot_nt(jnp.concatenate([at[b][:, hcol(h)], rt[b][:, hcol(h)]], axis=0),
                      jnp.concatenate([bt[b][:, hcol(h)], kt[b][:, hcol(h)]], axis=0))
              for b, h in units]
        pw = [jnp.where(strict, g4[u][:L, :L], 0.0) for u in ur]
        n_ak = [jnp.where(strict, g4[u][:L, L:], 0.0) for u in ur]
        m_rb = [jnp.where(incl, g4[u][L:, :L], 0.0) for u in ur]
        m_rk = [jnp.where(incl, g4[u][L:, L:], 0.0) for u in ur]
        nv = [_dot(jnp.concatenate([n_ak[u], m_rk[u]], axis=0), vh[u]) for u in ur]
        x = [jnp.concatenate([at[b][:, hcol(h)], nv[u][:L]], axis=1)
             for u, (b, h) in enumerate(units)]
        for step in range(6):
            if step < 5:
                px = [_dot(pw[u], jnp.concatenate([x[u], pw[u]], axis=1)) for u in ur]
                x = [x[u] + px[u][:, :2 * dh] for u in ur]
                pw = [px[u][:, 2 * dh:] for u in ur]
            else:
                x = [x[u] + _dot(pw[u], x[u]) for u in ur]
        ry = [jnp.concatenate([rt[b][:, hcol(h)], nv[u][L:]], axis=1) + _dot(m_rb[u], x[u])
              for u, (b, h) in enumerate(units)]
        qz = [_dot_tn(x[u], bg[b][:, hcol(h)]) for u, (b, h) in enumerate(units)]
        z2 = [_dot_tn(vh[u], kg[b][:, hcol(h)]) for u, (b, h) in enumerate(units)]
        for b in range(nb):
            mine = [u for u in ur if units[u][0] == b]
            rp_sc[b, rows, :] = jnp.concatenate([ry[u][:, :dh] for u in mine], axis=1)
            y_sc[b, rows, :] = jnp.concatenate([ry[u][:, dh:] for u in mine], axis=1)
        for u, (b, h) in enumerate(units):
            q_sc[c, b, h] = qz[u][:dh]
            z_sc[c, b, h] = qz[u][dh:] + z2[u]
        return carry

    lax.fori_loop(0, tb // L, chunk, 0)

    def carry_state(c, carry):
        rows = pl.ds(pl.multiple_of(c * L, L), L)
        st = [st_sc[b, h] for b, h in units]
        ys = [_dot_nt(rp_sc[b, rows, hcol(h)], st[u]) for u, (b, h) in enumerate(units)]
        sq = [_dot_xa(st[u], q_sc[c, b, h].astype(BF16)) for u, (b, h) in enumerate(units)]
        for b in range(nb):
            mine = [u for u in ur if units[u][0] == b]
            y_sc[b, rows, :] = y_sc[b, rows, :] + jnp.concatenate([ys[u] for u in mine], axis=1)
        for u, (b, h) in enumerate(units):
            st_sc[b, h] = st[u] * gl_sc[c, b][:, hcol(h)] + sq[u] + z_sc[c, b, h]
        return carry

    lax.fori_loop(0, tb // L, carry_state, 0)

    for b in range(nb):
        y = y_sc[b]
        mean = _dot_xa(y, ones_h) * (1.0 / dh)
        dlt = y - mean
        var = _dot_xa(dlt * dlt, ones_h) * (1.0 / dh)
        yn = dlt * lax.rsqrt(var + RWKV_GN_EPS) * lnw_ref[...] + lnb_ref[...]
        bonus = _dot_xa(r_sc[b] * k_sc[b] * rk_ref[...], ones_h) * v_sc[b]
        y_ref[b] = (yn + bonus) * gate_sc[b]


def _rwkv(pcols, v_first, prm, bsz, seq):
    db = prm["w0"].shape[0]
    heads = db // HEAD_DIM
    cols = pcols.shape[1]
    lw_dim, la_dim, lg_dim = prm["w_up"].shape[0], prm["a_up"].shape[0], prm["g_up"].shape[0]
    has_vres = v_first is not None
    tb = _row_tile(seq, 256)
    nblk = seq // tb
    row = lambda i: (0, i, 0)
    const = lambda i: (0, 0)
    vec = lambda a: a.reshape(1, -1).astype(F32)

    args = [pcols.reshape(bsz, seq, cols)]
    specs = [pl.BlockSpec((bsz, tb, cols), row)]
    if has_vres:
        args.append(v_first.reshape(bsz, seq, db))
        specs.append(pl.BlockSpec((bsz, tb, db), row))
    small = [vec(prm["mu"]), vec(prm["w0"]), prm["w_up"].astype(BF16), vec(prm["a0"]),
             prm["a_up"].astype(BF16), prm["g_up"].astype(BF16), vec(prm["k_k"]), vec(prm["k_a"]),
             vec(prm["r_k"]), vec(prm["ln_w"]), vec(prm["ln_b"])]
    if has_vres:
        lv = prm["v_dn"].shape[1]
        lvp = -(-lv // LANES) * LANES
        v_dn = jnp.zeros((db, lvp), F32).at[:, :lv].set(prm["v_dn"]).astype(BF16)
        v_up = jnp.zeros((lvp, db), F32).at[:lv, :].set(prm["v_up"]).astype(BF16)
        small += [vec(prm["v0"]), v_dn, v_up]
    args += small
    specs += [pl.BlockSpec(a.shape, const) for a in small]

    out_shape = [jax.ShapeDtypeStruct((bsz, seq, db), F32)]
    out_specs = [pl.BlockSpec((bsz, tb, db), row)]
    if not has_vres:
        out_shape.append(jax.ShapeDtypeStruct((bsz, seq, db), F32))
        out_specs.append(pl.BlockSpec((bsz, tb, db), row))

    kern = functools.partial(_rwkv_kernel, nb=bsz, tb=tb, heads=heads, lw_dim=lw_dim, la_dim=la_dim,
                             lg_dim=lg_dim, has_vres=has_vres)
    nch = tb // CHUNK
    res = pl.pallas_call(
        kern,
        grid=(nblk,),
        in_specs=specs,
        out_specs=out_specs,
        out_shape=out_shape,
        scratch_shapes=[pltpu.VMEM((bsz, tb + SUBLANES, cols), F32)]
        + [pltpu.VMEM((bsz, tb, db), F32) for _ in range(9)]
        + [pltpu.VMEM((bsz, heads, HEAD_DIM, HEAD_DIM), F32),
           pltpu.VMEM((nch, bsz, heads, HEAD_DIM, HEAD_DIM), F32),
           pltpu.VMEM((nch, bsz, heads, HEAD_DIM, HEAD_DIM), F32),
           pltpu.VMEM((nch, bsz, 1, db), F32)],
        compiler_params=_cparams("arbitrary"),
        name="rwkv7",
    )(*args)
    res = [a.reshape(bsz * seq, db) for a in res]
    return (res[0], v_first) if has_vres else (res[0], res[1])


def _cmul(ar, ai, br, bi):
    return ar * br - ai * bi, ar * bi + ai * br


def _shift_rows(x, d):
    row = lax.broadcasted_iota(jnp.int32, x.shape, 0)
    return jnp.where(row >= d, pltpu.roll(x, d, 0), 0.0)


def _s5_kernel(u_ref, are_r, aim_r, dt_r, are_c, aim_c, dt_c, bre_ref, bim_ref, cre_ref, cim_ref,
               y_ref, m_sc, *, bsz, nchunk):
    L, gc, P = CHUNK, S5_GC, S5_P
    n = L * gc
    a_re, a_im, dt = are_r[0], aim_r[0], jnp.exp(dt_r[0])
    mag, ang = jnp.exp(a_re * dt), a_im * dt
    ab_re, ab_im = mag * jnp.cos(ang), mag * jnp.sin(ang)
    inv = 1.0 / (a_re * a_re + a_im * a_im)
    co_re = ((ab_re - 1.0) * a_re + ab_im * a_im) * inv
    co_im = (ab_im * a_re - (ab_re - 1.0) * a_im) * inv
    bb_re, bb_im = _cmul(co_re, co_im, bre_ref[0], bim_ref[0])
    c_re, c_im = cre_ref[0], cim_ref[0]

    def powers(tau):
        m = jnp.exp(tau * (a_re * dt))
        return m * jnp.cos(tau * ang), m * jnp.sin(tau * ang)

    a_re_c, a_im_c, dt_c_ = are_c[0], aim_c[0], jnp.exp(dt_c[0])
    tau_row = lax.broadcasted_iota(jnp.int32, (P, L), 1).astype(F32)
    pm = jnp.exp(tau_row * (a_re_c * dt_c_))
    pt_re = pm * jnp.cos(tau_row * (a_im_c * dt_c_))
    pt_im = pm * jnp.sin(tau_row * (a_im_c * dt_c_))
    pair = lax.broadcasted_iota(jnp.int32, (gc * gc, gc), 0)
    col = lax.broadcasted_iota(jnp.int32, (gc * gc, gc), 1)
    rep_c = (pair // gc == col).astype(BF16)
    rep_b = (pair % gc == col).astype(BF16)
    cb_re, cb_im = _cmul(_dot_xb(rep_c, c_re), _dot_xb(rep_c, c_im),
                         _dot_xb(rep_b, bb_re), _dot_xb(rep_b, bb_im))
    kap = _dot3(cb_re, pt_re) - _dot3(cb_im, pt_im)

    kap_pad = jnp.concatenate([kap, jnp.zeros_like(kap)], axis=1)
    srow = lax.broadcasted_iota(jnp.int32, (L, LANES), 0)
    tcol = lax.broadcasted_iota(jnp.int32, (L, LANES), 1)
    for cp in range(gc):
        for c2 in range(0, gc, 2):
            k0 = jnp.broadcast_to(kap_pad[c2 * gc + cp:c2 * gc + cp + 1, :], (L, LANES))
            k1 = jnp.broadcast_to(kap_pad[(c2 + 1) * gc + cp:(c2 + 1) * gc + cp + 1, :], (L, LANES))
            t0 = pltpu.roll(k0, 0, 1, stride=1, stride_axis=0)
            t1 = pltpu.roll(k1, L, 1, stride=1, stride_axis=0)
            blk = jnp.where(tcol < L, jnp.where(tcol >= srow, t0, 0.0),
                            jnp.where(tcol - L >= srow, t1, 0.0))
            m_sc[cp * L:(cp + 1) * L, c2 * L:(c2 + 2) * L] = blk.astype(BF16)

    s_col = lax.broadcasted_iota(jnp.int32, (L, P), 0).astype(F32)
    pw_re, pw_im = powers((L - 1.0) - s_col)
    pg_re, pg_im = powers(s_col + 1.0)
    w_re, w_im, g_re, g_im = [], [], [], []
    for c in range(gc):
        br = jnp.broadcast_to(bb_re[c:c + 1, :], (L, P))
        bi = jnp.broadcast_to(bb_im[c:c + 1, :], (L, P))
        wr, wi = _cmul(br, bi, pw_re, pw_im)
        w_re.append(wr); w_im.append(wi)
        cr = jnp.broadcast_to(c_re[c:c + 1, :], (L, P))
        ci = jnp.broadcast_to(c_im[c:c + 1, :], (L, P))
        gr, gi = _cmul(cr, ci, pg_re, pg_im)
        g_re.append(gr); g_im.append(gi)
    w_re, w_im = jnp.concatenate(w_re, axis=0), jnp.concatenate(w_im, axis=0)
    g_re, g_im = jnp.concatenate(g_re, axis=0), jnp.concatenate(g_im, axis=0)

    u = pltpu.einshape("c(js)->j(cs)", u_ref[...], s=L).astype(BF16)
    x_re = jnp.dot(u, w_re.astype(BF16), preferred_element_type=F32)
    x_im = jnp.dot(u, w_im.astype(BF16), preferred_element_type=F32)
    xs_re, xs_im = [], []
    for b in range(bsz):
        xr = x_re[b * nchunk:(b + 1) * nchunk]
        xi = x_im[b * nchunk:(b + 1) * nchunk]
        d = 1
        while d < nchunk:
            ar_, ai_ = powers(float(L * d))
            sr, si = _cmul(ar_, ai_, _shift_rows(xr, d), _shift_rows(xi, d))
            xr, xi = xr + sr, xi + si
            d *= 2
        xs_re.append(_shift_rows(xr, 1))
        xs_im.append(_shift_rows(xi, 1))
    xs_re, xs_im = jnp.concatenate(xs_re, axis=0), jnp.concatenate(xs_im, axis=0)
    y = jnp.dot(u, m_sc[...], preferred_element_type=F32)
    y = y + _dot_nt(xs_re, g_re) - _dot_nt(xs_im, g_im)
    y_ref[...] = pltpu.einshape("j(cs)->c(js)", y, s=L)


def _s5_core(ut, prm, bsz, seq):
    L, gc, P = CHUNK, S5_GC, S5_P
    n_tok = ut.shape[1]
    groups = ut.shape[0] // gc
    nchunk = seq // L
    n = L * gc
    row3 = lambda a: a.reshape(groups, 1, P).astype(F32)
    col3 = lambda a: a.reshape(groups, P, 1).astype(F32)
    dt_b = jnp.broadcast_to(prm["log_dt"][:, None], (groups, P))
    args = [ut, row3(prm["a_re"]), row3(prm["a_im"]), row3(dt_b),
            col3(prm["a_re"]), col3(prm["a_im"]), col3(dt_b),
            prm["b_re"].transpose(0, 2, 1), prm["b_im"].transpose(0, 2, 1),
            prm["c_re"], prm["c_im"]]
    g3 = lambda g: (g, 0, 0)
    specs = [pl.BlockSpec((gc, n_tok), lambda g: (g, 0))]
    specs += [pl.BlockSpec((1, 1, P), g3)] * 3 + [pl.BlockSpec((1, P, 1), g3)] * 3
    specs += [pl.BlockSpec((1, gc, P), g3)] * 4
    return pl.pallas_call(
        functools.partial(_s5_kernel, bsz=bsz, nchunk=nchunk),
        grid=(groups,),
        in_specs=specs,
        out_specs=pl.BlockSpec((gc, n_tok), lambda g: (g, 0)),
        out_shape=jax.ShapeDtypeStruct(ut.shape, F32),
        scratch_shapes=[pltpu.VMEM((n, n), BF16)],
        compiler_params=_cparams("parallel"),
        name="s5_core",
    )(*args)


def _gelu_tanh(x):
    return 0.5 * x * (1.0 + jnp.tanh(0.7978845608028654 * (x + 0.044715 * x * x * x)))


def _out_kernel(ya_ref, yb_ref, yc_ref, u_ref, x_ref, gt1_ref, sh2_ref, sc2_ref, g_ref, d_ref,
                gw_ref, gbias_ref, wo_ref, wrh_ref, wrl_ref, br_ref,
                x1_ref, h2_ref, ri_ref, rw_ref, hist_ref, *, da, db):
    yc = _gelu_tanh(yc_ref[...].T + d_ref[...] * u_ref[...].T)
    yc = yc * _sigmoid(_dot(yc, gw_ref[...]) + gbias_ref[...])
    mixed = (_dot(ya_ref[...], wo_ref[0:da, :]) + _dot(yb_ref[...], wo_ref[da:da + db, :])
             + _dot(yc, wo_ref[da + db:, :]))
    x1 = x_ref[...] + (1.0 + gt1_ref[0, 0]) * mixed
    x1_ref[...] = x1
    h2 = _rmsnorm(x1, g_ref[...]) * (1.0 + sc2_ref[0, 0]) + sh2_ref[0, 0]
    h2_ref[...] = h2

    hh, hl = _split(h2)
    logits = (jnp.dot(hh, wrh_ref[...], preferred_element_type=F32)
              + jnp.dot(hh, wrl_ref[...], preferred_element_type=F32)
              + jnp.dot(hl, wrh_ref[...], preferred_element_type=F32)) + br_ref[...]
    lane_i = lax.broadcasted_iota(jnp.int32, logits.shape, 1)
    lane = lane_i.astype(F32)
    big = float(LANES)
    neg = -jnp.inf
    is_g = lane_i < N_GROUPS
    lg = jnp.where(is_g, logits, neg)
    gmax = jnp.max(lg, axis=1, keepdims=True)
    gi = jnp.min(jnp.where(is_g & (lg == gmax), lane, big), axis=1, keepdims=True)
    gp = 1.0 / jnp.sum(jnp.where(is_g, jnp.exp(lg - gmax), 0.0), axis=1, keepdims=True)
    e_lane = lane_i - N_GROUPS
    grp_of_lane = lax.shift_right_arithmetic(e_lane, 3).astype(F32)
    in_grp = (e_lane >= 0) & (e_lane < N_EXPERTS) & (grp_of_lane == gi)
    l1 = jnp.where(in_grp, logits, neg)
    m1 = jnp.max(l1, axis=1, keepdims=True)
    i1 = jnp.min(jnp.where(in_grp & (l1 == m1), lane, big), axis=1, keepdims=True)
    rest = in_grp & (lane != i1)
    l2 = jnp.where(rest, logits, neg)
    m2 = jnp.max(l2, axis=1, keepdims=True)
    i2 = jnp.min(jnp.where(rest & (l2 == m2), lane, big), axis=1, keepdims=True)
    e2 = jnp.exp(m2 - m1)
    w1 = gp / (1.0 + e2)
    w2 = gp * e2 / (1.0 + e2)
    hot1 = (lane == i1).astype(F32)
    hot2 = (lane == i2).astype(F32)
    both = (hot1 + hot2).astype(BF16)
    tm = logits.shape[0]
    rr = lax.broadcasted_iota(jnp.int32, (tm, tm), 0)
    cc = lax.broadcasted_iota(jnp.int32, (tm, tm), 1)
    before = jnp.dot((cc < rr).astype(BF16), both, preferred_element_type=F32)
    rank1 = jnp.sum(before * hot1, axis=1, keepdims=True)
    rank2 = jnp.sum(before * hot2, axis=1, keepdims=True)
    hist_ref[0] = jnp.sum(hot1 + hot2, axis=0, keepdims=True)
    ids = jnp.where(lane_i == 0, i1, jnp.where(lane_i == 1, i2, jnp.where(lane_i == 2, rank1, rank2)))
    ri_ref[...] = ids.astype(jnp.int32)
    rw_ref[...] = jnp.where(lane_i == 0, w1, jnp.where(lane_i == 1, w2, 0.0))


def _out_proj(ya, yb, yc, u, x2d, mod_l, g_ffn, s5_d, glu_w, glu_b, w_out, w_rg, b_rg, w_re, b_re, seq):
    n_tok, d = x2d.shape
    da, db, dc = ya.shape[1], yb.shape[1], yc.shape[0]
    tm = _row_tile(seq, 512)
    per_b = seq // tm
    wr = jnp.zeros((d, LANES), F32).at[:, :N_GROUPS].set(w_rg).at[:, N_GROUPS:N_GROUPS + N_EXPERTS].set(w_re)
    wr_hi = wr.astype(BF16)
    wr_lo = (wr - wr_hi.astype(F32)).astype(BF16)
    br = jnp.zeros((1, LANES), F32).at[0, :N_GROUPS].set(b_rg).at[0, N_GROUPS:N_GROUPS + N_EXPERTS].set(b_re)
    row = lambda i: (i, 0)
    const = lambda i: (0, 0)
    modspec = lambda j: pl.BlockSpec((1, 1, 1, d), lambda i: (i // per_b, j, 0, 0))
    return pl.pallas_call(
        functools.partial(_out_kernel, da=da, db=db),
        grid=(n_tok // tm,),
        in_specs=[pl.BlockSpec((tm, da), row), pl.BlockSpec((tm, db), row),
                  pl.BlockSpec((dc, tm), lambda i: (0, i)), pl.BlockSpec((dc, tm), lambda i: (0, i)),
                  pl.BlockSpec((tm, d), row),
                  modspec(2), modspec(3), modspec(4),
                  pl.BlockSpec((1, d), const), pl.BlockSpec((1, dc), const),
                  pl.BlockSpec((dc, dc), const), pl.BlockSpec((1, dc), const),
                  pl.BlockSpec((d, d), const), pl.BlockSpec((d, LANES), const),
                  pl.BlockSpec((d, LANES), const), pl.BlockSpec((1, LANES), const)],
        out_specs=[pl.BlockSpec((tm, d), row), pl.BlockSpec((tm, d), row),
                   pl.BlockSpec((tm, LANES), row), pl.BlockSpec((tm, LANES), row),
                   pl.BlockSpec((1, 1, LANES), lambda i: (i, 0, 0))],
        out_shape=[jax.ShapeDtypeStruct((n_tok, d), F32), jax.ShapeDtypeStruct((n_tok, d), F32),
                   jax.ShapeDtypeStruct((n_tok, LANES), jnp.int32),
                   jax.ShapeDtypeStruct((n_tok, LANES), F32),
                   jax.ShapeDtypeStruct((n_tok // tm, 1, LANES), F32)],
        compiler_params=_cparams("parallel"),
        name="out_proj_router",
    )(ya, yb, yc, u, x2d, mod_l, mod_l, mod_l, g_ffn, s5_d.reshape(1, dc), glu_w.astype(BF16),
      glu_b.reshape(1, dc), w_out.astype(BF16), wr_hi, wr_lo, br)


ROW_UNROLL = 16


def _route(ri, hist, n_tok, tm):
    ntile = n_tok // tm
    h = hist.reshape(ntile, LANES)[:, N_GROUPS:N_GROUPS + N_EXPERTS].astype(jnp.int32)
    counts = jnp.sum(h, axis=0)
    pcounts = (counts + MOE_ROWS - 1) // MOE_ROWS * MOE_ROWS
    pends = jnp.cumsum(pcounts)
    base = (pends - pcounts)[None, :] + jnp.cumsum(h, axis=0) - h
    n_rows = 2 * n_tok + N_EXPERTS * MOE_ROWS
    nblk = n_rows // MOE_ROWS
    blk_start = jnp.arange(nblk, dtype=jnp.int32)[:, None] * MOE_ROWS
    blk_e = jnp.sum((pends[None, :] <= blk_start).astype(jnp.int32), axis=1)
    eid = (ri[:, 0:2] - N_GROUPS).reshape(ntile, tm, 2, 1)
    hot = eid == jnp.arange(N_EXPERTS, dtype=jnp.int32)
    dest = jnp.sum(jnp.where(hot, base[:, None, None, :], 0), axis=-1) + ri[:, 2:4].reshape(ntile, tm, 2)
    return (dest.reshape(n_tok, 2).astype(jnp.int32), blk_e.astype(jnp.int32), pends.astype(jnp.int32),
            pcounts.astype(jnp.int32), n_rows)


def _tile_rows(dest, tm):
    nblk = dest.shape[0] // tm
    return dest.reshape(nblk, tm, 2).transpose(0, 2, 1).reshape(nblk, 1, 2 * tm)


def _dispatch_kernel(pend_ref, pcnt_ref, d_ref, h_ref, xs_hbm, zbuf, hbuf, sem, zsem, *, tm, n_rows):
    i = pl.program_id(0)
    n = pl.num_programs(0)
    slot = i % 2

    @pl.when(i == 0)
    def _():
        zbuf[...] = jnp.zeros_like(zbuf)

        def zcopy(start):
            return pltpu.make_async_copy(zbuf, xs_hbm.at[pl.ds(start, MOE_ROWS)], zsem.at[0])

        first_tail = lax.shift_right_logical(pend_ref[N_EXPERTS - 1], MOE_ROWS.bit_length() - 1)
        nblk = n_rows // MOE_ROWS

        def tail_start(blk, carry):
            zcopy(pl.multiple_of(blk * MOE_ROWS, MOE_ROWS)).start()
            return carry

        def tail_wait(blk, carry):
            zcopy(0).wait()
            return carry

        for e in range(N_EXPERTS):
            @pl.when(pcnt_ref[e] > 0)
            def _():
                zcopy(pl.multiple_of(pend_ref[e] - MOE_ROWS, MOE_ROWS)).start()
        lax.fori_loop(first_tail, nblk, tail_start, 0)
        for e in range(N_EXPERTS):
            @pl.when(pcnt_ref[e] > 0)
            def _():
                zcopy(0).wait()
        lax.fori_loop(first_tail, nblk, tail_wait, 0)

    def body(g, carry):
        for u in range(ROW_UNROLL):
            r = g * ROW_UNROLL + u
            src = hbuf.at[slot, pl.ds(r, 1)]
            pltpu.make_async_copy(src, xs_hbm.at[pl.ds(d_ref[0, 0, r], 1)], sem.at[slot]).start(priority=0)
            pltpu.make_async_copy(src, xs_hbm.at[pl.ds(d_ref[0, 0, tm + r], 1)], sem.at[slot]).start(priority=1)
        return carry

    def drain(s):
        for _ in range(2):
            pltpu.make_async_copy(hbuf.at[s], xs_hbm.at[pl.ds(0, tm)], sem.at[s]).wait()

    hbuf[slot] = h_ref[...].reshape(tm, SUBLANES, LANES)
    lax.fori_loop(0, tm // ROW_UNROLL, body, 0)

    @pl.when(i > 0)
    def _():
        drain(1 - slot)

    @pl.when(i == n - 1)
    def _():
        drain(slot)


def _dispatch(h2, dest, pends, pcounts, n_rows, seq):
    n_tok, d = h2.shape
    tm = _row_tile(seq, 512)
    grid_spec = pltpu.PrefetchScalarGridSpec(
        num_scalar_prefetch=2,
        grid=(n_tok // tm,),
        in_specs=[pl.BlockSpec((1, 1, 2 * tm), lambda i, pe, pc: (i, 0, 0), memory_space=pltpu.SMEM),
                  pl.BlockSpec((tm, d), lambda i, pe, pc: (i, 0))],
        out_specs=pl.BlockSpec(memory_space=pl.ANY),
        scratch_shapes=[pltpu.VMEM((MOE_ROWS, SUBLANES, LANES), F32),
                        pltpu.VMEM((2, tm, SUBLANES, LANES), F32),
                        pltpu.SemaphoreType.DMA((2,)), pltpu.SemaphoreType.DMA((1,))],
    )
    assert d == SUBLANES * LANES
    return pl.pallas_call(
        functools.partial(_dispatch_kernel, tm=tm, n_rows=n_rows),
        grid_spec=grid_spec,
        out_shape=jax.ShapeDtypeStruct((n_rows, SUBLANES, LANES), F32),
        compiler_params=_cparams("arbitrary"),
        name="moe_dispatch",
    )(pends, pcounts, _tile_rows(dest, tm), h2)


def _moe_kernel(blk_e_ref, x_ref, w1_ref, w3_ref, w2_ref, y_ref, w1_sc, w3_sc, w2_sc):
    i = pl.program_id(0)
    e = blk_e_ref[i]
    used = e < N_EXPERTS

    @pl.when(used & ((i == 0) | (e != blk_e_ref[jnp.maximum(i - 1, 0)])))
    def _():
        w1_sc[...] = w1_ref[0, 0].astype(BF16)
        w3_sc[...] = w3_ref[0, 0].astype(BF16)
        w2_sc[...] = w2_ref[0, 0].astype(BF16)

    @pl.when(used)
    def _():
        rows = x_ref.shape[0]
        xb = x_ref[...].reshape(rows, SUBLANES * LANES).astype(BF16)
        act = (_silu(jnp.dot(xb, w1_sc[...], preferred_element_type=F32))
               * jnp.dot(xb, w3_sc[...], preferred_element_type=F32))
        y = jnp.dot(act.astype(BF16), w2_sc[...], preferred_element_type=F32)
        y_ref[...] = y.reshape(rows, SUBLANES, LANES)

    @pl.when(jnp.logical_not(used))
    def _():
        y_ref[...] = jnp.zeros_like(y_ref)


def _moe_experts(xs, blk_e, w1, w3, w2, layer):
    n_rows = xs.shape[0]
    d, d_exp = w1.shape[2], w1.shape[3]
    row_blk = pl.BlockSpec((MOE_ROWS, SUBLANES, LANES), lambda i, e: (i, 0, 0))
    wmap = lambda i, e: (layer, jnp.minimum(e[i], N_EXPERTS - 1), 0, 0)
    grid_spec = pltpu.PrefetchScalarGridSpec(
        num_scalar_prefetch=1,
        grid=(n_rows // MOE_ROWS,),
        in_specs=[row_blk,
                  pl.BlockSpec((1, 1, d, d_exp), wmap), pl.BlockSpec((1, 1, d, d_exp), wmap),
                  pl.BlockSpec((1, 1, d_exp, d), wmap)],
        out_specs=row_blk,
        scratch_shapes=[pltpu.VMEM((d, d_exp), BF16), pltpu.VMEM((d, d_exp), BF16),
                        pltpu.VMEM((d_exp, d), BF16)],
    )
    return pl.pallas_call(
        _moe_kernel,
        grid_spec=grid_spec,
        out_shape=jax.ShapeDtypeStruct((n_rows, SUBLANES, LANES), F32),
        compiler_params=_cparams("arbitrary"),
        name="moe_experts",
    )(blk_e, xs, w1, w3, w2)


def _comb_kernel(d_ref, d_next_ref, x_ref, rw_ref, gt_ref, y_hbm, g_ref, o_ref, ybuf, sem, *, tm, final):
    i = pl.program_id(0)
    n = pl.num_programs(0)
    slot = i % 2

    def start_all(idx_ref, s):
        def body(g, carry):
            for u in range(ROW_UNROLL):
                r = g * ROW_UNROLL + u
                pltpu.make_async_copy(y_hbm.at[pl.ds(idx_ref[0, 0, r], 1)], ybuf.at[s, pl.ds(r, 1)],
                                      sem.at[s]).start(priority=u % 2)
            return carry
        lax.fori_loop(0, 2 * tm // ROW_UNROLL, body, 0)

    @pl.when(i == 0)
    def _():
        start_all(d_ref, 0)

    @pl.when(i + 1 < n)
    def _():
        start_all(d_next_ref, 1 - slot)

    pltpu.make_async_copy(y_hbm.at[pl.ds(0, 2 * tm)], ybuf.at[slot], sem.at[slot]).wait()

    w = rw_ref[...]
    d = x_ref.shape[1]
    moe = (w[:, 0:1] * ybuf[slot, 0:tm].reshape(tm, d)
           + w[:, 1:2] * ybuf[slot, tm:2 * tm].reshape(tm, d))
    x2 = x_ref[...] + (1.0 + gt_ref[0, 0]) * moe
    o_ref[...] = _rmsnorm(x2, g_ref[...]) if final else x2


def _combine(x1, y_rows, dest, rw, mod_l, g_final, seq, final):
    n_tok, d = x1.shape
    tm = _row_tile(seq, 512)
    per_b = seq // tm
    nblk = n_tok // tm
    idx = _tile_rows(dest, tm)
    return pl.pallas_call(
        functools.partial(_comb_kernel, tm=tm, final=final),
        grid=(nblk,),
        in_specs=[pl.BlockSpec((1, 1, 2 * tm), lambda i: (i, 0, 0), memory_space=pltpu.SMEM),
                  pl.BlockSpec((1, 1, 2 * tm), lambda i: (jnp.minimum(i + 1, nblk - 1), 0, 0),
                               memory_space=pltpu.SMEM),
                  pl.BlockSpec((tm, d), lambda i: (i, 0)),
                  pl.BlockSpec((tm, LANES), lambda i: (i, 0)),
                  pl.BlockSpec((1, 1, 1, d), lambda i: (i // per_b, 5, 0, 0)),
                  pl.BlockSpec(memory_space=pl.ANY),
                  pl.BlockSpec((1, d), lambda i: (0, 0))],
        out_specs=pl.BlockSpec((tm, d), lambda i: (i, 0)),
        out_shape=jax.ShapeDtypeStruct((n_tok, d), F32),
        scratch_shapes=[pltpu.VMEM((2, 2 * tm, SUBLANES, LANES), F32), pltpu.SemaphoreType.DMA((2,))],
        compiler_params=_cparams("arbitrary"),
        name="moe_combine",
    )(idx, idx, x1, rw, mod_l, y_rows, g_final)


def kernel(x, c, ada_w, ada_b, norm_mix, norm_ffn, norm_final, w_in, w_out, mlstm_conv_w, mlstm_conv_b, mlstm_w_q, mlstm_w_k, mlstm_b_i, mlstm_b_f, mlstm_norm_w, rwkv_mu, rwkv_w0, rwkv_w_up, rwkv_a0, rwkv_a_up, rwkv_g_up, rwkv_k_k, rwkv_k_a, rwkv_r_k, rwkv_ln_w, rwkv_ln_b, rwkv_v0, rwkv_v_dn, rwkv_v_up, s5_a_re, s5_a_im, s5_log_dt, s5_b_re, s5_b_im, s5_c_re, s5_c_im, s5_d, s5_glu_w, s5_glu_b, moe_w_rg, moe_b_rg, moe_w_re, moe_b_re, moe_w1, moe_w3, moe_w2):
    bsz, seq, d = x.shape
    depth = ada_w.shape[0]
    n_tok = bsz * seq
    heads_a = mlstm_w_q.shape[1]
    da = heads_a * HEAD_DIM
    db = rwkv_w0.shape[1]
    dc = s5_d.shape[1]
    rw_cols = rwkv_mu.shape[1]
    assert seq % CHUNK == 0 and w_in.shape[2] == 3 * da + 2 * heads_a + rw_cols + dc
    assert 2 * HEAD_DIM == LANES and CHUNK == HEAD_DIM and EXPERTS_PER_GROUP == 8
    widths = (3 * da, 2 * LANES, rw_cols, dc)

    mod = _modulation(c, ada_w, ada_b).reshape(depth, bsz, 6, 1, d)
    xc = x.reshape(n_tok, d)
    v_first = None
    for l in range(depth):
        mod_l = mod[l]
        qkvo, gates, pcols, u = _in_proj(xc, mod_l, norm_mix[l].reshape(1, d), w_in, l, widths, da,
                                         heads_a, seq)

        ya = _mlstm(qkvo, gates, mlstm_conv_w[l], mlstm_conv_b[l], mlstm_w_q[l], mlstm_w_k[l],
                    mlstm_b_i[l], mlstm_b_f[l], mlstm_norm_w[l], bsz, seq)
        rprm = dict(mu=rwkv_mu[l], w0=rwkv_w0[l], w_up=rwkv_w_up[l], a0=rwkv_a0[l], a_up=rwkv_a_up[l],
                    g_up=rwkv_g_up[l], k_k=rwkv_k_k[l], k_a=rwkv_k_a[l], r_k=rwkv_r_k[l],
                    ln_w=rwkv_ln_w[l], ln_b=rwkv_ln_b[l])
        if l > 0:
            rprm.update(v0=rwkv_v0[l - 1], v_dn=rwkv_v_dn[l - 1], v_up=rwkv_v_up[l - 1])
        yb, v_first = _rwkv(pcols, v_first if l > 0 else None, rprm, bsz, seq)
        sprm = dict(a_re=s5_a_re[l], a_im=s5_a_im[l], log_dt=s5_log_dt[l], b_re=s5_b_re[l],
                    b_im=s5_b_im[l], c_re=s5_c_re[l], c_im=s5_c_im[l])
        yc = _s5_core(u, sprm, bsz, seq)

        x1, h2, ri, rw, hist = _out_proj(ya, yb, yc, u, xc, mod_l, norm_ffn[l].reshape(1, d), s5_d[l],
                                         s5_glu_w[l], s5_glu_b[l], w_out[l], moe_w_rg[l], moe_b_rg[l],
                                         moe_w_re[l], moe_b_re[l], seq)
        dest, blk_e, pends, pcounts, n_rows = _route(ri, hist, n_tok, n_tok // hist.shape[0])
        y_rows = _moe_experts(_dispatch(h2, dest, pends, pcounts, n_rows, seq), blk_e, moe_w1, moe_w3,
                              moe_w2, l)
        xc = _combine(x1, y_rows, dest, rw, mod_l, norm_final.reshape(1, d), seq, final=(l == depth - 1))
    return xc.reshape(bsz, seq, d).astype(x.dtype)
```

```python
import functools

import jax
import jax.numpy as jnp
from jax import lax
from jax.experimental import pallas as pl
from jax.experimental.pallas import tpu as pltpu

F32 = jnp.float32
BF16 = jnp.bfloat16

HEAD_DIM = 64
CHUNK = 64
CONV_K = 4
S5_GC = 16
S5_P = 64
N_GROUPS = 4
EXPERTS_PER_GROUP = 8
N_EXPERTS = N_GROUPS * EXPERTS_PER_GROUP
NORM_EPS = 1e-6
HEAD_NORM_EPS = 1e-5
RWKV_GN_EPS = 64e-5
L2_EPS = 1e-12
LANES = 128
SUBLANES = 8
MOE_ROWS = 512
VMEM_LIMIT = 56 * 1024 * 1024


def _cparams(*sem):
    return pltpu.CompilerParams(dimension_semantics=sem, vmem_limit_bytes=VMEM_LIMIT)


def _row_tile(n, want):
    t = min(n, want)
    assert n % t == 0
    return t


def _dot(a, b):
    return jnp.dot(a.astype(BF16), b.astype(BF16), preferred_element_type=F32)


def _dot_nt(a, b):
    return lax.dot_general(a.astype(BF16), b.astype(BF16), (((1,), (1,)), ((), ())),
                           preferred_element_type=F32)


def _dot_tn(a, b):
    return lax.dot_general(a.astype(BF16), b.astype(BF16), (((0,), (0,)), ((), ())),
                           preferred_element_type=F32)


def _split(a):
    hi = a.astype(BF16)
    lo = (a - hi.astype(F32)).astype(BF16)
    return hi, lo


def _dot_xa(a, b_exact):
    hi, lo = _split(a)
    return (jnp.dot(hi, b_exact, preferred_element_type=F32)
            + jnp.dot(lo, b_exact, preferred_element_type=F32))


def _dot_xb(a_exact, b):
    hi, lo = _split(b)
    return (jnp.dot(a_exact, hi, preferred_element_type=F32)
            + jnp.dot(a_exact, lo, preferred_element_type=F32))


def _dot_x3(a, b_exact):
    hi = a.astype(BF16)
    r1 = a - hi.astype(F32)
    mid = r1.astype(BF16)
    lo = (r1 - mid.astype(F32)).astype(BF16)
    return (jnp.dot(hi, b_exact, preferred_element_type=F32)
            + jnp.dot(mid, b_exact, preferred_element_type=F32)
            + jnp.dot(lo, b_exact, preferred_element_type=F32))


def _dot3(a, b):
    ah, al = _split(a)
    bh, bl = _split(b)
    return (jnp.dot(ah, bh, preferred_element_type=F32)
            + jnp.dot(ah, bl, preferred_element_type=F32)
            + jnp.dot(al, bh, preferred_element_type=F32))


def _sigmoid(x):
    return 1.0 / (1.0 + jnp.exp(-x))


def _silu(x):
    return x * _sigmoid(x)


def _log_sigmoid(x):
    return jnp.minimum(x, 0.0) - jnp.log1p(jnp.exp(-jnp.abs(x)))


def _rmsnorm(x, g):
    ms = jnp.mean(x * x, axis=-1, keepdims=True)
    return x * lax.rsqrt(ms + NORM_EPS) * g


def _tri_incl(n):
    r = lax.broadcasted_iota(jnp.int32, (n, n), 0)
    c = lax.broadcasted_iota(jnp.int32, (n, n), 1)
    return (c <= r).astype(BF16)


def _head_ones(width):
    r = lax.broadcasted_iota(jnp.int32, (width, width), 0) // HEAD_DIM
    c = lax.broadcasted_iota(jnp.int32, (width, width), 1) // HEAD_DIM
    return (r == c).astype(BF16)


def _mod_kernel(c_ref, w_ref, b_ref, o_ref):
    o_ref[0] = _dot(_silu(c_ref[...]), w_ref[0]) + b_ref[0]


def _modulation(c, ada_w, ada_b):
    depth, d, d6 = ada_w.shape
    bsz = c.shape[0]
    tn = _row_tile(d6, 1024)
    return pl.pallas_call(
        _mod_kernel,
        grid=(depth, d6 // tn),
        in_specs=[pl.BlockSpec((bsz, d), lambda l, j: (0, 0)),
                  pl.BlockSpec((1, d, tn), lambda l, j: (l, 0, j)),
                  pl.BlockSpec((1, 1, tn), lambda l, j: (l, 0, j))],
        out_specs=pl.BlockSpec((1, bsz, tn), lambda l, j: (l, 0, j)),
        out_shape=jax.ShapeDtypeStruct((depth, bsz, d6), F32),
        compiler_params=_cparams("parallel", "parallel"),
        name="adaln_mod",
    )(c, ada_w, ada_b.reshape(depth, 1, d6))


def _in_kernel(x_ref, sh_ref, sc_ref, g_ref, w_ref, *refs, widths, da, heads):
    out_refs, w_sc = refs[:-1], refs[-1]
    d = w_ref.shape[1]
    rw_cols, dc = widths[2], widths[3]

    @pl.when(pl.program_id(0) == 0)
    def _():
        rb = 256
        g0 = 3 * da
        r_src = g0 + 2 * heads
        u_src = r_src + rw_cols
        r_win = -(-(2 * heads + rw_cols) // LANES) * LANES
        u_al = u_src // LANES * LANES
        lane = lax.broadcasted_iota(jnp.int32, (rb, LANES), 1)
        for r0 in range(0, d, rb):
            rs = slice(r0, r0 + rb)
            w_sc[rs, 0:g0] = w_ref[0, rs, 0:g0].astype(BF16)
            gblk = w_ref[0, rs, g0:g0 + LANES]
            w_sc[rs, g0:g0 + LANES] = jnp.where(lane < heads, gblk, 0.0).astype(BF16)
            w_sc[rs, g0 + LANES:g0 + 2 * LANES] = jnp.where(
                lane < heads, pltpu.roll(gblk, LANES - heads, 1), 0.0).astype(BF16)
            win = w_ref[0, rs, g0:g0 + r_win]
            w_sc[rs, g0 + 2 * LANES:g0 + 2 * LANES + rw_cols] = win[:, 2 * heads:2 * heads + rw_cols].astype(BF16)
            win2 = w_ref[0, rs, u_al:u_src + dc]
            w_sc[rs, g0 + 2 * LANES + rw_cols:] = win2[:, u_src - u_al:u_src - u_al + dc].astype(BF16)

    h = _rmsnorm(x_ref[...], g_ref[...]) * (1.0 + sc_ref[0, 0]) + sh_ref[0, 0]
    hb = h.astype(BF16)
    off = 0
    for k, (o_ref, wd) in enumerate(zip(out_refs, widths)):
        res = jnp.dot(hb, w_sc[:, off:off + wd], preferred_element_type=F32)
        o_ref[...] = res.T if k == len(widths) - 1 else res.astype(o_ref.dtype)
        off += wd


def _in_proj(x2d, mod_l, g, w_in, layer, widths, da, heads, seq):
    n_tok, d = x2d.shape
    cols = w_in.shape[2]
    tm = _row_tile(seq, 1024)
    per_b = seq // tm
    return pl.pallas_call(
        functools.partial(_in_kernel, widths=widths, da=da, heads=heads),
        grid=(n_tok // tm,),
        in_specs=[pl.BlockSpec((tm, d), lambda i: (i, 0)),
                  pl.BlockSpec((1, 1, 1, d), lambda i: (i // per_b, 0, 0, 0)),
                  pl.BlockSpec((1, 1, 1, d), lambda i: (i // per_b, 1, 0, 0)),
                  pl.BlockSpec((1, d), lambda i: (0, 0)),
                  pl.BlockSpec((1, d, cols), lambda i: (layer, 0, 0), pipeline_mode=pl.Buffered(1))],
        out_specs=[pl.BlockSpec((tm, wd), lambda i: (i, 0)) for wd in widths[:-1]]
        + [pl.BlockSpec((widths[-1], tm), lambda i: (0, i))],
        out_shape=[jax.ShapeDtypeStruct((n_tok, wd), dt) for wd, dt in zip(widths[:-1], (BF16, F32, BF16))]
        + [jax.ShapeDtypeStruct((widths[-1], n_tok), F32)],
        scratch_shapes=[pltpu.VMEM((d, sum(widths)), BF16)],
        compiler_params=_cparams("arbitrary"),
        name="in_proj",
    )(x2d, mod_l, mod_l, g, w_in)


def _mlstm_kernel(qkvo_ref, gate_ref, cw_ref, cb_ref, wq_ref, wk_ref, gb_ref, nw_ref, out_ref,
                  xf_sc, q_sc, k_sc, gi_sc, gf_sc, cn_sc, m_sc, *, nb, tb, heads):
    dh, L = HEAD_DIM, CHUNK
    da = heads * dh
    i = pl.program_id(0)

    @pl.when(i == 0)
    def _():
        xf_sc[:, 0:SUBLANES, :] = jnp.zeros((nb, SUBLANES, da), F32)
        cn_sc[...] = jnp.zeros_like(cn_sc)
        m_sc[...] = jnp.zeros_like(m_sc)

    for b in range(nb):
        xqk = qkvo_ref[b, :, 0:da].astype(F32)
        xf_sc[b, pl.ds(SUBLANES, tb), :] = xqk
        acc = xqk * cw_ref[CONV_K - 1:CONV_K, :] + cb_ref[...]
        for j in range(1, CONV_K):
            acc = acc + xf_sc[b, pl.ds(SUBLANES - j, tb), :] * cw_ref[CONV_K - 1 - j:CONV_K - j, :]
        xf_sc[b, 0:SUBLANES, :] = xf_sc[b, pl.ds(tb, SUBLANES), :]
        cx = _silu(acc).astype(BF16)
        q_sc[b] = jnp.dot(cx, wq_ref[...], preferred_element_type=F32)
        k_sc[b] = jnp.dot(cx, wk_ref[...], preferred_element_type=F32) * (dh ** -0.5)

        g = gate_ref[b] + gb_ref[...]
        gi_sc[b] = g[:, :LANES]
        gf_sc[b] = _log_sigmoid(g[:, LANES:])

    tri = _tri_incl(L)
    rr = lax.broadcasted_iota(jnp.int32, (L, L), 0)
    cc = lax.broadcasted_iota(jnp.int32, (L, L), 1)
    causal = cc <= rr
    row_l = lax.broadcasted_iota(jnp.int32, (L, LANES), 0)
    sel = (lax.broadcasted_iota(jnp.int32, (LANES, heads * LANES), 0)
           == lax.broadcasted_iota(jnp.int32, (LANES, heads * LANES), 1) // LANES).astype(BF16)
    ones_v = jnp.ones((L, dh), F32)
    mean_m = jnp.full((dh, dh), 1.0 / dh, BF16)

    units = [(b, h) for b in range(nb) for h in range(heads)]
    ur = range(len(units))

    def hcol(h, base=0):
        return slice(base + h * dh, base + (h + 1) * dh)

    def chunk(c, carry):
        rows = pl.ds(pl.multiple_of(c * L, L), L)
        ex, g_t = [], []
        for b in range(nb):
            bc = _dot_xb(tri, gf_sc[b, rows, :])
            gtot = bc[L - 1:L, :]
            g = gi_sc[b, rows, :] - bc
            gmax = jnp.max(g, axis=0, keepdims=True)
            m_prev = m_sc[b]
            m_loc = gtot + gmax
            m_new = jnp.maximum(gtot + m_prev, m_loc)
            a_old = jnp.exp(gtot + m_prev - m_new)
            a_loc = jnp.exp(m_loc - m_new)
            m_sc[b] = m_new
            pm = g
            d = 1
            while d < L:
                pm = jnp.maximum(pm, jnp.where(row_l >= d, pltpu.roll(pm, d, 0), -jnp.inf))
                d *= 2
            mm = jnp.maximum(m_prev, pm)
            stack = jnp.concatenate([jnp.exp(g - gmax), mm, jnp.exp(m_prev - mm), jnp.exp(-(bc + mm)),
                                     a_old, a_loc, jnp.zeros((SUBLANES - 2, LANES), F32)], axis=0)
            ex.append(_dot_x3(stack, sel))
            g_t.append(g.T)

        def part(k, b, h, width=dh):
            return ex[b][k * L:(k + 1) * L, h * LANES:h * LANES + width]

        qc = [q_sc[b, rows, hcol(h)] for b, h in units]
        kc = [k_sc[b, rows, hcol(h)] for b, h in units]
        vo = [jnp.concatenate([qkvo_ref[b, rows, hcol(h, da)].astype(F32), ones_v], axis=1) for b, h in units]
        oc = [qkvo_ref[b, rows, hcol(h, 2 * da)].astype(F32) for b, h in units]
        cn_prev = [cn_sc[b, h] for b, h in units]
        s_raw = [_dot_nt(qc[u], kc[u]) for u in ur]
        q_cn = [_dot(qc[u], cn_prev[u]) for u in ur]
        cn_loc = [_dot_tn(kc[u] * part(0, b, h), vo[u]) for u, (b, h) in enumerate(units)]
        s_qk = [s_raw[u] * jnp.where(causal, jnp.exp(g_t[b][h:h + 1, :] - part(1, b, h)), 0.0)
                for u, (b, h) in enumerate(units)]
        s_vn = [_dot(s_qk[u], vo[u]) for u in ur]
        nd = [part(2, b, h, 2 * dh) * q_cn[u] + s_vn[u] for u, (b, h) in enumerate(units)]
        hh = [nd[u][:, :dh] / jnp.maximum(jnp.abs(nd[u][:, dh:]), part(3, b, h))
              for u, (b, h) in enumerate(units)]
        mu = [_dot_xa(hh[u], mean_m) for u in ur]
        dlt = [hh[u] - mu[u] for u in ur]
        var = [_dot_xa(dlt[u] * dlt[u], mean_m) for u in ur]
        outs = [dlt[u] * lax.rsqrt(var[u] + HEAD_NORM_EPS) * _sigmoid(oc[u]) for u in ur]
        for b in range(nb):
            mine = [u for u in ur if units[u][0] == b]
            out_ref[b, rows, :] = jnp.concatenate([outs[u] for u in mine], axis=1) * nw_ref[...]
        for u, (b, h) in enumerate(units):
            cn_sc[b, h] = (ex[b][4 * L:4 * L + 1, h * LANES:(h + 1) * LANES] * cn_prev[u]
                           + ex[b][4 * L + 1:4 * L + 2, h * LANES:(h + 1) * LANES] * cn_loc[u])
        return carry

    lax.fori_loop(0, tb // L, chunk, 0)


def _block_diag(w):
    heads, dh, _ = w.shape
    eye = jnp.eye(heads, dtype=w.dtype)
    return (eye[:, None, :, None] * w[:, :, None, :]).reshape(heads * dh, heads * dh)


def _mlstm(qkvo, gates, conv_w, conv_b, w_q, w_k, b_i, b_f, norm_w, bsz, seq):
    heads = w_q.shape[0]
    da = heads * HEAD_DIM
    tb = _row_tile(seq, 256)
    nblk = seq // tb
    gbias = jnp.zeros((1, 2 * LANES), F32).at[0, :heads].set(b_i).at[0, LANES:LANES + heads].set(b_f)
    kern = functools.partial(_mlstm_kernel, nb=bsz, tb=tb, heads=heads)
    row = lambda i: (0, i, 0)
    const = lambda i: (0, 0)
    out = pl.pallas_call(
        kern,
        grid=(nblk,),
        in_specs=[pl.BlockSpec((bsz, tb, 3 * da), row),
                  pl.BlockSpec((bsz, tb, 2 * LANES), row),
                  pl.BlockSpec((CONV_K, da), const),
                  pl.BlockSpec((1, da), const),
                  pl.BlockSpec((da, da), const),
                  pl.BlockSpec((da, da), const),
                  pl.BlockSpec((1, 2 * LANES), const),
                  pl.BlockSpec((1, da), const)],
        out_specs=pl.BlockSpec((bsz, tb, da), row),
        out_shape=jax.ShapeDtypeStruct((bsz, seq, da), F32),
        scratch_shapes=[pltpu.VMEM((bsz, tb + SUBLANES, da), F32),
                        pltpu.VMEM((bsz, tb, da), F32),
                        pltpu.VMEM((bsz, tb, da), F32),
                        pltpu.VMEM((bsz, tb, LANES), F32),
                        pltpu.VMEM((bsz, tb, LANES), F32),
                        pltpu.VMEM((bsz, heads, HEAD_DIM, 2 * HEAD_DIM), F32),
                        pltpu.VMEM((bsz, 1, LANES), F32)],
        compiler_params=_cparams("arbitrary"),
        name="mlstm",
    )(qkvo.reshape(bsz, seq, 3 * da), gates.reshape(bsz, seq, 2 * LANES), conv_w, conv_b.reshape(1, da),
      _block_diag(w_q).astype(BF16), _block_diag(w_k).astype(BF16), gbias, norm_w.reshape(1, da))
    return out.reshape(bsz * seq, da)


def _rwkv_kernel(*refs, nb, tb, heads, lw_dim, la_dim, lg_dim, has_vres):
    dh, L = HEAD_DIM, CHUNK
    db = heads * dh
    it = iter(refs)
    p_ref = next(it)
    vf_ref = next(it) if has_vres else None
    (mu_ref, w0_ref, wup_ref, a0_ref, aup_ref, gup_ref, kk_ref, ka_ref, rk_ref,
     lnw_ref, lnb_ref) = (next(it) for _ in range(11))
    if has_vres:
        v0_ref, vdn_ref, vup_ref = (next(it) for _ in range(3))
    y_ref = next(it)
    vout_ref = None if has_vres else next(it)
    (xf_sc, r_sc, k_sc, v_sc, a_sc, b_sc, lw_sc, y_sc, gate_sc, rp_sc, st_sc, q_sc, z_sc,
     gl_sc) = (next(it) for _ in range(14))

    i = pl.program_id(0)

    @pl.when(i == 0)
    def _():
        xf_sc[:, 0:SUBLANES, :] = jnp.zeros((nb, SUBLANES, xf_sc.shape[2]), F32)
        st_sc[...] = jnp.zeros_like(st_sc)

    ones_h = _head_ones(db)
    for b in range(nb):
        p = p_ref[b].astype(F32)
        xf_sc[b, pl.ds(SUBLANES, tb), :] = p
        prev = xf_sc[b, pl.ds(SUBLANES - 1, tb), :]
        xf_sc[b, 0:SUBLANES, :] = xf_sc[b, pl.ds(tb, SUBLANES), :]
        p = p + mu_ref[...] * (prev - p)

        o = 0
        r = p[:, o:o + db]; o += db
        k = p[:, o:o + db]; o += db
        v = p[:, o:o + db]; o += db
        wd = p[:, o:o + lw_dim]; o += lw_dim
        ad = p[:, o:o + la_dim]; o += la_dim
        gd = p[:, o:o + lg_dim]

        wlog = _log_sigmoid(w0_ref[...] + _dot(jnp.tanh(wd), wup_ref[...])) - 0.5
        lw_sc[b] = -jnp.exp(wlog)
        a = _sigmoid(a0_ref[...] + _dot(ad, aup_ref[...]))
        gate_sc[b] = _dot(_sigmoid(gd), gup_ref[...])
        if has_vres:
            v = v + (vf_ref[b] - v) * _sigmoid(v0_ref[...] + _dot(_dot(v, vdn_ref[...]), vup_ref[...]))
        else:
            vout_ref[b] = v
        kk = k * kk_ref[...]
        kk = kk / jnp.maximum(jnp.sqrt(_dot_xa(kk * kk, ones_h)), L2_EPS)
        r_sc[b] = r
        k_sc[b] = k * (1.0 + (a - 1.0) * ka_ref[...])
        v_sc[b] = v
        a_sc[b] = -kk
        b_sc[b] = kk * a

    tri = _tri_incl(L)
    rr = lax.broadcasted_iota(jnp.int32, (L, L), 0)
    cc = lax.broadcasted_iota(jnp.int32, (L, L), 1)
    strict = cc < rr
    incl = cc <= rr

    units = [(b, h) for b in range(nb) for h in range(heads)]
    ur = range(len(units))

    def hcol(h):
        return slice(h * dh, (h + 1) * dh)

    def chunk(c, carry):
        rows = pl.ds(pl.multiple_of(c * L, L), L)
        at, rt, bt, kt, bg, kg, vch = ([] for _ in range(7))
        for b in range(nb):
            lwc = lw_sc[b, rows, :]
            cum = _dot_xb(tri, lwc)
            cum_l = cum[L - 1:L, :]
            e_inv = jnp.exp(-cum)
            e_end = jnp.exp(cum_l - cum)
            bv = b_sc[b, rows, :]
            kv = k_sc[b, rows, :]
            at.append(a_sc[b, rows, :] * jnp.exp(cum - lwc))
            rt.append(r_sc[b, rows, :] * jnp.exp(cum))
            bt.append(bv * e_inv)
            kt.append(kv * e_inv)
            bg.append(bv * e_end)
            kg.append(kv * e_end)
            vch.append(v_sc[b, rows, :])
            gl_sc[c, b] = jnp.exp(cum_l)
        vh = [vch[b][:, hcol(h)] for b, h in units]
        g4 = [_dot_nt(jnp.concatenate([at[b][:, hcol(h)], rt[b][:, hcol(h)]], axis=0),
                      jnp.concatenate([bt[b][:, hcol(h)], kt[b][:, hcol(h)]], axis=0))
              for b, h in units]
        pw = [jnp.where(strict, g4[u][:L, :L], 0.0) for u in ur]
        n_ak = [jnp.where(strict, g4[u][:L, L:], 0.0) for u in ur]
        m_rb = [jnp.where(incl, g4[u][L:, :L], 0.0) for u in ur]
        m_rk = [jnp.where(incl, g4[u][L:, L:], 0.0) for u in ur]
        nv = [_dot(jnp.concatenate([n_ak[u], m_rk[u]], axis=0), vh[u]) for u in ur]
        x = [jnp.concatenate([at[b][:, hcol(h)], nv[u][:L]], axis=1)
             for u, (b, h) in enumerate(units)]
        for step in range(6):
            if step < 5:
                px = [_dot(pw[u], jnp.concatenate([x[u], pw[u]], axis=1)) for u in ur]
                x = [x[u] + px[u][:, :2 * dh] for u in ur]
                pw = [px[u][:, 2 * dh:] for u in ur]
            else:
                x = [x[u] + _dot(pw[u], x[u]) for u in ur]
        ry = [jnp.concatenate([rt[b][:, hcol(h)], nv[u][L:]], axis=1) + _dot(m_rb[u], x[u])
              for u, (b, h) in enumerate(units)]
        qz = [_dot_tn(x[u], bg[b][:, hcol(h)]) for u, (b, h) in enumerate(units)]
        z2 = [_dot_tn(vh[u], kg[b][:, hcol(h)]) for u, (b, h) in enumerate(units)]
        for b in range(nb):
            mine = [u for u in ur if units[u][0] == b]
            rp_sc[b, rows, :] = jnp.concatenate([ry[u][:, :dh] for u in mine], axis=1)
            y_sc[b, rows, :] = jnp.concatenate([ry[u][:, dh:] for u in mine], axis=1)
        for u, (b, h) in enumerate(units):
            q_sc[c, b, h] = qz[u][:dh]
            z_sc[c, b, h] = qz[u][dh:] + z2[u]
        return carry

    lax.fori_loop(0, tb // L, chunk, 0)

    def carry_state(c, carry):
        rows = pl.ds(pl.multiple_of(c * L, L), L)
        st = [st_sc[b, h] for b, h in units]
        ys = [_dot_nt(rp_sc[b, rows, hcol(h)], st[u]) for u, (b, h) in enumerate(units)]
        sq = [_dot_xa(st[u], q_sc[c, b, h].astype(BF16)) for u, (b, h) in enumerate(units)]
        for b in range(nb):
            mine = [u for u in ur if units[u][0] == b]
            y_sc[b, rows, :] = y_sc[b, rows, :] + jnp.concatenate([ys[u] for u in mine], axis=1)
        for u, (b, h) in enumerate(units):
            st_sc[b, h] = st[u] * gl_sc[c, b][:, hcol(h)] + sq[u] + z_sc[c, b, h]
        return carry

    lax.fori_loop(0, tb // L, carry_state, 0)

    for b in range(nb):
        y = y_sc[b]
        mean = _dot_xa(y, ones_h) * (1.0 / dh)
        dlt = y - mean
        var = _dot_xa(dlt * dlt, ones_h) * (1.0 / dh)
        yn = dlt * lax.rsqrt(var + RWKV_GN_EPS) * lnw_ref[...] + lnb_ref[...]
        bonus = _dot_xa(r_sc[b] * k_sc[b] * rk_ref[...], ones_h) * v_sc[b]
        y_ref[b] = (yn + bonus) * gate_sc[b]


def _rwkv(pcols, v_first, prm, bsz, seq):
    db = prm["w0"].shape[0]
    heads = db // HEAD_DIM
    cols = pcols.shape[1]
    lw_dim, la_dim, lg_dim = prm["w_up"].shape[0], prm["a_up"].shape[0], prm["g_up"].shape[0]
    has_vres = v_first is not None
    tb = _row_tile(seq, 256)
    nblk = seq // tb
    row = lambda i: (0, i, 0)
    const = lambda i: (0, 0)
    vec = lambda a: a.reshape(1, -1).astype(F32)

    args = [pcols.reshape(bsz, seq, cols)]
    specs = [pl.BlockSpec((bsz, tb, cols), row)]
    if has_vres:
        args.append(v_first.reshape(bsz, seq, db))
        specs.append(pl.BlockSpec((bsz, tb, db), row))
    small = [vec(prm["mu"]), vec(prm["w0"]), prm["w_up"].astype(BF16), vec(prm["a0"]),
             prm["a_up"].astype(BF16), prm["g_up"].astype(BF16), vec(prm["k_k"]), vec(prm["k_a"]),
             vec(prm["r_k"]), vec(prm["ln_w"]), vec(prm["ln_b"])]
    if has_vres:
        lv = prm["v_dn"].shape[1]
        lvp = -(-lv // LANES) * LANES
        v_dn = jnp.zeros((db, lvp), F32).at[:, :lv].set(prm["v_dn"]).astype(BF16)
        v_up = jnp.zeros((lvp, db), F32).at[:lv, :].set(prm["v_up"]).astype(BF16)
        small += [vec(prm["v0"]), v_dn, v_up]
    args += small
    specs += [pl.BlockSpec(a.shape, const) for a in small]

    out_shape = [jax.ShapeDtypeStruct((bsz, seq, db), F32)]
    out_specs = [pl.BlockSpec((bsz, tb, db), row)]
    if not has_vres:
        out_shape.append(jax.ShapeDtypeStruct((bsz, seq, db), F32))
        out_specs.append(pl.BlockSpec((bsz, tb, db), row))

    kern = functools.partial(_rwkv_kernel, nb=bsz, tb=tb, heads=heads, lw_dim=lw_dim, la_dim=la_dim,
                             lg_dim=lg_dim, has_vres=has_vres)
    nch = tb // CHUNK
    res = pl.pallas_call(
        kern,
        grid=(nblk,),
        in_specs=specs,
        out_specs=out_specs,
        out_shape=out_shape,
        scratch_shapes=[pltpu.VMEM((bsz, tb + SUBLANES, cols), F32)]
        + [pltpu.VMEM((bsz, tb, db), F32) for _ in range(9)]
        + [pltpu.VMEM((bsz, heads, HEAD_DIM, HEAD_DIM), F32),
           pltpu.VMEM((nch, bsz, heads, HEAD_DIM, HEAD_DIM), F32),
           pltpu.VMEM((nch, bsz, heads, HEAD_DIM, HEAD_DIM), F32),
           pltpu.VMEM((nch, bsz, 1, db), F32)],
        compiler_params=_cparams("arbitrary"),
        name="rwkv7",
    )(*args)
    res = [a.reshape(bsz * seq, db) for a in res]
    return (res[0], v_first) if has_vres else (res[0], res[1])


def _cmul(ar, ai, br, bi):
    return ar * br - ai * bi, ar * bi + ai * br


def _shift_rows(x, d):
    row = lax.broadcasted_iota(jnp.int32, x.shape, 0)
    return jnp.where(row >= d, pltpu.roll(x, d, 0), 0.0)


def _s5_kernel(u_ref, are_r, aim_r, dt_r, are_c, aim_c, dt_c, bre_ref, bim_ref, cre_ref, cim_ref,
               y_ref, m_sc, *, bsz, nchunk):
    L, gc, P = CHUNK, S5_GC, S5_P
    n = L * gc
    a_re, a_im, dt = are_r[0], aim_r[0], jnp.exp(dt_r[0])
    mag, ang = jnp.exp(a_re * dt), a_im * dt
    ab_re, ab_im = mag * jnp.cos(ang), mag * jnp.sin(ang)
    inv = 1.0 / (a_re * a_re + a_im * a_im)
    co_re = ((ab_re - 1.0) * a_re + ab_im * a_im) * inv
    co_im = (ab_im * a_re - (ab_re - 1.0) * a_im) * inv
    bb_re, bb_im = _cmul(co_re, co_im, bre_ref[0], bim_ref[0])
    c_re, c_im = cre_ref[0], cim_ref[0]

    def powers(tau):
        m = jnp.exp(tau * (a_re * dt))
        return m * jnp.cos(tau * ang), m * jnp.sin(tau * ang)

    a_re_c, a_im_c, dt_c_ = are_c[0], aim_c[0], jnp.exp(dt_c[0])
    tau_row = lax.broadcasted_iota(jnp.int32, (P, L), 1).astype(F32)
    pm = jnp.exp(tau_row * (a_re_c * dt_c_))
    pt_re = pm * jnp.cos(tau_row * (a_im_c * dt_c_))
    pt_im = pm * jnp.sin(tau_row * (a_im_c * dt_c_))
    pair = lax.broadcasted_iota(jnp.int32, (gc * gc, gc), 0)
    col = lax.broadcasted_iota(jnp.int32, (gc * gc, gc), 1)
    rep_c = (pair // gc == col).astype(BF16)
    rep_b = (pair % gc == col).astype(BF16)
    cb_re, cb_im = _cmul(_dot_xb(rep_c, c_re), _dot_xb(rep_c, c_im),
                         _dot_xb(rep_b, bb_re), _dot_xb(rep_b, bb_im))
    kap = _dot3(cb_re, pt_re) - _dot3(cb_im, pt_im)

    kap_pad = jnp.concatenate([kap, jnp.zeros_like(kap)], axis=1)
    srow = lax.broadcasted_iota(jnp.int32, (L, LANES), 0)
    tcol = lax.broadcasted_iota(jnp.int32, (L, LANES), 1)
    for cp in range(gc):
        for c2 in range(0, gc, 2):
            k0 = jnp.broadcast_to(kap_pad[c2 * gc + cp:c2 * gc + cp + 1, :], (L, LANES))
            k1 = jnp.broadcast_to(kap_pad[(c2 + 1) * gc + cp:(c2 + 1) * gc + cp + 1, :], (L, LANES))
            t0 = pltpu.roll(k0, 0, 1, stride=1, stride_axis=0)
            t1 = pltpu.roll(k1, L, 1, stride=1, stride_axis=0)
            blk = jnp.where(tcol < L, jnp.where(tcol >= srow, t0, 0.0),
                            jnp.where(tcol - L >= srow, t1, 0.0))
            m_sc[cp * L:(cp + 1) * L, c2 * L:(c2 + 2) * L] = blk.astype(BF16)

    s_col = lax.broadcasted_iota(jnp.int32, (L, P), 0).astype(F32)
    pw_re, pw_im = powers((L - 1.0) - s_col)
    pg_re, pg_im = powers(s_col + 1.0)
    w_re, w_im, g_re, g_im = [], [], [], []
    for c in range(gc):
        br = jnp.broadcast_to(bb_re[c:c + 1, :], (L, P))
        bi = jnp.broadcast_to(bb_im[c:c + 1, :], (L, P))
        wr, wi = _cmul(br, bi, pw_re, pw_im)
        w_re.append(wr); w_im.append(wi)
        cr = jnp.broadcast_to(c_re[c:c + 1, :], (L, P))
        ci = jnp.broadcast_to(c_im[c:c + 1, :], (L, P))
        gr, gi = _cmul(cr, ci, pg_re, pg_im)
        g_re.append(gr); g_im.append(gi)
    w_re, w_im = jnp.concatenate(w_re, axis=0), jnp.concatenate(w_im, axis=0)
    g_re, g_im = jnp.concatenate(g_re, axis=0), jnp.concatenate(g_im, axis=0)

    u = pltpu.einshape("c(js)->j(cs)", u_ref[...], s=L).astype(BF16)
    x_re = jnp.dot(u, w_re.astype(BF16), preferred_element_type=F32)
    x_im = jnp.dot(u, w_im.astype(BF16), preferred_element_type=F32)
    xs_re, xs_im = [], []
    for b in range(bsz):
        xr = x_re[b * nchunk:(b + 1) * nchunk]
        xi = x_im[b * nchunk:(b + 1) * nchunk]
        d = 1
        while d < nchunk:
            ar_, ai_ = powers(float(L * d))
            sr, si = _cmul(ar_, ai_, _shift_rows(xr, d), _shift_rows(xi, d))
            xr, xi = xr + sr, xi + si
            d *= 2
        xs_re.append(_shift_rows(xr, 1))
        xs_im.append(_shift_rows(xi, 1))
    xs_re, xs_im = jnp.concatenate(xs_re, axis=0), jnp.concatenate(xs_im, axis=0)
    y = jnp.dot(u, m_sc[...], preferred_element_type=F32)
    y = y + _dot_nt(xs_re, g_re) - _dot_nt(xs_im, g_im)
    y_ref[...] = pltpu.einshape("j(cs)->c(js)", y, s=L)


def _s5_core(ut, prm, bsz, seq):
    L, gc, P = CHUNK, S5_GC, S5_P
    n_tok = ut.shape[1]
    groups = ut.shape[0] // gc
    nchunk = seq // L
    n = L * gc
    row3 = lambda a: a.reshape(groups, 1, P).astype(F32)
    col3 = lambda a: a.reshape(groups, P, 1).astype(F32)
    dt_b = jnp.broadcast_to(prm["log_dt"][:, None], (groups, P))
    args = [ut, row3(prm["a_re"]), row3(prm["a_im"]), row3(dt_b),
            col3(prm["a_re"]), col3(prm["a_im"]), col3(dt_b),
            prm["b_re"].transpose(0, 2, 1), prm["b_im"].transpose(0, 2, 1),
            prm["c_re"], prm["c_im"]]
    g3 = lambda g: (g, 0, 0)
    specs = [pl.BlockSpec((gc, n_tok), lambda g: (g, 0))]
    specs += [pl.BlockSpec((1, 1, P), g3)] * 3 + [pl.BlockSpec((1, P, 1), g3)] * 3
    specs += [pl.BlockSpec((1, gc, P), g3)] * 4
    return pl.pallas_call(
        functools.partial(_s5_kernel, bsz=bsz, nchunk=nchunk),
        grid=(groups,),
        in_specs=specs,
        out_specs=pl.BlockSpec((gc, n_tok), lambda g: (g, 0)),
        out_shape=jax.ShapeDtypeStruct(ut.shape, F32),
        scratch_shapes=[pltpu.VMEM((n, n), BF16)],
        compiler_params=_cparams("parallel"),
        name="s5_core",
    )(*args)


def _gelu_tanh(x):
    return 0.5 * x * (1.0 + jnp.tanh(0.7978845608028654 * (x + 0.044715 * x * x * x)))


def _out_kernel(ya_ref, yb_ref, yc_ref, u_ref, x_ref, gt1_ref, sh2_ref, sc2_ref, g_ref, d_ref,
                gw_ref, gbias_ref, wo_ref, wrh_ref, wrl_ref, br_ref,
                x1_ref, h2_ref, ri_ref, rw_ref, hist_ref, *, da, db):
    yc = _gelu_tanh(yc_ref[...].T + d_ref[...] * u_ref[...].T)
    yc = yc * _sigmoid(_dot(yc, gw_ref[...]) + gbias_ref[...])
    mixed = (_dot(ya_ref[...], wo_ref[0:da, :]) + _dot(yb_ref[...], wo_ref[da:da + db, :])
             + _dot(yc, wo_ref[da + db:, :]))
    x1 = x_ref[...] + (1.0 + gt1_ref[0, 0]) * mixed
    x1_ref[...] = x1
    h2 = _rmsnorm(x1, g_ref[...]) * (1.0 + sc2_ref[0, 0]) + sh2_ref[0, 0]
    h2_ref[...] = h2

    hh, hl = _split(h2)
    logits = (jnp.dot(hh, wrh_ref[...], preferred_element_type=F32)
              + jnp.dot(hh, wrl_ref[...], preferred_element_type=F32)
              + jnp.dot(hl, wrh_ref[...], preferred_element_type=F32)) + br_ref[...]
    lane_i = lax.broadcasted_iota(jnp.int32, logits.shape, 1)
    lane = lane_i.astype(F32)
    big = float(LANES)
    neg = -jnp.inf
    is_g = lane_i < N_GROUPS
    lg = jnp.where(is_g, logits, neg)
    gmax = jnp.max(lg, axis=1, keepdims=True)
    gi = jnp.min(jnp.where(is_g & (lg == gmax), lane, big), axis=1, keepdims=True)
    gp = 1.0 / jnp.sum(jnp.where(is_g, jnp.exp(lg - gmax), 0.0), axis=1, keepdims=True)
    e_lane = lane_i - N_GROUPS
    grp_of_lane = lax.shift_right_arithmetic(e_lane, 3).astype(F32)
    in_grp = (e_lane >= 0) & (e_lane < N_EXPERTS) & (grp_of_lane == gi)
    l1 = jnp.where(in_grp, logits, neg)
    m1 = jnp.max(l1, axis=1, keepdims=True)
    i1 = jnp.min(jnp.where(in_grp & (l1 == m1), lane, big), axis=1, keepdims=True)
    rest = in_grp & (lane != i1)
    l2 = jnp.where(rest, logits, neg)
    m2 = jnp.max(l2, axis=1, keepdims=True)
    i2 = jnp.min(jnp.where(rest & (l2 == m2), lane, big), axis=1, keepdims=True)
    e2 = jnp.exp(m2 - m1)
    w1 = gp / (1.0 + e2)
    w2 = gp * e2 / (1.0 + e2)
    hot1 = (lane == i1).astype(F32)
    hot2 = (lane == i2).astype(F32)
    both = (hot1 + hot2).astype(BF16)
    tm = logits.shape[0]
    rr = lax.broadcasted_iota(jnp.int32, (tm, tm), 0)
    cc = lax.broadcasted_iota(jnp.int32, (tm, tm), 1)
    before = jnp.dot((cc < rr).astype(BF16), both, preferred_element_type=F32)
    rank1 = jnp.sum(before * hot1, axis=1, keepdims=True)
    rank2 = jnp.sum(before * hot2, axis=1, keepdims=True)
    hist_ref[0] = jnp.sum(hot1 + hot2, axis=0, keepdims=True)
    ids = jnp.where(lane_i == 0, i1, jnp.where(lane_i == 1, i2, jnp.where(lane_i == 2, rank1, rank2)))
    ri_ref[...] = ids.astype(jnp.int32)
    rw_ref[...] = jnp.where(lane_i == 0, w1, jnp.where(lane_i == 1, w2, 0.0))


def _out_proj(ya, yb, yc, u, x2d, mod_l, g_ffn, s5_d, glu_w, glu_b, w_out, w_rg, b_rg, w_re, b_re, seq):
    n_tok, d = x2d.shape
    da, db, dc = ya.shape[1], yb.shape[1], yc.shape[0]
    tm = _row_tile(seq, 512)
    per_b = seq // tm
    wr = jnp.zeros((d, LANES), F32).at[:, :N_GROUPS].set(w_rg).at[:, N_GROUPS:N_GROUPS + N_EXPERTS].set(w_re)
    wr_hi = wr.astype(BF16)
    wr_lo = (wr - wr_hi.astype(F32)).astype(BF16)
    br = jnp.zeros((1, LANES), F32).at[0, :N_GROUPS].set(b_rg).at[0, N_GROUPS:N_GROUPS + N_EXPERTS].set(b_re)
    row = lambda i: (i, 0)
    const = lambda i: (0, 0)
    modspec = lambda j: pl.BlockSpec((1, 1, 1, d), lambda i: (i // per_b, j, 0, 0))
    return pl.pallas_call(
        functools.partial(_out_kernel, da=da, db=db),
        grid=(n_tok // tm,),
        in_specs=[pl.BlockSpec((tm, da), row), pl.BlockSpec((tm, db), row),
                  pl.BlockSpec((dc, tm), lambda i: (0, i)), pl.BlockSpec((dc, tm), lambda i: (0, i)),
                  pl.BlockSpec((tm, d), row),
                  modspec(2), modspec(3), modspec(4),
                  pl.BlockSpec((1, d), const), pl.BlockSpec((1, dc), const),
                  pl.BlockSpec((dc, dc), const), pl.BlockSpec((1, dc), const),
                  pl.BlockSpec((d, d), const), pl.BlockSpec((d, LANES), const),
                  pl.BlockSpec((d, LANES), const), pl.BlockSpec((1, LANES), const)],
        out_specs=[pl.BlockSpec((tm, d), row), pl.BlockSpec((tm, d), row),
                   pl.BlockSpec((tm, LANES), row), pl.BlockSpec((tm, LANES), row),
                   pl.BlockSpec((1, 1, LANES), lambda i: (i, 0, 0))],
        out_shape=[jax.ShapeDtypeStruct((n_tok, d), F32), jax.ShapeDtypeStruct((n_tok, d), F32),
                   jax.ShapeDtypeStruct((n_tok, LANES), jnp.int32),
                   jax.ShapeDtypeStruct((n_tok, LANES), F32),
                   jax.ShapeDtypeStruct((n_tok // tm, 1, LANES), F32)],
        compiler_params=_cparams("parallel"),
        name="out_proj_router",
    )(ya, yb, yc, u, x2d, mod_l, mod_l, mod_l, g_ffn, s5_d.reshape(1, dc), glu_w.astype(BF16),
      glu_b.reshape(1, dc), w_out.astype(BF16), wr_hi, wr_lo, br)


ROW_UNROLL = 16


def _route(ri, hist, n_tok, tm):
    ntile = n_tok // tm
    h = hist.reshape(ntile, LANES)[:, N_GROUPS:N_GROUPS + N_EXPERTS].astype(jnp.int32)
    counts = jnp.sum(h, axis=0)
    pcounts = (counts + MOE_ROWS - 1) // MOE_ROWS * MOE_ROWS
    pends = jnp.cumsum(pcounts)
    base = (pends - pcounts)[None, :] + jnp.cumsum(h, axis=0) - h
    n_rows = 2 * n_tok + N_EXPERTS * MOE_ROWS
    nblk = n_rows // MOE_ROWS
    blk_start = jnp.arange(nblk, dtype=jnp.int32)[:, None] * MOE_ROWS
    blk_e = jnp.sum((pends[None, :] <= blk_start).astype(jnp.int32), axis=1)
    eid = (ri[:, 0:2] - N_GROUPS).reshape(ntile, tm, 2, 1)
    hot = eid == jnp.arange(N_EXPERTS, dtype=jnp.int32)
    dest = jnp.sum(jnp.where(hot, base[:, None, None, :], 0), axis=-1) + ri[:, 2:4].reshape(ntile, tm, 2)
    return (dest.reshape(n_tok, 2).astype(jnp.int32), blk_e.astype(jnp.int32), pends.astype(jnp.int32),
            pcounts.astype(jnp.int32), n_rows)


def _tile_rows(dest, tm):
    nblk = dest.shape[0] // tm
    return dest.reshape(nblk, tm, 2).transpose(0, 2, 1).reshape(nblk, 1, 2 * tm)


def _dispatch_kernel(pend_ref, pcnt_ref, d_ref, h_ref, xs_hbm, zbuf, hbuf, sem, zsem, *, tm, n_rows):
    i = pl.program_id(0)
    n = pl.num_programs(0)
    slot = i % 2

    @pl.when(i == 0)
    def _():
        zbuf[...] = jnp.zeros_like(zbuf)

        def zcopy(start):
            return pltpu.make_async_copy(zbuf, xs_hbm.at[pl.ds(start, MOE_ROWS)], zsem.at[0])

        first_tail = lax.shift_right_logical(pend_ref[N_EXPERTS - 1], MOE_ROWS.bit_length() - 1)
        nblk = n_rows // MOE_ROWS

        def tail_start(blk, carry):
            zcopy(pl.multiple_of(blk * MOE_ROWS, MOE_ROWS)).start()
            return carry

        def tail_wait(blk, carry):
            zcopy(0).wait()
            return carry

        for e in range(N_EXPERTS):
            @pl.when(pcnt_ref[e] > 0)
            def _():
                zcopy(pl.multiple_of(pend_ref[e] - MOE_ROWS, MOE_ROWS)).start()
        lax.fori_loop(first_tail, nblk, tail_start, 0)
        for e in range(N_EXPERTS):
            @pl.when(pcnt_ref[e] > 0)
            def _():
                zcopy(0).wait()
        lax.fori_loop(first_tail, nblk, tail_wait, 0)

    def body(g, carry):
        for u in range(ROW_UNROLL):
            r = g * ROW_UNROLL + u
            src = hbuf.at[slot, pl.ds(r, 1)]
            pltpu.make_async_copy(src, xs_hbm.at[pl.ds(d_ref[0, 0, r], 1)], sem.at[slot]).start(priority=0)
            pltpu.make_async_copy(src, xs_hbm.at[pl.ds(d_ref[0, 0, tm + r], 1)], sem.at[slot]).start(priority=1)
        return carry

    def drain(s):
        for _ in range(2):
            pltpu.make_async_copy(hbuf.at[s], xs_hbm.at[pl.ds(0, tm)], sem.at[s]).wait()

    hbuf[slot] = h_ref[...].reshape(tm, SUBLANES, LANES)
    lax.fori_loop(0, tm // ROW_UNROLL, body, 0)

    @pl.when(i > 0)
    def _():
        drain(1 - slot)

    @pl.when(i == n - 1)
    def _():
        drain(slot)


def _dispatch(h2, dest, pends, pcounts, n_rows, seq):
    n_tok, d = h2.shape
    tm = _row_tile(seq, 512)
    grid_spec = pltpu.PrefetchScalarGridSpec(
        num_scalar_prefetch=2,
        grid=(n_tok // tm,),
        in_specs=[pl.BlockSpec((1, 1, 2 * tm), lambda i, pe, pc: (i, 0, 0), memory_space=pltpu.SMEM),
                  pl.BlockSpec((tm, d), lambda i, pe, pc: (i, 0))],
        out_specs=pl.BlockSpec(memory_space=pl.ANY),
        scratch_shapes=[pltpu.VMEM((MOE_ROWS, SUBLANES, LANES), F32),
                        pltpu.VMEM((2, tm, SUBLANES, LANES), F32),
                        pltpu.SemaphoreType.DMA((2,)), pltpu.SemaphoreType.DMA((1,))],
    )
    assert d == SUBLANES * LANES
    return pl.pallas_call(
        functools.partial(_dispatch_kernel, tm=tm, n_rows=n_rows),
        grid_spec=grid_spec,
        out_shape=jax.ShapeDtypeStruct((n_rows, SUBLANES, LANES), F32),
        compiler_params=_cparams("arbitrary"),
        name="moe_dispatch",
    )(pends, pcounts, _tile_rows(dest, tm), h2)


def _moe_kernel(blk_e_ref, x_ref, w1_ref, w3_ref, w2_ref, y_ref, w1_sc, w3_sc, w2_sc):
    i = pl.program_id(0)
    e = blk_e_ref[i]
    used = e < N_EXPERTS

    @pl.when(used & ((i == 0) | (e != blk_e_ref[jnp.maximum(i - 1, 0)])))
    def _():
        w1_sc[...] = w1_ref[0, 0].astype(BF16)
        w3_sc[...] = w3_ref[0, 0].astype(BF16)
        w2_sc[...] = w2_ref[0, 0].astype(BF16)

    @pl.when(used)
    def _():
        rows = x_ref.shape[0]
        xb = x_ref[...].reshape(rows, SUBLANES * LANES).astype(BF16)
        act = (_silu(jnp.dot(xb, w1_sc[...], preferred_element_type=F32))
               * jnp.dot(xb, w3_sc[...], preferred_element_type=F32))
        y = jnp.dot(act.astype(BF16), w2_sc[...], preferred_element_type=F32)
        y_ref[...] = y.reshape(rows, SUBLANES, LANES)

    @pl.when(jnp.logical_not(used))
    def _():
        y_ref[...] = jnp.zeros_like(y_ref)


def _moe_experts(xs, blk_e, w1, w3, w2, layer):
    n_rows = xs.shape[0]
    d, d_exp = w1.shape[2], w1.shape[3]
    row_blk = pl.BlockSpec((MOE_ROWS, SUBLANES, LANES), lambda i, e: (i, 0, 0))
    wmap = lambda i, e: (layer, jnp.minimum(e[i], N_EXPERTS - 1), 0, 0)
    grid_spec = pltpu.PrefetchScalarGridSpec(
        num_scalar_prefetch=1,
        grid=(n_rows // MOE_ROWS,),
        in_specs=[row_blk,
                  pl.BlockSpec((1, 1, d, d_exp), wmap), pl.BlockSpec((1, 1, d, d_exp), wmap),
                  pl.BlockSpec((1, 1, d_exp, d), wmap)],
        out_specs=row_blk,
        scratch_shapes=[pltpu.VMEM((d, d_exp), BF16), pltpu.VMEM((d, d_exp), BF16),
                        pltpu.VMEM((d_exp, d), BF16)],
    )
    return pl.pallas_call(
        _moe_kernel,
        grid_spec=grid_spec,
        out_shape=jax.ShapeDtypeStruct((n_rows, SUBLANES, LANES), F32),
        compiler_params=_cparams("arbitrary"),
        name="moe_experts",
    )(blk_e, xs, w1, w3, w2)


def _comb_kernel(d_ref, d_next_ref, x_ref, rw_ref, gt_ref, y_hbm, g_ref, o_ref, ybuf, sem, *, tm, final):
    i = pl.program_id(0)
    n = pl.num_programs(0)
    slot = i % 2

    def start_all(idx_ref, s):
        def body(g, carry):
            for u in range(ROW_UNROLL):
                r = g * ROW_UNROLL + u
                pltpu.make_async_copy(y_hbm.at[pl.ds(idx_ref[0, 0, r], 1)], ybuf.at[s, pl.ds(r, 1)],
                                      sem.at[s]).start(priority=u % 2)
            return carry
        lax.fori_loop(0, 2 * tm // ROW_UNROLL, body, 0)

    @pl.when(i == 0)
    def _():
        start_all(d_ref, 0)

    @pl.when(i + 1 < n)
    def _():
        start_all(d_next_ref, 1 - slot)

    pltpu.make_async_copy(y_hbm.at[pl.ds(0, 2 * tm)], ybuf.at[slot], sem.at[slot]).wait()

    w = rw_ref[...]
    d = x_ref.shape[1]
    moe = (w[:, 0:1] * ybuf[slot, 0:tm].reshape(tm, d)
           + w[:, 1:2] * ybuf[slot, tm:2 * tm].reshape(tm, d))
    x2 = x_ref[...] + (1.0 + gt_ref[0, 0]) * moe
    o_ref[...] = _rmsnorm(x2, g_ref[...]) if final else x2


def _combine(x1, y_rows, dest, rw, mod_l, g_final, seq, final):
    n_tok, d = x1.shape
    tm = _row_tile(seq, 512)
    per_b = seq // tm
    nblk = n_tok // tm
    idx = _tile_rows(dest, tm)
    return pl.pallas_call(
        functools.partial(_comb_kernel, tm=tm, final=final),
        grid=(nblk,),
        in_specs=[pl.BlockSpec((1, 1, 2 * tm), lambda i: (i, 0, 0), memory_space=pltpu.SMEM),
                  pl.BlockSpec((1, 1, 2 * tm), lambda i: (jnp.minimum(i + 1, nblk - 1), 0, 0),
                               memory_space=pltpu.SMEM),
                  pl.BlockSpec((tm, d), lambda i: (i, 0)),
                  pl.BlockSpec((tm, LANES), lambda i: (i, 0)),
                  pl.BlockSpec((1, 1, 1, d), lambda i: (i // per_b, 5, 0, 0)),
                  pl.BlockSpec(memory_space=pl.ANY),
                  pl.BlockSpec((1, d), lambda i: (0, 0))],
        out_specs=pl.BlockSpec((tm, d), lambda i: (i, 0)),
        out_shape=jax.ShapeDtypeStruct((n_tok, d), F32),
        scratch_shapes=[pltpu.VMEM((2, 2 * tm, SUBLANES, LANES), F32), pltpu.SemaphoreType.DMA((2,))],
        compiler_params=_cparams("arbitrary"),
        name="moe_combine",
    )(idx, idx, x1, rw, mod_l, y_rows, g_final)


def kernel(x, c, ada_w, ada_b, norm_mix, norm_ffn, norm_final, w_in, w_out, mlstm_conv_w, mlstm_conv_b, mlstm_w_q, mlstm_w_k, mlstm_b_i, mlstm_b_f, mlstm_norm_w, rwkv_mu, rwkv_w0, rwkv_w_up, rwkv_a0, rwkv_a_up, rwkv_g_up, rwkv_k_k, rwkv_k_a, rwkv_r_k, rwkv_ln_w, rwkv_ln_b, rwkv_v0, rwkv_v_dn, rwkv_v_up, s5_a_re, s5_a_im, s5_log_dt, s5_b_re, s5_b_im, s5_c_re, s5_c_im, s5_d, s5_glu_w, s5_glu_b, moe_w_rg, moe_b_rg, moe_w_re, moe_b_re, moe_w1, moe_w3, moe_w2):
    bsz, seq, d = x.shape
    depth = ada_w.shape[0]
    n_tok = bsz * seq
    heads_a = mlstm_w_q.shape[1]
    da = heads_a * HEAD_DIM
    db = rwkv_w0.shape[1]
    dc = s5_d.shape[1]
    rw_cols = rwkv_mu.shape[1]
    assert seq % CHUNK == 0 and w_in.shape[2] == 3 * da + 2 * heads_a + rw_cols + dc
    assert 2 * HEAD_DIM == LANES and CHUNK == HEAD_DIM and EXPERTS_PER_GROUP == 8
    widths = (3 * da, 2 * LANES, rw_cols, dc)

    mod = _modulation(c, ada_w, ada_b).reshape(depth, bsz, 6, 1, d)
    xc = x.reshape(n_tok, d)
    v_first = None
    for l in range(depth):
        mod_l = mod[l]
        qkvo, gates, pcols, u = _in_proj(xc, mod_l, norm_mix[l].reshape(1, d), w_in, l, widths, da,
                                         heads_a, seq)

        ya = _mlstm(qkvo, gates, mlstm_conv_w[l], mlstm_conv_b[l], mlstm_w_q[l], mlstm_w_k[l],
                    mlstm_b_i[l], mlstm_b_f[l], mlstm_norm_w[l], bsz, seq)
        rprm = dict(mu=rwkv_mu[l], w0=rwkv_w0[l], w_up=rwkv_w_up[l], a0=rwkv_a0[l], a_up=rwkv_a_up[l],
                    g_up=rwkv_g_up[l], k_k=rwkv_k_k[l], k_a=rwkv_k_a[l], r_k=rwkv_r_k[l],
                    ln_w=rwkv_ln_w[l], ln_b=rwkv_ln_b[l])
        if l > 0:
            rprm.update(v0=rwkv_v0[l - 1], v_dn=rwkv_v_dn[l - 1], v_up=rwkv_v_up[l - 1])
        yb, v_first = _rwkv(pcols, v_first if l > 0 else None, rprm, bsz, seq)
        sprm = dict(a_re=s5_a_re[l], a_im=s5_a_im[l], log_dt=s5_log_dt[l], b_re=s5_b_re[l],
                    b_im=s5_b_im[l], c_re=s5_c_re[l], c_im=s5_c_im[l])
        yc = _s5_core(u, sprm, bsz, seq)

        x1, h2, ri, rw, hist = _out_proj(ya, yb, yc, u, xc, mod_l, norm_ffn[l].reshape(1, d), s5_d[l],
                                         s5_glu_w[l], s5_glu_b[l], w_out[l], moe_w_rg[l], moe_b_rg[l],
                                         moe_w_re[l], moe_b_re[l], seq)
        dest, blk_e, pends, pcounts, n_rows = _route(ri, hist, n_tok, n_tok // hist.shape[0])
        y_rows = _moe_experts(_dispatch(h2, dest, pends, pcounts, n_rows, seq), blk_e, moe_w1, moe_w3,
                              moe_w2, l)
        xc = _combine(x1, y_rows, dest, rw, mod_l, norm_final.reshape(1, d), seq, final=(l == depth - 1))
    return xc.reshape(bsz, seq, d).astype(x.dtype)
```

```python
import functools

import jax
import jax.numpy as jnp
from jax import lax
from jax.experimental import pallas as pl
from jax.experimental.pallas import tpu as pltpu

F32 = jnp.float32
BF16 = jnp.bfloat16

HEAD_DIM = 64
CHUNK = 64
CONV_K = 4
S5_GC = 16
S5_P = 64
N_GROUPS = 4
EXPERTS_PER_GROUP = 8
N_EXPERTS = N_GROUPS * EXPERTS_PER_GROUP
NORM_EPS = 1e-6
HEAD_NORM_EPS = 1e-5
RWKV_GN_EPS = 64e-5
L2_EPS = 1e-12
LANES = 128
SUBLANES = 8
MOE_ROWS = 512
VMEM_LIMIT = 56 * 1024 * 1024


def _cparams(*sem):
    return pltpu.CompilerParams(dimension_semantics=sem, vmem_limit_bytes=VMEM_LIMIT)


def _row_tile(n, want):
    t = min(n, want)
    assert n % t == 0
    return t


def _dot(a, b):
    return jnp.dot(a.astype(BF16), b.astype(BF16), preferred_element_type=F32)


def _dot_nt(a, b):
    return lax.dot_general(a.astype(BF16), b.astype(BF16), (((1,), (1,)), ((), ())),
                           preferred_element_type=F32)


def _dot_tn(a, b):
    return lax.dot_general(a.astype(BF16), b.astype(BF16), (((0,), (0,)), ((), ())),
                           preferred_element_type=F32)


def _split(a):
    hi = a.astype(BF16)
    lo = (a - hi.astype(F32)).astype(BF16)
    return hi, lo


def _dot_xa(a, b_exact):
    hi, lo = _split(a)
    return (jnp.dot(hi, b_exact, preferred_element_type=F32)
            + jnp.dot(lo, b_exact, preferred_element_type=F32))


def _dot_xb(a_exact, b):
    hi, lo = _split(b)
    return (jnp.dot(a_exact, hi, preferred_element_type=F32)
            + jnp.dot(a_exact, lo, preferred_element_type=F32))


def _dot_x3(a, b_exact):
    hi = a.astype(BF16)
    r1 = a - hi.astype(F32)
    mid = r1.astype(BF16)
    lo = (r1 - mid.astype(F32)).astype(BF16)
    return (jnp.dot(hi, b_exact, preferred_element_type=F32)
            + jnp.dot(mid, b_exact, preferred_element_type=F32)
            + jnp.dot(lo, b_exact, preferred_element_type=F32))


def _dot3(a, b):
    ah, al = _split(a)
    bh, bl = _split(b)
    return (jnp.dot(ah, bh, preferred_element_type=F32)
            + jnp.dot(ah, bl, preferred_element_type=F32)
            + jnp.dot(al, bh, preferred_element_type=F32))


def _sigmoid(x):
    return 1.0 / (1.0 + jnp.exp(-x))


def _silu(x):
    return x * _sigmoid(x)


def _log_sigmoid(x):
    return jnp.minimum(x, 0.0) - jnp.log1p(jnp.exp(-jnp.abs(x)))


def _rmsnorm(x, g):
    ms = jnp.mean(x * x, axis=-1, keepdims=True)
    return x * lax.rsqrt(ms + NORM_EPS) * g


def _tri_incl(n):
    r = lax.broadcasted_iota(jnp.int32, (n, n), 0)
    c = lax.broadcasted_iota(jnp.int32, (n, n), 1)
    return (c <= r).astype(BF16)


def _head_ones(width):
    r = lax.broadcasted_iota(jnp.int32, (width, width), 0) // HEAD_DIM
    c = lax.broadcasted_iota(jnp.int32, (width, width), 1) // HEAD_DIM
    return (r == c).astype(BF16)


def _mod_kernel(c_ref, w_ref, b_ref, o_ref):
    o_ref[0] = _dot(_silu(c_ref[...]), w_ref[0]) + b_ref[0]


def _modulation(c, ada_w, ada_b):
    depth, d, d6 = ada_w.shape
    bsz = c.shape[0]
    tn = _row_tile(d6, 1024)
    return pl.pallas_call(
        _mod_kernel,
        grid=(depth, d6 // tn),
        in_specs=[pl.BlockSpec((bsz, d), lambda l, j: (0, 0)),
                  pl.BlockSpec((1, d, tn), lambda l, j: (l, 0, j)),
                  pl.BlockSpec((1, 1, tn), lambda l, j: (l, 0, j))],
        out_specs=pl.BlockSpec((1, bsz, tn), lambda l, j: (l, 0, j)),
        out_shape=jax.ShapeDtypeStruct((depth, bsz, d6), F32),
        compiler_params=_cparams("parallel", "parallel"),
        name="adaln_mod",
    )(c, ada_w, ada_b.reshape(depth, 1, d6))


def _in_kernel(x_ref, sh_ref, sc_ref, g_ref, w_ref, *refs, widths, da, heads):
    out_refs, w_sc = refs[:-1], refs[-1]
    d = w_ref.shape[1]
    rw_cols, dc = widths[2], widths[3]

    @pl.when(pl.program_id(0) == 0)
    def _():
        rb = 256
        g0 = 3 * da
        r_src = g0 + 2 * heads
        u_src = r_src + rw_cols
        r_win = -(-(2 * heads + rw_cols) // LANES) * LANES
        u_al = u_src // LANES * LANES
        lane = lax.broadcasted_iota(jnp.int32, (rb, LANES), 1)
        for r0 in range(0, d, rb):
            rs = slice(r0, r0 + rb)
            w_sc[rs, 0:g0] = w_ref[0, rs, 0:g0].astype(BF16)
            gblk = w_ref[0, rs, g0:g0 + LANES]
            w_sc[rs, g0:g0 + LANES] = jnp.where(lane < heads, gblk, 0.0).astype(BF16)
            w_sc[rs, g0 + LANES:g0 + 2 * LANES] = jnp.where(
                lane < heads, pltpu.roll(gblk, LANES - heads, 1), 0.0).astype(BF16)
            win = w_ref[0, rs, g0:g0 + r_win]
            w_sc[rs, g0 + 2 * LANES:g0 + 2 * LANES + rw_cols] = win[:, 2 * heads:2 * heads + rw_cols].astype(BF16)
            win2 = w_ref[0, rs, u_al:u_src + dc]
            w_sc[rs, g0 + 2 * LANES + rw_cols:] = win2[:, u_src - u_al:u_src - u_al + dc].astype(BF16)

    h = _rmsnorm(x_ref[...], g_ref[...]) * (1.0 + sc_ref[0, 0]) + sh_ref[0, 0]
    hb = h.astype(BF16)
    off = 0
    for k, (o_ref, wd) in enumerate(zip(out_refs, widths)):
        res = jnp.dot(hb, w_sc[:, off:off + wd], preferred_element_type=F32)
        o_ref[...] = res.T if k == len(widths) - 1 else res
        off += wd


def _in_proj(x2d, mod_l, g, w_in, layer, widths, da, heads, seq):
    n_tok, d = x2d.shape
    cols = w_in.shape[2]
    tm = _row_tile(seq, 1024)
    per_b = seq // tm
    return pl.pallas_call(
        functools.partial(_in_kernel, widths=widths, da=da, heads=heads),
        grid=(n_tok // tm,),
        in_specs=[pl.BlockSpec((tm, d), lambda i: (i, 0)),
                  pl.BlockSpec((1, 1, 1, d), lambda i: (i // per_b, 0, 0, 0)),
                  pl.BlockSpec((1, 1, 1, d), lambda i: (i // per_b, 1, 0, 0)),
                  pl.BlockSpec((1, d), lambda i: (0, 0)),
                  pl.BlockSpec((1, d, cols), lambda i: (layer, 0, 0), pipeline_mode=pl.Buffered(1))],
        out_specs=[pl.BlockSpec((tm, wd), lambda i: (i, 0)) for wd in widths[:-1]]
        + [pl.BlockSpec((widths[-1], tm), lambda i: (0, i))],
        out_shape=[jax.ShapeDtypeStruct((n_tok, wd), F32) for wd in widths[:-1]]
        + [jax.ShapeDtypeStruct((widths[-1], n_tok), F32)],
        scratch_shapes=[pltpu.VMEM((d, sum(widths)), BF16)],
        compiler_params=_cparams("arbitrary"),
        name="in_proj",
    )(x2d, mod_l, mod_l, g, w_in)


def _mlstm_kernel(qkvo_ref, gate_ref, cw_ref, cb_ref, wq_ref, wk_ref, gb_ref, nw_ref, out_ref,
                  xf_sc, q_sc, k_sc, gi_sc, gf_sc, cn_sc, m_sc, *, nb, tb, heads):
    dh, L = HEAD_DIM, CHUNK
    da = heads * dh
    i = pl.program_id(0)

    @pl.when(i == 0)
    def _():
        xf_sc[:, 0:SUBLANES, :] = jnp.zeros((nb, SUBLANES, da), F32)
        cn_sc[...] = jnp.zeros_like(cn_sc)
        m_sc[...] = jnp.zeros_like(m_sc)

    for b in range(nb):
        xqk = qkvo_ref[b, :, 0:da]
        xf_sc[b, pl.ds(SUBLANES, tb), :] = xqk
        acc = xqk * cw_ref[CONV_K - 1:CONV_K, :] + cb_ref[...]
        for j in range(1, CONV_K):
            acc = acc + xf_sc[b, pl.ds(SUBLANES - j, tb), :] * cw_ref[CONV_K - 1 - j:CONV_K - j, :]
        xf_sc[b, 0:SUBLANES, :] = xf_sc[b, pl.ds(tb, SUBLANES), :]
        cx = _silu(acc).astype(BF16)
        q_sc[b] = jnp.dot(cx, wq_ref[...], preferred_element_type=F32)
        k_sc[b] = jnp.dot(cx, wk_ref[...], preferred_element_type=F32) * (dh ** -0.5)

        g = gate_ref[b] + gb_ref[...]
        gi_sc[b] = g[:, :LANES]
        gf_sc[b] = _log_sigmoid(g[:, LANES:])

    tri = _tri_incl(L)
    rr = lax.broadcasted_iota(jnp.int32, (L, L), 0)
    cc = lax.broadcasted_iota(jnp.int32, (L, L), 1)
    causal = cc <= rr
    row_l = lax.broadcasted_iota(jnp.int32, (L, LANES), 0)
    sel = (lax.broadcasted_iota(jnp.int32, (LANES, heads * LANES), 0)
           == lax.broadcasted_iota(jnp.int32, (LANES, heads * LANES), 1) // LANES).astype(BF16)
    ones_v = jnp.ones((L, dh), F32)
    mean_m = jnp.full((dh, dh), 1.0 / dh, BF16)

    units = [(b, h) for b in range(nb) for h in range(heads)]
    ur = range(len(units))

    def hcol(h, base=0):
        return slice(base + h * dh, base + (h + 1) * dh)

    def chunk(c, carry):
        rows = pl.ds(pl.multiple_of(c * L, L), L)
        ex, g_t = [], []
        for b in range(nb):
            bc = _dot_xb(tri, gf_sc[b, rows, :])
            gtot = bc[L - 1:L, :]
            g = gi_sc[b, rows, :] - bc
            gmax = jnp.max(g, axis=0, keepdims=True)
            m_prev = m_sc[b]
            m_loc = gtot + gmax
            m_new = jnp.maximum(gtot + m_prev, m_loc)
            a_old = jnp.exp(gtot + m_prev - m_new)
            a_loc = jnp.exp(m_loc - m_new)
            m_sc[b] = m_new
            pm = g
            d = 1
            while d < L:
                pm = jnp.maximum(pm, jnp.where(row_l >= d, pltpu.roll(pm, d, 0), -jnp.inf))
                d *= 2
            mm = jnp.maximum(m_prev, pm)
            stack = jnp.concatenate([jnp.exp(g - gmax), mm, jnp.exp(m_prev - mm), jnp.exp(-(bc + mm))], axis=0)
            scal = jnp.concatenate([a_old, a_loc, jnp.zeros((SUBLANES - 2, LANES), F32)], axis=0)
            ex.append(jnp.concatenate([_dot_xa(stack, sel), _dot_x3(scal, sel)], axis=0))
            g_t.append(g.T)

        def part(k, b, h, width=dh):
            return ex[b][k * L:(k + 1) * L, h * LANES:h * LANES + width]

        qc = [q_sc[b, rows, hcol(h)] for b, h in units]
        kc = [k_sc[b, rows, hcol(h)] for b, h in units]
        vo = [jnp.concatenate([qkvo_ref[b, rows, hcol(h, da)], ones_v], axis=1) for b, h in units]
        oc = [qkvo_ref[b, rows, hcol(h, 2 * da)] for b, h in units]
        cn_prev = [cn_sc[b, h] for b, h in units]
        s_raw = [_dot_nt(qc[u], kc[u]) for u in ur]
        q_cn = [_dot(qc[u], cn_prev[u]) for u in ur]
        cn_loc = [_dot_tn(kc[u] * part(0, b, h), vo[u]) for u, (b, h) in enumerate(units)]
        s_qk = [s_raw[u] * jnp.where(causal, jnp.exp(g_t[b][h:h + 1, :] - part(1, b, h)), 0.0)
                for u, (b, h) in enumerate(units)]
        s_vn = [_dot(s_qk[u], vo[u]) for u in ur]
        nd = [part(2, b, h, 2 * dh) * q_cn[u] + s_vn[u] for u, (b, h) in enumerate(units)]
        hh = [nd[u][:, :dh] / jnp.maximum(jnp.abs(nd[u][:, dh:]), part(3, b, h))
              for u, (b, h) in enumerate(units)]
        mu = [_dot_xa(hh[u], mean_m) for u in ur]
        dlt = [hh[u] - mu[u] for u in ur]
        var = [_dot_xa(dlt[u] * dlt[u], mean_m) for u in ur]
        outs = [dlt[u] * lax.rsqrt(var[u] + HEAD_NORM_EPS) * _sigmoid(oc[u]) for u in ur]
        for b in range(nb):
            mine = [u for u in ur if units[u][0] == b]
            out_ref[b, rows, :] = jnp.concatenate([outs[u] for u in mine], axis=1) * nw_ref[...]
        for u, (b, h) in enumerate(units):
            cn_sc[b, h] = (ex[b][4 * L:4 * L + 1, h * LANES:(h + 1) * LANES] * cn_prev[u]
                           + ex[b][4 * L + 1:4 * L + 2, h * LANES:(h + 1) * LANES] * cn_loc[u])
        return carry

    lax.fori_loop(0, tb // L, chunk, 0)


def _block_diag(w):
    heads, dh, _ = w.shape
    eye = jnp.eye(heads, dtype=w.dtype)
    return (eye[:, None, :, None] * w[:, :, None, :]).reshape(heads * dh, heads * dh)


def _mlstm(qkvo, gates, conv_w, conv_b, w_q, w_k, b_i, b_f, norm_w, bsz, seq):
    heads = w_q.shape[0]
    da = heads * HEAD_DIM
    tb = _row_tile(seq, 256)
    nblk = seq // tb
    gbias = jnp.zeros((1, 2 * LANES), F32).at[0, :heads].set(b_i).at[0, LANES:LANES + heads].set(b_f)
    kern = functools.partial(_mlstm_kernel, nb=bsz, tb=tb, heads=heads)
    row = lambda i: (0, i, 0)
    const = lambda i: (0, 0)
    out = pl.pallas_call(
        kern,
        grid=(nblk,),
        in_specs=[pl.BlockSpec((bsz, tb, 3 * da), row),
                  pl.BlockSpec((bsz, tb, 2 * LANES), row),
                  pl.BlockSpec((CONV_K, da), const),
                  pl.BlockSpec((1, da), const),
                  pl.BlockSpec((da, da), const),
                  pl.BlockSpec((da, da), const),
                  pl.BlockSpec((1, 2 * LANES), const),
                  pl.BlockSpec((1, da), const)],
        out_specs=pl.BlockSpec((bsz, tb, da), row),
        out_shape=jax.ShapeDtypeStruct((bsz, seq, da), F32),
        scratch_shapes=[pltpu.VMEM((bsz, tb + SUBLANES, da), F32),
                        pltpu.VMEM((bsz, tb, da), F32),
                        pltpu.VMEM((bsz, tb, da), F32),
                        pltpu.VMEM((bsz, tb, LANES), F32),
                        pltpu.VMEM((bsz, tb, LANES), F32),
                        pltpu.VMEM((bsz, heads, HEAD_DIM, 2 * HEAD_DIM), F32),
                        pltpu.VMEM((bsz, 1, LANES), F32)],
        compiler_params=_cparams("arbitrary"),
        name="mlstm",
    )(qkvo.reshape(bsz, seq, 3 * da), gates.reshape(bsz, seq, 2 * LANES), conv_w, conv_b.reshape(1, da),
      _block_diag(w_q).astype(BF16), _block_diag(w_k).astype(BF16), gbias, norm_w.reshape(1, da))
    return out.reshape(bsz * seq, da)


def _rwkv_kernel(*refs, nb, tb, heads, lw_dim, la_dim, lg_dim, has_vres):
    dh, L = HEAD_DIM, CHUNK
    db = heads * dh
    it = iter(refs)
    p_ref = next(it)
    vf_ref = next(it) if has_vres else None
    (mu_ref, w0_ref, wup_ref, a0_ref, aup_ref, gup_ref, kk_ref, ka_ref, rk_ref,
     lnw_ref, lnb_ref) = (next(it) for _ in range(11))
    if has_vres:
        v0_ref, vdn_ref, vup_ref = (next(it) for _ in range(3))
    y_ref = next(it)
    vout_ref = None if has_vres else next(it)
    (xf_sc, r_sc, k_sc, v_sc, a_sc, b_sc, lw_sc, y_sc, gate_sc, rp_sc, st_sc, q_sc, z_sc,
     gl_sc) = (next(it) for _ in range(14))

    i = pl.program_id(0)

    @pl.when(i == 0)
    def _():
        xf_sc[:, 0:SUBLANES, :] = jnp.zeros((nb, SUBLANES, xf_sc.shape[2]), F32)
        st_sc[...] = jnp.zeros_like(st_sc)

    ones_h = _head_ones(db)
    for b in range(nb):
        p = p_ref[b]
        xf_sc[b, pl.ds(SUBLANES, tb), :] = p
        prev = xf_sc[b, pl.ds(SUBLANES - 1, tb), :]
        xf_sc[b, 0:SUBLANES, :] = xf_sc[b, pl.ds(tb, SUBLANES), :]
        p = p + mu_ref[...] * (prev - p)

        o = 0
        r = p[:, o:o + db]; o += db
        k = p[:, o:o + db]; o += db
        v = p[:, o:o + db]; o += db
        wd = p[:, o:o + lw_dim]; o += lw_dim
        ad = p[:, o:o + la_dim]; o += la_dim
        gd = p[:, o:o + lg_dim]

        wlog = _log_sigmoid(w0_ref[...] + _dot(jnp.tanh(wd), wup_ref[...])) - 0.5
        lw_sc[b] = -jnp.exp(wlog)
        a = _sigmoid(a0_ref[...] + _dot(ad, aup_ref[...]))
        gate_sc[b] = _dot(_sigmoid(gd), gup_ref[...])
        if has_vres:
            v = v + (vf_ref[b] - v) * _sigmoid(v0_ref[...] + _dot(_dot(v, vdn_ref[...]), vup_ref[...]))
        else:
            vout_ref[b] = v
        kk = k * kk_ref[...]
        kk = kk / jnp.maximum(jnp.sqrt(_dot_xa(kk * kk, ones_h)), L2_EPS)
        r_sc[b] = r
        k_sc[b] = k * (1.0 + (a - 1.0) * ka_ref[...])
        v_sc[b] = v
        a_sc[b] = -kk
        b_sc[b] = kk * a

    tri = _tri_incl(L)
    rr = lax.broadcasted_iota(jnp.int32, (L, L), 0)
    cc = lax.broadcasted_iota(jnp.int32, (L, L), 1)
    strict = cc < rr
    incl = cc <= rr

    units = [(b, h) for b in range(nb) for h in range(heads)]
    ur = range(len(units))

    def hcol(h):
        return slice(h * dh, (h + 1) * dh)

    def chunk(c, carry):
        rows = pl.ds(pl.multiple_of(c * L, L), L)
        at, rt, bt, kt, bg, kg, vch = ([] for _ in range(7))
        for b in range(nb):
            lwc = lw_sc[b, rows, :]
            cum = _dot_xb(tri, lwc)
            cum_l = cum[L - 1:L, :]
            e_inv = jnp.exp(-cum)
            e_end = jnp.exp(cum_l - cum)
            bv = b_sc[b, rows, :]
            kv = k_sc[b, rows, :]
            at.append(a_sc[b, rows, :] * jnp.exp(cum - lwc))
            rt.append(r_sc[b, rows, :] * jnp.exp(cum))
            bt.append(bv * e_inv)
            kt.append(kv * e_inv)
            bg.append(bv * e_end)
            kg.append(kv * e_end)
            vch.append(v_sc[b, rows, :])
            gl_sc[c, b] = jnp.exp(cum_l)
        vh = [vch[b][:, hcol(h)] for b, h in units]
        g4 = [_dot_nt(jnp.concatenate([at[b][:, hcol(h)], rt[b][:, hcol(h)]], axis=0),
                      jnp.concatenate([bt[b][:, hcol(h)], kt[b][:, hcol(h)]], axis=0))
              for b, h in units]
        pw = [jnp.where(strict, g4[u][:L, :L], 0.0) for u in ur]
        n_ak = [jnp.where(strict, g4[u][:L, L:], 0.0) for u in ur]
        m_rb = [jnp.where(incl, g4[u][L:, :L], 0.0) for u in ur]
        m_rk = [jnp.where(incl, g4[u][L:, L:], 0.0) for u in ur]
        nv = [_dot(jnp.concatenate([n_ak[u], m_rk[u]], axis=0), vh[u]) for u in ur]
        x = [jnp.concatenate([at[b][:, hcol(h)], nv[u][:L]], axis=1)
             for u, (b, h) in enumerate(units)]
        for step in range(6):
            if step < 5:
                px = [_dot(pw[u], jnp.concatenate([x[u], pw[u]], axis=1)) for u in ur]
                x = [x[u] + px[u][:, :2 * dh] for u in ur]
                pw = [px[u][:, 2 * dh:] for u in ur]
            else:
                x = [x[u] + _dot(pw[u], x[u]) for u in ur]
        ry = [jnp.concatenate([rt[b][:, hcol(h)], nv[u][L:]], axis=1) + _dot(m_rb[u], x[u])
              for u, (b, h) in enumerate(units)]
        qz = [_dot_tn(x[u], bg[b][:, hcol(h)]) for u, (b, h) in enumerate(units)]
        z2 = [_dot_tn(vh[u], kg[b][:, hcol(h)]) for u, (b, h) in enumerate(units)]
        for b in range(nb):
            mine = [u for u in ur if units[u][0] == b]
            rp_sc[b, rows, :] = jnp.concatenate([ry[u][:, :dh] for u in mine], axis=1)
            y_sc[b, rows, :] = jnp.concatenate([ry[u][:, dh:] for u in mine], axis=1)
        for u, (b, h) in enumerate(units):
            q_sc[c, b, h] = qz[u][:dh]
            z_sc[c, b, h] = qz[u][dh:] + z2[u]
        return carry

    lax.fori_loop(0, tb // L, chunk, 0)

    def carry_state(c, carry):
        rows = pl.ds(pl.multiple_of(c * L, L), L)
        st = [st_sc[b, h] for b, h in units]
        ys = [_dot_nt(rp_sc[b, rows, hcol(h)], st[u]) for u, (b, h) in enumerate(units)]
        sq = [_dot_xa(st[u], q_sc[c, b, h].astype(BF16)) for u, (b, h) in enumerate(units)]
        for b in range(nb):
            mine = [u for u in ur if units[u][0] == b]
            y_sc[b, rows, :] = y_sc[b, rows, :] + jnp.concatenate([ys[u] for u in mine], axis=1)
        for u, (b, h) in enumerate(units):
            st_sc[b, h] = st[u] * gl_sc[c, b][:, hcol(h)] + sq[u] + z_sc[c, b, h]
        return carry

    lax.fori_loop(0, tb // L, carry_state, 0)

    for b in range(nb):
        y = y_sc[b]
        mean = _dot_xa(y, ones_h) * (1.0 / dh)
        dlt = y - mean
        var = _dot_xa(dlt * dlt, ones_h) * (1.0 / dh)
        yn = dlt * lax.rsqrt(var + RWKV_GN_EPS) * lnw_ref[...] + lnb_ref[...]
        bonus = _dot_xa(r_sc[b] * k_sc[b] * rk_ref[...], ones_h) * v_sc[b]
        y_ref[b] = (yn + bonus) * gate_sc[b]


def _rwkv(pcols, v_first, prm, bsz, seq):
    db = prm["w0"].shape[0]
    heads = db // HEAD_DIM
    cols = pcols.shape[1]
    lw_dim, la_dim, lg_dim = prm["w_up"].shape[0], prm["a_up"].shape[0], prm["g_up"].shape[0]
    has_vres = v_first is not None
    tb = _row_tile(seq, 256)
    nblk = seq // tb
    row = lambda i: (0, i, 0)
    const = lambda i: (0, 0)
    vec = lambda a: a.reshape(1, -1).astype(F32)

    args = [pcols.reshape(bsz, seq, cols)]
    specs = [pl.BlockSpec((bsz, tb, cols), row)]
    if has_vres:
        args.append(v_first.reshape(bsz, seq, db))
        specs.append(pl.BlockSpec((bsz, tb, db), row))
    small = [vec(prm["mu"]), vec(prm["w0"]), prm["w_up"].astype(BF16), vec(prm["a0"]),
             prm["a_up"].astype(BF16), prm["g_up"].astype(BF16), vec(prm["k_k"]), vec(prm["k_a"]),
             vec(prm["r_k"]), vec(prm["ln_w"]), vec(prm["ln_b"])]
    if has_vres:
        lv = prm["v_dn"].shape[1]
        lvp = -(-lv // LANES) * LANES
        v_dn = jnp.zeros((db, lvp), F32).at[:, :lv].set(prm["v_dn"]).astype(BF16)
        v_up = jnp.zeros((lvp, db), F32).at[:lv, :].set(prm["v_up"]).astype(BF16)
        small += [vec(prm["v0"]), v_dn, v_up]
    args += small
    specs += [pl.BlockSpec(a.shape, const) for a in small]

    out_shape = [jax.ShapeDtypeStruct((bsz, seq, db), F32)]
    out_specs = [pl.BlockSpec((bsz, tb, db), row)]
    if not has_vres:
        out_shape.append(jax.ShapeDtypeStruct((bsz, seq, db), F32))
        out_specs.append(pl.BlockSpec((bsz, tb, db), row))

    kern = functools.partial(_rwkv_kernel, nb=bsz, tb=tb, heads=heads, lw_dim=lw_dim, la_dim=la_dim,
                             lg_dim=lg_dim, has_vres=has_vres)
    nch = tb // CHUNK
    res = pl.pallas_call(
        kern,
        grid=(nblk,),
        in_specs=specs,
        out_specs=out_specs,
        out_shape=out_shape,
        scratch_shapes=[pltpu.VMEM((bsz, tb + SUBLANES, cols), F32)]
        + [pltpu.VMEM((bsz, tb, db), F32) for _ in range(9)]
        + [pltpu.VMEM((bsz, heads, HEAD_DIM, HEAD_DIM), F32),
           pltpu.VMEM((nch, bsz, heads, HEAD_DIM, HEAD_DIM), F32),
           pltpu.VMEM((nch, bsz, heads, HEAD_DIM, HEAD_DIM), F32),
           pltpu.VMEM((nch, bsz, 1, db), F32)],
        compiler_params=_cparams("arbitrary"),
        name="rwkv7",
    )(*args)
    res = [a.reshape(bsz * seq, db) for a in res]
    return (res[0], v_first) if has_vres else (res[0], res[1])


def _cmul(ar, ai, br, bi):
    return ar * br - ai * bi, ar * bi + ai * br


def _shift_rows(x, d):
    row = lax.broadcasted_iota(jnp.int32, x.shape, 0)
    return jnp.where(row >= d, pltpu.roll(x, d, 0), 0.0)


def _s5_kernel(u_ref, are_r, aim_r, dt_r, are_c, aim_c, dt_c, bre_ref, bim_ref, cre_ref, cim_ref,
               y_ref, m_sc, *, bsz, nchunk):
    L, gc, P = CHUNK, S5_GC, S5_P
    n = L * gc
    a_re, a_im, dt = are_r[0], aim_r[0], jnp.exp(dt_r[0])
    mag, ang = jnp.exp(a_re * dt), a_im * dt
    ab_re, ab_im = mag * jnp.cos(ang), mag * jnp.sin(ang)
    inv = 1.0 / (a_re * a_re + a_im * a_im)
    co_re = ((ab_re - 1.0) * a_re + ab_im * a_im) * inv
    co_im = (ab_im * a_re - (ab_re - 1.0) * a_im) * inv
    bb_re, bb_im = _cmul(co_re, co_im, bre_ref[0], bim_ref[0])
    c_re, c_im = cre_ref[0], cim_ref[0]

    def powers(tau):
        m = jnp.exp(tau * (a_re * dt))
        return m * jnp.cos(tau * ang), m * jnp.sin(tau * ang)

    a_re_c, a_im_c, dt_c_ = are_c[0], aim_c[0], jnp.exp(dt_c[0])
    tau_row = lax.broadcasted_iota(jnp.int32, (P, L), 1).astype(F32)
    pm = jnp.exp(tau_row * (a_re_c * dt_c_))
    pt_re = pm * jnp.cos(tau_row * (a_im_c * dt_c_))
    pt_im = pm * jnp.sin(tau_row * (a_im_c * dt_c_))
    pair = lax.broadcasted_iota(jnp.int32, (gc * gc, gc), 0)
    col = lax.broadcasted_iota(jnp.int32, (gc * gc, gc), 1)
    rep_c = (pair // gc == col).astype(BF16)
    rep_b = (pair % gc == col).astype(BF16)
    cb_re, cb_im = _cmul(_dot_xb(rep_c, c_re), _dot_xb(rep_c, c_im),
                         _dot_xb(rep_b, bb_re), _dot_xb(rep_b, bb_im))
    kap = _dot3(cb_re, pt_re) - _dot3(cb_im, pt_im)

    kap_pad = jnp.concatenate([kap, jnp.zeros_like(kap)], axis=1)
    srow = lax.broadcasted_iota(jnp.int32, (L, LANES), 0)
    tcol = lax.broadcasted_iota(jnp.int32, (L, LANES), 1)
    for cp in range(gc):
        for c2 in range(0, gc, 2):
            k0 = jnp.broadcast_to(kap_pad[c2 * gc + cp:c2 * gc + cp + 1, :], (L, LANES))
            k1 = jnp.broadcast_to(kap_pad[(c2 + 1) * gc + cp:(c2 + 1) * gc + cp + 1, :], (L, LANES))
            t0 = pltpu.roll(k0, 0, 1, stride=1, stride_axis=0)
            t1 = pltpu.roll(k1, L, 1, stride=1, stride_axis=0)
            blk = jnp.where(tcol < L, jnp.where(tcol >= srow, t0, 0.0),
                            jnp.where(tcol - L >= srow, t1, 0.0))
            m_sc[cp * L:(cp + 1) * L, c2 * L:(c2 + 2) * L] = blk.astype(BF16)

    s_col = lax.broadcasted_iota(jnp.int32, (L, P), 0).astype(F32)
    pw_re, pw_im = powers((L - 1.0) - s_col)
    pg_re, pg_im = powers(s_col + 1.0)
    w_re, w_im, g_re, g_im = [], [], [], []
    for c in range(gc):
        br = jnp.broadcast_to(bb_re[c:c + 1, :], (L, P))
        bi = jnp.broadcast_to(bb_im[c:c + 1, :], (L, P))
        wr, wi = _cmul(br, bi, pw_re, pw_im)
        w_re.append(wr); w_im.append(wi)
        cr = jnp.broadcast_to(c_re[c:c + 1, :], (L, P))
        ci = jnp.broadcast_to(c_im[c:c + 1, :], (L, P))
        gr, gi = _cmul(cr, ci, pg_re, pg_im)
        g_re.append(gr); g_im.append(gi)
    w_re, w_im = jnp.concatenate(w_re, axis=0), jnp.concatenate(w_im, axis=0)
    g_re, g_im = jnp.concatenate(g_re, axis=0), jnp.concatenate(g_im, axis=0)

    u = pltpu.einshape("c(js)->j(cs)", u_ref[...], s=L).astype(BF16)
    x_re = jnp.dot(u, w_re.astype(BF16), preferred_element_type=F32)
    x_im = jnp.dot(u, w_im.astype(BF16), preferred_element_type=F32)
    xs_re, xs_im = [], []
    for b in range(bsz):
        xr = x_re[b * nchunk:(b + 1) * nchunk]
        xi = x_im[b * nchunk:(b + 1) * nchunk]
        d = 1
        while d < nchunk:
            ar_, ai_ = powers(float(L * d))
            sr, si = _cmul(ar_, ai_, _shift_rows(xr, d), _shift_rows(xi, d))
            xr, xi = xr + sr, xi + si
            d *= 2
        xs_re.append(_shift_rows(xr, 1))
        xs_im.append(_shift_rows(xi, 1))
    xs_re, xs_im = jnp.concatenate(xs_re, axis=0), jnp.concatenate(xs_im, axis=0)
    y = jnp.dot(u, m_sc[...], preferred_element_type=F32)
    y = y + _dot_nt(xs_re, g_re) - _dot_nt(xs_im, g_im)
    y_ref[...] = pltpu.einshape("j(cs)->c(js)", y, s=L)


def _s5_core(ut, prm, bsz, seq):
    L, gc, P = CHUNK, S5_GC, S5_P
    n_tok = ut.shape[1]
    groups = ut.shape[0] // gc
    nchunk = seq // L
    n = L * gc
    row3 = lambda a: a.reshape(groups, 1, P).astype(F32)
    col3 = lambda a: a.reshape(groups, P, 1).astype(F32)
    dt_b = jnp.broadcast_to(prm["log_dt"][:, None], (groups, P))
    args = [ut, row3(prm["a_re"]), row3(prm["a_im"]), row3(dt_b),
            col3(prm["a_re"]), col3(prm["a_im"]), col3(dt_b),
            prm["b_re"].transpose(0, 2, 1), prm["b_im"].transpose(0, 2, 1),
            prm["c_re"], prm["c_im"]]
    g3 = lambda g: (g, 0, 0)
    specs = [pl.BlockSpec((gc, n_tok), lambda g: (g, 0))]
    specs += [pl.BlockSpec((1, 1, P), g3)] * 3 + [pl.BlockSpec((1, P, 1), g3)] * 3
    specs += [pl.BlockSpec((1, gc, P), g3)] * 4
    return pl.pallas_call(
        functools.partial(_s5_kernel, bsz=bsz, nchunk=nchunk),
        grid=(groups,),
        in_specs=specs,
        out_specs=pl.BlockSpec((gc, n_tok), lambda g: (g, 0)),
        out_shape=jax.ShapeDtypeStruct(ut.shape, F32),
        scratch_shapes=[pltpu.VMEM((n, n), BF16)],
        compiler_params=_cparams("parallel"),
        name="s5_core",
    )(*args)


def _gelu_tanh(x):
    return 0.5 * x * (1.0 + jnp.tanh(0.7978845608028654 * (x + 0.044715 * x * x * x)))


def _out_kernel(ya_ref, yb_ref, yc_ref, u_ref, x_ref, gt1_ref, sh2_ref, sc2_ref, g_ref, d_ref,
                gw_ref, gbias_ref, wo_ref, wrh_ref, wrl_ref, br_ref,
                x1_ref, h2_ref, ri_ref, rw_ref, hist_ref, *, da, db):
    yc = _gelu_tanh(yc_ref[...].T + d_ref[...] * u_ref[...].T)
    yc = yc * _sigmoid(_dot(yc, gw_ref[...]) + gbias_ref[...])
    mixed = (_dot(ya_ref[...], wo_ref[0:da, :]) + _dot(yb_ref[...], wo_ref[da:da + db, :])
             + _dot(yc, wo_ref[da + db:, :]))
    x1 = x_ref[...] + (1.0 + gt1_ref[0, 0]) * mixed
    x1_ref[...] = x1
    h2 = _rmsnorm(x1, g_ref[...]) * (1.0 + sc2_ref[0, 0]) + sh2_ref[0, 0]
    h2_ref[...] = h2

    hh, hl = _split(h2)
    logits = (jnp.dot(hh, wrh_ref[...], preferred_element_type=F32)
              + jnp.dot(hh, wrl_ref[...], preferred_element_type=F32)
              + jnp.dot(hl, wrh_ref[...], preferred_element_type=F32)) + br_ref[...]
    lane_i = lax.broadcasted_iota(jnp.int32, logits.shape, 1)
    lane = lane_i.astype(F32)
    big = float(LANES)
    neg = -jnp.inf
    is_g = lane_i < N_GROUPS
    lg = jnp.where(is_g, logits, neg)
    gmax = jnp.max(lg, axis=1, keepdims=True)
    gi = jnp.min(jnp.where(is_g & (lg == gmax), lane, big), axis=1, keepdims=True)
    gp = 1.0 / jnp.sum(jnp.where(is_g, jnp.exp(lg - gmax), 0.0), axis=1, keepdims=True)
    e_lane = lane_i - N_GROUPS
    grp_of_lane = lax.shift_right_arithmetic(e_lane, 3).astype(F32)
    in_grp = (e_lane >= 0) & (e_lane < N_EXPERTS) & (grp_of_lane == gi)
    l1 = jnp.where(in_grp, logits, neg)
    m1 = jnp.max(l1, axis=1, keepdims=True)
    i1 = jnp.min(jnp.where(in_grp & (l1 == m1), lane, big), axis=1, keepdims=True)
    rest = in_grp & (lane != i1)
    l2 = jnp.where(rest, logits, neg)
    m2 = jnp.max(l2, axis=1, keepdims=True)
    i2 = jnp.min(jnp.where(rest & (l2 == m2), lane, big), axis=1, keepdims=True)
    e2 = jnp.exp(m2 - m1)
    w1 = gp / (1.0 + e2)
    w2 = gp * e2 / (1.0 + e2)
    hot1 = (lane == i1).astype(F32)
    hot2 = (lane == i2).astype(F32)
    both = (hot1 + hot2).astype(BF16)
    tm = logits.shape[0]
    rr = lax.broadcasted_iota(jnp.int32, (tm, tm), 0)
    cc = lax.broadcasted_iota(jnp.int32, (tm, tm), 1)
    before = jnp.dot((cc < rr).astype(BF16), both, preferred_element_type=F32)
    rank1 = jnp.sum(before * hot1, axis=1, keepdims=True)
    rank2 = jnp.sum(before * hot2, axis=1, keepdims=True)
    hist_ref[0] = jnp.sum(hot1 + hot2, axis=0, keepdims=True)
    ids = jnp.where(lane_i == 0, i1, jnp.where(lane_i == 1, i2, jnp.where(lane_i == 2, rank1, rank2)))
    ri_ref[...] = ids.astype(jnp.int32)
    rw_ref[...] = jnp.where(lane_i == 0, w1, jnp.where(lane_i == 1, w2, 0.0))


def _out_proj(ya, yb, yc, u, x2d, mod_l, g_ffn, s5_d, glu_w, glu_b, w_out, w_rg, b_rg, w_re, b_re, seq):
    n_tok, d = x2d.shape
    da, db, dc = ya.shape[1], yb.shape[1], yc.shape[0]
    tm = _row_tile(seq, 512)
    per_b = seq // tm
    wr = jnp.zeros((d, LANES), F32).at[:, :N_GROUPS].set(w_rg).at[:, N_GROUPS:N_GROUPS + N_EXPERTS].set(w_re)
    wr_hi = wr.astype(BF16)
    wr_lo = (wr - wr_hi.astype(F32)).astype(BF16)
    br = jnp.zeros((1, LANES), F32).at[0, :N_GROUPS].set(b_rg).at[0, N_GROUPS:N_GROUPS + N_EXPERTS].set(b_re)
    row = lambda i: (i, 0)
    const = lambda i: (0, 0)
    modspec = lambda j: pl.BlockSpec((1, 1, 1, d), lambda i: (i // per_b, j, 0, 0))
    return pl.pallas_call(
        functools.partial(_out_kernel, da=da, db=db),
        grid=(n_tok // tm,),
        in_specs=[pl.BlockSpec((tm, da), row), pl.BlockSpec((tm, db), row),
                  pl.BlockSpec((dc, tm), lambda i: (0, i)), pl.BlockSpec((dc, tm), lambda i: (0, i)),
                  pl.BlockSpec((tm, d), row),
                  modspec(2), modspec(3), modspec(4),
                  pl.BlockSpec((1, d), const), pl.BlockSpec((1, dc), const),
                  pl.BlockSpec((dc, dc), const), pl.BlockSpec((1, dc), const),
                  pl.BlockSpec((d, d), const), pl.BlockSpec((d, LANES), const),
                  pl.BlockSpec((d, LANES), const), pl.BlockSpec((1, LANES), const)],
        out_specs=[pl.BlockSpec((tm, d), row), pl.BlockSpec((tm, d), row),
                   pl.BlockSpec((tm, LANES), row), pl.BlockSpec((tm, LANES), row),
                   pl.BlockSpec((1, 1, LANES), lambda i: (i, 0, 0))],
        out_shape=[jax.ShapeDtypeStruct((n_tok, d), F32), jax.ShapeDtypeStruct((n_tok, d), F32),
                   jax.ShapeDtypeStruct((n_tok, LANES), jnp.int32),
                   jax.ShapeDtypeStruct((n_tok, LANES), F32),
                   jax.ShapeDtypeStruct((n_tok // tm, 1, LANES), F32)],
        compiler_params=_cparams("parallel"),
        name="out_proj_router",
    )(ya, yb, yc, u, x2d, mod_l, mod_l, mod_l, g_ffn, s5_d.reshape(1, dc), glu_w.astype(BF16),
      glu_b.reshape(1, dc), w_out.astype(BF16), wr_hi, wr_lo, br)


ROW_UNROLL = 16


def _route(ri, hist, n_tok, tm):
    ntile = n_tok // tm
    h = hist.reshape(ntile, LANES)[:, N_GROUPS:N_GROUPS + N_EXPERTS].astype(jnp.int32)
    counts = jnp.sum(h, axis=0)
    pcounts = (counts + MOE_ROWS - 1) // MOE_ROWS * MOE_ROWS
    pends = jnp.cumsum(pcounts)
    base = (pends - pcounts)[None, :] + jnp.cumsum(h, axis=0) - h
    n_rows = 2 * n_tok + N_EXPERTS * MOE_ROWS
    nblk = n_rows // MOE_ROWS
    blk_start = jnp.arange(nblk, dtype=jnp.int32)[:, None] * MOE_ROWS
    blk_e = jnp.sum((pends[None, :] <= blk_start).astype(jnp.int32), axis=1)
    eid = (ri[:, 0:2] - N_GROUPS).reshape(ntile, tm, 2, 1)
    hot = eid == jnp.arange(N_EXPERTS, dtype=jnp.int32)
    dest = jnp.sum(jnp.where(hot, base[:, None, None, :], 0), axis=-1) + ri[:, 2:4].reshape(ntile, tm, 2)
    return (dest.reshape(n_tok, 2).astype(jnp.int32), blk_e.astype(jnp.int32), pends.astype(jnp.int32),
            pcounts.astype(jnp.int32), n_rows)


def _tile_rows(dest, tm):
    nblk = dest.shape[0] // tm
    return dest.reshape(nblk, tm, 2).transpose(0, 2, 1).reshape(nblk, 1, 2 * tm)


def _dispatch_kernel(pend_ref, pcnt_ref, d_ref, h_ref, xs_hbm, zbuf, hbuf, sem, zsem, *, tm, n_rows):
    i = pl.program_id(0)
    n = pl.num_programs(0)
    slot = i % 2

    @pl.when(i == 0)
    def _():
        zbuf[...] = jnp.zeros_like(zbuf)

        def zcopy(start):
            return pltpu.make_async_copy(zbuf, xs_hbm.at[pl.ds(start, MOE_ROWS)], zsem.at[0])

        first_tail = lax.shift_right_logical(pend_ref[N_EXPERTS - 1], MOE_ROWS.bit_length() - 1)
        nblk = n_rows // MOE_ROWS

        def tail_start(blk, carry):
            zcopy(pl.multiple_of(blk * MOE_ROWS, MOE_ROWS)).start()
            return carry

        def tail_wait(blk, carry):
            zcopy(0).wait()
            return carry

        for e in range(N_EXPERTS):
            @pl.when(pcnt_ref[e] > 0)
            def _():
                zcopy(pl.multiple_of(pend_ref[e] - MOE_ROWS, MOE_ROWS)).start()
        lax.fori_loop(first_tail, nblk, tail_start, 0)
        for e in range(N_EXPERTS):
            @pl.when(pcnt_ref[e] > 0)
            def _():
                zcopy(0).wait()
        lax.fori_loop(first_tail, nblk, tail_wait, 0)

    def body(g, carry):
        for u in range(ROW_UNROLL):
            r = g * ROW_UNROLL + u
            src = hbuf.at[slot, pl.ds(r, 1)]
            pltpu.make_async_copy(src, xs_hbm.at[pl.ds(d_ref[0, 0, r], 1)], sem.at[slot]).start(priority=0)
            pltpu.make_async_copy(src, xs_hbm.at[pl.ds(d_ref[0, 0, tm + r], 1)], sem.at[slot]).start(priority=1)
        return carry

    def drain(s):
        for _ in range(2):
            pltpu.make_async_copy(hbuf.at[s], xs_hbm.at[pl.ds(0, tm)], sem.at[s]).wait()

    hbuf[slot] = h_ref[...].reshape(tm, SUBLANES, LANES)
    lax.fori_loop(0, tm // ROW_UNROLL, body, 0)

    @pl.when(i > 0)
    def _():
        drain(1 - slot)

    @pl.when(i == n - 1)
    def _():
        drain(slot)


def _dispatch(h2, dest, pends, pcounts, n_rows, seq):
    n_tok, d = h2.shape
    tm = _row_tile(seq, 512)
    grid_spec = pltpu.PrefetchScalarGridSpec(
        num_scalar_prefetch=2,
        grid=(n_tok // tm,),
        in_specs=[pl.BlockSpec((1, 1, 2 * tm), lambda i, pe, pc: (i, 0, 0), memory_space=pltpu.SMEM),
                  pl.BlockSpec((tm, d), lambda i, pe, pc: (i, 0))],
        out_specs=pl.BlockSpec(memory_space=pl.ANY),
        scratch_shapes=[pltpu.VMEM((MOE_ROWS, SUBLANES, LANES), F32),
                        pltpu.VMEM((2, tm, SUBLANES, LANES), F32),
                        pltpu.SemaphoreType.DMA((2,)), pltpu.SemaphoreType.DMA((1,))],
    )
    assert d == SUBLANES * LANES
    return pl.pallas_call(
        functools.partial(_dispatch_kernel, tm=tm, n_rows=n_rows),
        grid_spec=grid_spec,
        out_shape=jax.ShapeDtypeStruct((n_rows, SUBLANES, LANES), F32),
        compiler_params=_cparams("arbitrary"),
        name="moe_dispatch",
    )(pends, pcounts, _tile_rows(dest, tm), h2)


def _moe_kernel(blk_e_ref, x_ref, w1_ref, w3_ref, w2_ref, y_ref, w1_sc, w3_sc, w2_sc):
    i = pl.program_id(0)
    e = blk_e_ref[i]
    used = e < N_EXPERTS

    @pl.when(used & ((i == 0) | (e != blk_e_ref[jnp.maximum(i - 1, 0)])))
    def _():
        w1_sc[...] = w1_ref[0, 0].astype(BF16)
        w3_sc[...] = w3_ref[0, 0].astype(BF16)
        w2_sc[...] = w2_ref[0, 0].astype(BF16)

    @pl.when(used)
    def _():
        rows = x_ref.shape[0]
        xb = x_ref[...].reshape(rows, SUBLANES * LANES).astype(BF16)
        act = (_silu(jnp.dot(xb, w1_sc[...], preferred_element_type=F32))
               * jnp.dot(xb, w3_sc[...], preferred_element_type=F32))
        y = jnp.dot(act.astype(BF16), w2_sc[...], preferred_element_type=F32)
        y_ref[...] = y.reshape(rows, SUBLANES, LANES)

    @pl.when(jnp.logical_not(used))
    def _():
        y_ref[...] = jnp.zeros_like(y_ref)


def _moe_experts(xs, blk_e, w1, w3, w2, layer):
    n_rows = xs.shape[0]
    d, d_exp = w1.shape[2], w1.shape[3]
    row_blk = pl.BlockSpec((MOE_ROWS, SUBLANES, LANES), lambda i, e: (i, 0, 0))
    wmap = lambda i, e: (layer, jnp.minimum(e[i], N_EXPERTS - 1), 0, 0)
    grid_spec = pltpu.PrefetchScalarGridSpec(
        num_scalar_prefetch=1,
        grid=(n_rows // MOE_ROWS,),
        in_specs=[row_blk,
                  pl.BlockSpec((1, 1, d, d_exp), wmap), pl.BlockSpec((1, 1, d, d_exp), wmap),
                  pl.BlockSpec((1, 1, d_exp, d), wmap)],
        out_specs=row_blk,
        scratch_shapes=[pltpu.VMEM((d, d_exp), BF16), pltpu.VMEM((d, d_exp), BF16),
                        pltpu.VMEM((d_exp, d), BF16)],
    )
    return pl.pallas_call(
        _moe_kernel,
        grid_spec=grid_spec,
        out_shape=jax.ShapeDtypeStruct((n_rows, SUBLANES, LANES), F32),
        compiler_params=_cparams("arbitrary"),
        name="moe_experts",
    )(blk_e, xs, w1, w3, w2)


def _comb_kernel(d_ref, d_next_ref, x_ref, rw_ref, gt_ref, y_hbm, g_ref, o_ref, ybuf, sem, *, tm, final):
    i = pl.program_id(0)
    n = pl.num_programs(0)
    slot = i % 2

    def start_all(idx_ref, s):
        def body(g, carry):
            for u in range(ROW_UNROLL):
                r = g * ROW_UNROLL + u
                pltpu.make_async_copy(y_hbm.at[pl.ds(idx_ref[0, 0, r], 1)], ybuf.at[s, pl.ds(r, 1)],
                                      sem.at[s]).start(priority=u % 2)
            return carry
        lax.fori_loop(0, 2 * tm // ROW_UNROLL, body, 0)

    @pl.when(i == 0)
    def _():
        start_all(d_ref, 0)

    @pl.when(i + 1 < n)
    def _():
        start_all(d_next_ref, 1 - slot)

    pltpu.make_async_copy(y_hbm.at[pl.ds(0, 2 * tm)], ybuf.at[slot], sem.at[slot]).wait()

    w = rw_ref[...]
    d = x_ref.shape[1]
    moe = (w[:, 0:1] * ybuf[slot, 0:tm].reshape(tm, d)
           + w[:, 1:2] * ybuf[slot, tm:2 * tm].reshape(tm, d))
    x2 = x_ref[...] + (1.0 + gt_ref[0, 0]) * moe
    o_ref[...] = _rmsnorm(x2, g_ref[...]) if final else x2


def _combine(x1, y_rows, dest, rw, mod_l, g_final, seq, final):
    n_tok, d = x1.shape
    tm = _row_tile(seq, 512)
    per_b = seq // tm
    nblk = n_tok // tm
    idx = _tile_rows(dest, tm)
    return pl.pallas_call(
        functools.partial(_comb_kernel, tm=tm, final=final),
        grid=(nblk,),
        in_specs=[pl.BlockSpec((1, 1, 2 * tm), lambda i: (i, 0, 0), memory_space=pltpu.SMEM),
                  pl.BlockSpec((1, 1, 2 * tm), lambda i: (jnp.minimum(i + 1, nblk - 1), 0, 0),
                               memory_space=pltpu.SMEM),
                  pl.BlockSpec((tm, d), lambda i: (i, 0)),
                  pl.BlockSpec((tm, LANES), lambda i: (i, 0)),
                  pl.BlockSpec((1, 1, 1, d), lambda i: (i // per_b, 5, 0, 0)),
                  pl.BlockSpec(memory_space=pl.ANY),
                  pl.BlockSpec((1, d), lambda i: (0, 0))],
        out_specs=pl.BlockSpec((tm, d), lambda i: (i, 0)),
        out_shape=jax.ShapeDtypeStruct((n_tok, d), F32),
        scratch_shapes=[pltpu.VMEM((2, 2 * tm, SUBLANES, LANES), F32), pltpu.SemaphoreType.DMA((2,))],
        compiler_params=_cparams("arbitrary"),
        name="moe_combine",
    )(idx, idx, x1, rw, mod_l, y_rows, g_final)


def kernel(x, c, ada_w, ada_b, norm_mix, norm_ffn, norm_final, w_in, w_out, mlstm_conv_w, mlstm_conv_b, mlstm_w_q, mlstm_w_k, mlstm_b_i, mlstm_b_f, mlstm_norm_w, rwkv_mu, rwkv_w0, rwkv_w_up, rwkv_a0, rwkv_a_up, rwkv_g_up, rwkv_k_k, rwkv_k_a, rwkv_r_k, rwkv_ln_w, rwkv_ln_b, rwkv_v0, rwkv_v_dn, rwkv_v_up, s5_a_re, s5_a_im, s5_log_dt, s5_b_re, s5_b_im, s5_c_re, s5_c_im, s5_d, s5_glu_w, s5_glu_b, moe_w_rg, moe_b_rg, moe_w_re, moe_b_re, moe_w1, moe_w3, moe_w2):
    bsz, seq, d = x.shape
    depth = ada_w.shape[0]
    n_tok = bsz * seq
    heads_a = mlstm_w_q.shape[1]
    da = heads_a * HEAD_DIM
    db = rwkv_w0.shape[1]
    dc = s5_d.shape[1]
    rw_cols = rwkv_mu.shape[1]
    assert seq % CHUNK == 0 and w_in.shape[2] == 3 * da + 2 * heads_a + rw_cols + dc
    assert 2 * HEAD_DIM == LANES and CHUNK == HEAD_DIM and EXPERTS_PER_GROUP == 8
    widths = (3 * da, 2 * LANES, rw_cols, dc)

    mod = _modulation(c, ada_w, ada_b).reshape(depth, bsz, 6, 1, d)
    xc = x.reshape(n_tok, d)
    v_first = None
    for l in range(depth):
        mod_l = mod[l]
        qkvo, gates, pcols, u = _in_proj(xc, mod_l, norm_mix[l].reshape(1, d), w_in, l, widths, da,
                                         heads_a, seq)

        ya = _mlstm(qkvo, gates, mlstm_conv_w[l], mlstm_conv_b[l], mlstm_w_q[l], mlstm_w_k[l],
                    mlstm_b_i[l], mlstm_b_f[l], mlstm_norm_w[l], bsz, seq)
        rprm = dict(mu=rwkv_mu[l], w0=rwkv_w0[l], w_up=rwkv_w_up[l], a0=rwkv_a0[l], a_up=rwkv_a_up[l],
                    g_up=rwkv_g_up[l], k_k=rwkv_k_k[l], k_a=rwkv_k_a[l], r_k=rwkv_r_k[l],
                    ln_w=rwkv_ln_w[l], ln_b=rwkv_ln_b[l])
        if l > 0:
            rprm.update(v0=rwkv_v0[l - 1], v_dn=rwkv_v_dn[l - 1], v_up=rwkv_v_up[l - 1])
        yb, v_first = _rwkv(pcols, v_first if l > 0 else None, rprm, bsz, seq)
        sprm = dict(a_re=s5_a_re[l], a_im=s5_a_im[l], log_dt=s5_log_dt[l], b_re=s5_b_re[l],
                    b_im=s5_b_im[l], c_re=s5_c_re[l], c_im=s5_c_im[l])
        yc = _s5_core(u, sprm, bsz, seq)

        x1, h2, ri, rw, hist = _out_proj(ya, yb, yc, u, xc, mod_l, norm_ffn[l].reshape(1, d), s5_d[l],
                                         s5_glu_w[l], s5_glu_b[l], w_out[l], moe_w_rg[l], moe_b_rg[l],
                                         moe_w_re[l], moe_b_re[l], seq)
        dest, blk_e, pends, pcounts, n_rows = _route(ri, hist, n_tok, n_tok // hist.shape[0])
        y_rows = _moe_experts(_dispatch(h2, dest, pends, pcounts, n_rows, seq), blk_e, moe_w1, moe_w3,
                              moe_w2, l)
        xc = _combine(x1, y_rows, dest, rw, mod_l, norm_final.reshape(1, d), seq, final=(l == depth - 1))
    return xc.reshape(bsz, seq, d).astype(x.dtype)
```
